```python
import math
import jax, jax.numpy as jnp
from jax import lax
import numpy as np

D_MODEL = 1024
BATCH = 32
SEQ = 2048
DEPTH = 2

HEAD_DIM = 64
NSA_HEADS = 4
DIFF_HEADS = 4
FOX_HEADS = 8
NSA_WIDTH = NSA_HEADS * HEAD_DIM
DIFF_WIDTH = DIFF_HEADS * HEAD_DIM
FOX_WIDTH = FOX_HEADS * HEAD_DIM
MIX_WIDTH = NSA_WIDTH + DIFF_WIDTH + FOX_WIDTH
NSA_KV_HEADS = 1
NSA_GROUP = NSA_HEADS // NSA_KV_HEADS
NSA_BRANCHES = 3
CMP_BLOCK = 32
CMP_STRIDE = 16
CMP_HIDDEN = 256
SLC_BLOCK = 64
SLC_TOPK = 16
WINDOW = 512
DIFF_QK_DIM = HEAD_DIM // 2
DIFF_V_DIM = HEAD_DIM
D_FF = 2752
PLE_DIM = 256
ROPE_THETA = 10000.0
QUERY_BLOCK = 128
SLC_QUERY_BLOCK = 32
LN_EPS = 1e-5
NEG_BIG = -1e30
DEEPNORM_ALPHA = (2.0 * DEPTH) ** 0.25
DEEPNORM_BETA = (8.0 * DEPTH) ** -0.25
IN_SPLITS = (
    NSA_WIDTH,
    6 * NSA_KV_HEADS * HEAD_DIM,
    NSA_BRANCHES * NSA_HEADS,
    2 * DIFF_HEADS * DIFF_QK_DIM,
    2 * DIFF_HEADS * DIFF_QK_DIM,
    DIFF_HEADS * DIFF_V_DIM,
    FOX_WIDTH, FOX_WIDTH, FOX_WIDTH,
    FOX_HEADS,
)
IN_COLS = sum(IN_SPLITS)

kernel_name = "hymba_nsa_diff_fox_macaron_deepnorm"


def layer_norm(x, g, b):
    xf = x.astype(jnp.float32)
    mu = jnp.mean(xf, axis=-1, keepdims=True)
    var = jnp.mean(jnp.square(xf - mu), axis=-1, keepdims=True)
    return ((xf - mu) * lax.rsqrt(var + LN_EPS) * g.astype(jnp.float32) + b.astype(jnp.float32)).astype(x.dtype)


def rope(x, pos):
    d = x.shape[-1]
    inv = ROPE_THETA ** (-jnp.arange(0, d, 2, dtype=jnp.float32) / d)
    ang = pos.astype(jnp.float32)[:, None] * inv[None, :]
    cos = jnp.cos(ang)[:, None, :]
    sin = jnp.sin(ang)[:, None, :]
    xf = x.astype(jnp.float32)
    x1, x2 = xf[..., : d // 2], xf[..., d // 2:]
    return jnp.concatenate([x1 * cos - x2 * sin, x2 * cos + x1 * sin], axis=-1).astype(x.dtype)


def swiglu(x, w_gate, w_up, w_down):
    return (jax.nn.silu(x @ w_gate) * (x @ w_up)) @ w_down


def masked_softmax(s, mask):
    p = jax.nn.softmax(jnp.where(mask, s.astype(jnp.float32), NEG_BIG), axis=-1)
    return jnp.where(mask, p, 0.0)


def sweep(fn, n_blocks):
    out = jnp.moveaxis(lax.map(fn, jnp.arange(n_blocks)), 0, 1)
    return out.reshape((out.shape[0], -1) + out.shape[3:])


def cmp_to_slc_matrix(n_cmp, n_slc):
    c0 = np.arange(n_cmp) * CMP_STRIDE
    s0 = np.arange(n_slc) * SLC_BLOCK
    m = (c0[:, None] < s0[None, :] + SLC_BLOCK) & (c0[:, None] + CMP_BLOCK > s0[None, :])
    return jnp.asarray(m.astype(np.float32))


def nsa_attention(q, k_cmp, v_cmp, k_slc, v_slc, k_win, v_win, gates, pos_k, pos_v, phi_k1, phi_k2, phi_v1, phi_v2):
    B, S = q.shape[0], q.shape[1]
    scale = HEAD_DIM ** -0.5
    t = jnp.arange(S)
    n_cmp = (S - CMP_BLOCK) // CMP_STRIDE + 1
    starts = jnp.arange(n_cmp) * CMP_STRIDE
    idx = starts[:, None] + jnp.arange(CMP_BLOCK)[None, :]

    def compress(kv, pe, w1, w2):
        blk = kv[:, idx] + pe[:, None, :]
        blk = jnp.transpose(blk, (0, 1, 3, 2, 4)).reshape(B, n_cmp, NSA_KV_HEADS, CMP_BLOCK * HEAD_DIM)
        return jax.nn.gelu(blk @ w1) @ w2

    block_end = starts + CMP_BLOCK - 1
    kc = rope(compress(k_cmp, pos_k, phi_k1, phi_k2), block_end)
    vc = compress(v_cmp, pos_v, phi_v1, phi_v2)
    mask_c = block_end[None, :] <= t[:, None]
    s_c = jnp.einsum('bthgd,bchd->bhgtc', q, kc) * scale
    p_c = masked_softmax(s_c, mask_c)
    o_cmp = jnp.einsum('bhgtc,bchd->bthgd', p_c.astype(vc.dtype), vc)
    n_slc = S // SLC_BLOCK
    top = min(SLC_TOPK, n_slc)
    imp = jnp.sum(p_c, axis=2) @ cmp_to_slc_matrix(n_cmp, n_slc)
    j = jnp.arange(n_slc)[None, :]
    blk_t = (t // SLC_BLOCK)[:, None]
    forced = (j == 0) | (j == blk_t) | (j == blk_t - 1)
    valid = j * SLC_BLOCK <= t[:, None]
    score = jnp.where(forced, 1e9, jnp.where(valid, imp, -1.0))
    _, sel = lax.top_k(score, top)
    sel = jnp.transpose(sel, (0, 2, 1, 3))
    k_blk = jnp.transpose(k_slc.reshape(B, n_slc, SLC_BLOCK, NSA_KV_HEADS, HEAD_DIM), (0, 3, 1, 2, 4))
    v_blk = jnp.transpose(v_slc.reshape(B, n_slc, SLC_BLOCK, NSA_KV_HEADS, HEAD_DIM), (0, 3, 1, 2, 4))
    b_idx = jnp.arange(B)[:, None, None, None]
    h_idx = jnp.arange(NSA_KV_HEADS)[None, None, :, None]

    def slc_block(i):
        t0 = i * SLC_QUERY_BLOCK
        qb = lax.dynamic_slice_in_dim(q, t0, SLC_QUERY_BLOCK, axis=1)
        sb = lax.dynamic_slice_in_dim(sel, t0, SLC_QUERY_BLOCK, axis=1)
        kg = k_blk[b_idx, h_idx, sb]
        vg = v_blk[b_idx, h_idx, sb].reshape(B, SLC_QUERY_BLOCK, NSA_KV_HEADS, top * SLC_BLOCK, HEAD_DIM)
        key_pos = sb[..., None] * SLC_BLOCK + jnp.arange(SLC_BLOCK)
        tq = t0 + jnp.arange(SLC_QUERY_BLOCK)
        mask = (key_pos <= tq[None, :, None, None, None]).reshape(B, SLC_QUERY_BLOCK, NSA_KV_HEADS, 1, top * SLC_BLOCK)
        s = jnp.einsum('bqhgd,bqhkld->bqhgkl', qb, kg) * scale
        p = masked_softmax(s.reshape(B, SLC_QUERY_BLOCK, NSA_KV_HEADS, NSA_GROUP, top * SLC_BLOCK), mask)
        return jnp.einsum('bqhgn,bqhnd->bqhgd', p.astype(vg.dtype), vg)

    o_slc = sweep(slc_block, S // SLC_QUERY_BLOCK)
    k_pad = jnp.pad(k_win, ((0, 0), (WINDOW, 0), (0, 0), (0, 0)))
    v_pad = jnp.pad(v_win, ((0, 0), (WINDOW, 0), (0, 0), (0, 0)))

    def win_block(i):
        t0 = i * QUERY_BLOCK
        qb = lax.dynamic_slice_in_dim(q, t0, QUERY_BLOCK, axis=1)
        kb = lax.dynamic_slice_in_dim(k_pad, t0, WINDOW + QUERY_BLOCK, axis=1)
        vb = lax.dynamic_slice_in_dim(v_pad, t0, WINDOW + QUERY_BLOCK, axis=1)
        key_pos = t0 - WINDOW + jnp.arange(WINDOW + QUERY_BLOCK)
        tq = (t0 + jnp.arange(QUERY_BLOCK))[:, None]
        mask = (key_pos[None, :] <= tq) & (key_pos[None, :] > tq - WINDOW) & (key_pos[None, :] >= 0)
        s = jnp.einsum('bqhgd,bkhd->bhgqk', qb, kb) * scale
        p = masked_softmax(s, mask)
        return jnp.einsum('bhgqk,bkhd->bqhgd', p.astype(vb.dtype), vb)

    o_win = sweep(win_block, S // QUERY_BLOCK)
    out = gates[..., 0:1] * o_cmp + gates[..., 1:2] * o_slc + gates[..., 2:3] * o_win
    return out.reshape(B, S, NSA_WIDTH)


def diff_attention(q, k, v, lam_params, subln_g, lambda_init):
    B, S = q.shape[0], q.shape[1]
    lp = lam_params.astype(jnp.float32)
    lam = jnp.exp(jnp.sum(lp[0] * lp[1])) - jnp.exp(jnp.sum(lp[2] * lp[3])) + lambda_init
    scale = DIFF_QK_DIM ** -0.5
    kpos = jnp.arange(S)

    def block(i):
        t0 = i * QUERY_BLOCK
        qb = lax.dynamic_slice_in_dim(q, t0, QUERY_BLOCK, axis=1)
        mask = kpos[None, :] <= (t0 + jnp.arange(QUERY_BLOCK))[:, None]
        s = jnp.einsum('bqhcd,bkhcd->bhcqk', qb, k) * scale
        p = masked_softmax(s, mask)
        a = p[:, :, 0] - lam * p[:, :, 1]
        return jnp.einsum('bhqk,bkhd->bqhd', a.astype(v.dtype), v)

    o = sweep(block, S // QUERY_BLOCK).astype(jnp.float32)
    o = o * lax.rsqrt(jnp.mean(jnp.square(o), axis=-1, keepdims=True) + LN_EPS) * subln_g.astype(jnp.float32)
    o = o * (1.0 - lambda_init)
    return o.astype(v.dtype).reshape(B, S, DIFF_WIDTH)


def forgetting_attention(q, k, v, f_logit):
    B, S = q.shape[0], q.shape[1]
    c = jnp.cumsum(jax.nn.log_sigmoid(f_logit.astype(jnp.float32)), axis=1)
    cT = jnp.transpose(c, (0, 2, 1))
    scale = HEAD_DIM ** -0.5
    kpos = jnp.arange(S)

    def block(i):
        t0 = i * QUERY_BLOCK
        qb = lax.dynamic_slice_in_dim(q, t0, QUERY_BLOCK, axis=1)
        cq = lax.dynamic_slice_in_dim(cT, t0, QUERY_BLOCK, axis=2)
        mask = kpos[None, :] <= (t0 + jnp.arange(QUERY_BLOCK))[:, None]
        s = jnp.einsum('bqhd,bkhd->bhqk', qb, k).astype(jnp.float32) * scale + (cq[..., :, None] - cT[..., None, :])
        p = masked_softmax(s, mask)
        return jnp.einsum('bhqk,bkhd->bqhd', p.astype(v.dtype), v)

    return sweep(block, S // QUERY_BLOCK).reshape(B, S, FOX_WIDTH)


def hybrid_mixer(x, w_in, fox_b_f, nsa_pos_k, nsa_pos_v, nsa_phi_k1, nsa_phi_k2, nsa_phi_v1, nsa_phi_v2,
                 diff_lambda, diff_subln_g, w_out, lambda_init):
    B, S, _ = x.shape
    pos = jnp.arange(S, dtype=jnp.int32)
    proj = x @ w_in
    offsets = np.cumsum(IN_SPLITS)[:-1].tolist()
    nsa_q, nsa_kv, nsa_g, diff_q, diff_k, diff_v, fox_q, fox_k, fox_v, fox_f = jnp.split(proj, offsets, axis=-1)
    q = rope(nsa_q.reshape(B, S, NSA_HEADS, HEAD_DIM), pos).reshape(B, S, NSA_KV_HEADS, NSA_GROUP, HEAD_DIM)
    kv = nsa_kv.reshape(B, S, 6, NSA_KV_HEADS, HEAD_DIM)
    gates = jax.nn.sigmoid(nsa_g.reshape(B, S, NSA_KV_HEADS, NSA_GROUP, NSA_BRANCHES))
    o_nsa = nsa_attention(q, kv[:, :, 0], kv[:, :, 1], rope(kv[:, :, 2], pos), kv[:, :, 3],
                          rope(kv[:, :, 4], pos), kv[:, :, 5], gates,
                          nsa_pos_k, nsa_pos_v, nsa_phi_k1, nsa_phi_k2, nsa_phi_v1, nsa_phi_v2)
    dq = rope(diff_q.reshape(B, S, 2 * DIFF_HEADS, DIFF_QK_DIM), pos).reshape(B, S, DIFF_HEADS, 2, DIFF_QK_DIM)
    dk = rope(diff_k.reshape(B, S, 2 * DIFF_HEADS, DIFF_QK_DIM), pos).reshape(B, S, DIFF_HEADS, 2, DIFF_QK_DIM)
    o_diff = diff_attention(dq, dk, diff_v.reshape(B, S, DIFF_HEADS, DIFF_V_DIM), diff_lambda, diff_subln_g, lambda_init)
    o_fox = forgetting_attention(fox_q.reshape(B, S, FOX_HEADS, HEAD_DIM), fox_k.reshape(B, S, FOX_HEADS, HEAD_DIM),
                                 fox_v.reshape(B, S, FOX_HEADS, HEAD_DIM), fox_f + fox_b_f)
    return jnp.concatenate([o_nsa, o_diff, o_fox], axis=-1) @ w_out


def setup_inputs(seed: int = 0) -> dict:
    key = jax.random.key(seed)
    ks = iter(jax.random.split(key, 32))
    L = DEPTH

    def nrm(shape, scale):
        return jax.random.normal(next(ks), shape, jnp.float32) * scale

    return {
        "x": nrm((BATCH, SEQ, D_MODEL), 1.0),
        "p": nrm((DEPTH, BATCH, SEQ, PLE_DIM), 1.0),
        "ln_g": 1.0 + nrm((L, 3, D_MODEL), 0.01),
        "ln_b": nrm((L, 3, D_MODEL), 0.01),
        "ffn1_w_gate": nrm((L, D_MODEL, D_FF), D_MODEL ** -0.5),
        "ffn1_w_up": nrm((L, D_MODEL, D_FF), D_MODEL ** -0.5),
        "ffn1_w_down": nrm((L, D_FF, D_MODEL), DEEPNORM_BETA * D_FF ** -0.5),
        "ffn2_w_gate": nrm((L, D_MODEL, D_FF), D_MODEL ** -0.5),
        "ffn2_w_up": nrm((L, D_MODEL, D_FF), D_MODEL ** -0.5),
        "ffn2_w_down": nrm((L, D_FF, D_MODEL), DEEPNORM_BETA * D_FF ** -0.5),
        "w_in": nrm((L, D_MODEL, IN_COLS), D_MODEL ** -0.5),
        "fox_b_f": 2.0 + nrm((L, FOX_HEADS), 0.1),
        "nsa_pos_k": nrm((L, CMP_BLOCK, HEAD_DIM), 0.02),
        "nsa_pos_v": nrm((L, CMP_BLOCK, HEAD_DIM), 0.02),
        "nsa_phi_k1": nrm((L, CMP_BLOCK * HEAD_DIM, CMP_HIDDEN), (CMP_BLOCK * HEAD_DIM) ** -0.5),
        "nsa_phi_k2": nrm((L, CMP_HIDDEN, HEAD_DIM), CMP_HIDDEN ** -0.5),
        "nsa_phi_v1": nrm((L, CMP_BLOCK * HEAD_DIM, CMP_HIDDEN), (CMP_BLOCK * HEAD_DIM) ** -0.5),
        "nsa_phi_v2": nrm((L, CMP_HIDDEN, HEAD_DIM), CMP_HIDDEN ** -0.5),
        "diff_lambda": nrm((L, 4, DIFF_QK_DIM), 0.1),
        "diff_subln_g": 1.0 + nrm((L, DIFF_V_DIM), 0.01),
        "w_out": nrm((L, MIX_WIDTH, D_MODEL), DEEPNORM_BETA * MIX_WIDTH ** -0.5),
        "ple_w_gate": nrm((L, D_MODEL, D_MODEL), D_MODEL ** -0.5),
        "ple_b_gate": nrm((L, D_MODEL), 0.01),
        "ple_w_proj": nrm((L, PLE_DIM, D_MODEL), PLE_DIM ** -0.5),
    }


def reference(x, p, ln_g, ln_b, ffn1_w_gate, ffn1_w_up, ffn1_w_down, ffn2_w_gate, ffn2_w_up, ffn2_w_down,
              w_in, fox_b_f, nsa_pos_k, nsa_pos_v, nsa_phi_k1, nsa_phi_k2, nsa_phi_v1, nsa_phi_v2,
              diff_lambda, diff_subln_g, w_out, ple_w_gate, ple_b_gate, ple_w_proj):
    for i in range(DEPTH):
        lambda_init = 0.8 - 0.6 * math.exp(-0.3 * i)
        h = 0.5 * swiglu(x, ffn1_w_gate[i], ffn1_w_up[i], ffn1_w_down[i])
        x = layer_norm(DEEPNORM_ALPHA * x + h, ln_g[i, 0], ln_b[i, 0])
        h = hybrid_mixer(x, w_in[i], fox_b_f[i], nsa_pos_k[i], nsa_pos_v[i], nsa_phi_k1[i], nsa_phi_k2[i],
                         nsa_phi_v1[i], nsa_phi_v2[i], diff_lambda[i], diff_subln_g[i], w_out[i], lambda_init)
        x = layer_norm(DEEPNORM_ALPHA * x + h, ln_g[i, 1], ln_b[i, 1])
        h = 0.5 * swiglu(x, ffn2_w_gate[i], ffn2_w_up[i], ffn2_w_down[i])
        x = layer_norm(DEEPNORM_ALPHA * x + h, ln_g[i, 2], ln_b[i, 2])
        x = x + jax.nn.sigmoid(x @ ple_w_gate[i] + ple_b_gate[i]) * (p[i] @ ple_w_proj[i])
    return x
```

```python
import functools
import math

import numpy as np
import jax
import jax.numpy as jnp
from jax import lax
from jax.experimental import pallas as pl
from jax.experimental.pallas import tpu as pltpu

F32 = jnp.float32
BF16 = jnp.bfloat16

D_MODEL = 1024
HEAD_DIM = 64
NSA_HEADS = 4
DIFF_HEADS = 4
FOX_HEADS = 8
NSA_WIDTH = NSA_HEADS * HEAD_DIM
DIFF_WIDTH = DIFF_HEADS * HEAD_DIM
FOX_WIDTH = FOX_HEADS * HEAD_DIM
CMP_BLOCK = 32
CMP_STRIDE = 16
CMP_HIDDEN = 256
SLC_BLOCK = 64
SLC_TOPK = 16
WINDOW = 512
DIFF_QK_DIM = HEAD_DIM // 2
D_FF = 2752
PLE_DIM = 256
ROPE_THETA = 10000.0
LN_EPS = 1e-5
NEG_BIG = -1e30
DEPTH = 2
DEEPNORM_ALPHA = (2.0 * DEPTH) ** 0.25

LANES = 128
MXU_COLS = 256
VMEM_LIMIT_BYTES = 56 * 1024 * 1024

FF_CHUNK = MXU_COLS
D_FF_PAD = ((D_FF + FF_CHUNK - 1) // FF_CHUNK) * FF_CHUNK
FF_CHUNKS = D_FF_PAD // FF_CHUNK

ROW_TILE = 512
ATT_TILE = 512
NSA_Q_TILE = 256
NSA_K_TILE = 256
CUM_TILE = 256

ROPE_COLS = NSA_WIDTH + 2 * HEAD_DIM + 2 * DIFF_WIDTH
PLAIN_COLS = 4 * HEAD_DIM + DIFF_WIDTH + 3 * FOX_WIDTH
GATE_COLS = LANES
N_NSA_GATES = 3 * NSA_HEADS
NSA_GATE_OFF = FOX_HEADS
N_CMP = 128
N_SLC = 32


def _cparams(*sem):
    return pltpu.CompilerParams(dimension_semantics=sem, vmem_limit_bytes=VMEM_LIMIT_BYTES)


def _dot(a, b):
    return jnp.dot(a, b, preferred_element_type=F32)


def _dot_nt(a, b):
    return lax.dot_general(a, b, (((1,), (1,)), ((), ())), preferred_element_type=F32)


def _sigmoid(z):
    return 1.0 / (1.0 + jnp.exp(-z))


def _layer_norm(y, g, b):
    mu = jnp.mean(y, axis=-1, keepdims=True)
    yc = y - mu
    var = jnp.mean(yc * yc, axis=-1, keepdims=True)
    return yc * lax.rsqrt(var + LN_EPS) * g + b


def _ffn_kernel(*refs, with_ple):
    if with_ple:
        (x_ref, wg_ref, wu_ref, wd_ref, lng_ref, lnb_ref, p_ref, pwg_ref, pbg_ref, pwp_ref,
         o_ref, xb_ref, acc_ref) = refs
    else:
        x_ref, wg_ref, wu_ref, wd_ref, lng_ref, lnb_ref, o_ref, xb_ref, acc_ref = refs
    xb_ref[...] = x_ref[...].astype(BF16)
    acc_ref[...] = jnp.zeros_like(acc_ref)

    def chunk(c, carry):
        xb = xb_ref[...]
        g = _dot(xb, wg_ref[c])
        u = _dot(xb, wu_ref[c])
        h = (g * _sigmoid(g)) * u
        acc_ref[...] += _dot(h.astype(BF16), wd_ref[c])
        return carry

    lax.fori_loop(0, FF_CHUNKS, chunk, 0)
    y = DEEPNORM_ALPHA * x_ref[...] + 0.5 * acc_ref[...]
    z = _layer_norm(y, lng_ref[...], lnb_ref[...])
    if with_ple:
        gate = _sigmoid(_dot(z.astype(BF16), pwg_ref[...]) + pbg_ref[...])
        z = z + gate * _dot(p_ref[...].astype(BF16), pwp_ref[...])
    o_ref[...] = z


def _ffn_ln(x, wg, wu, wd, ln_g, ln_b, ple=None):
    t = x.shape[0]
    tm = ROW_TILE
    row = lambda i: (i, 0)
    full2 = lambda i: (0, 0)
    full3 = lambda i: (0, 0, 0)
    in_specs = [
        pl.BlockSpec((tm, D_MODEL), row),
        pl.BlockSpec((FF_CHUNKS, D_MODEL, FF_CHUNK), full3),
        pl.BlockSpec((FF_CHUNKS, D_MODEL, FF_CHUNK), full3),
        pl.BlockSpec((FF_CHUNKS, FF_CHUNK, D_MODEL), full3),
        pl.BlockSpec((1, D_MODEL), full2),
        pl.BlockSpec((1, D_MODEL), full2),
    ]
    args = [x, wg, wu, wd, ln_g, ln_b]
    if ple is not None:
        p, pwg, pbg, pwp = ple
        in_specs += [
            pl.BlockSpec((tm, PLE_DIM), row),
            pl.BlockSpec((D_MODEL, D_MODEL), full2),
            pl.BlockSpec((1, D_MODEL), full2),
            pl.BlockSpec((PLE_DIM, D_MODEL), full2),
        ]
        args += [p, pwg, pbg, pwp]
    return pl.pallas_call(
        functools.partial(_ffn_kernel, with_ple=ple is not None),
        grid=(t // tm,),
        in_specs=in_specs,
        out_specs=pl.BlockSpec((tm, D_MODEL), row),
        out_shape=jax.ShapeDtypeStruct((t, D_MODEL), F32),
        scratch_shapes=[pltpu.VMEM((tm, D_MODEL), BF16), pltpu.VMEM((tm, D_MODEL), F32)],
        compiler_params=_cparams("parallel"),
        name="ffn_ln_ple" if ple is not None else "ffn_ln",
    )(*args)


def _in_proj_kernel(x_ref, wa_ref, war_ref, wb_ref, wc_ref, cos_ref, sin_ref, bc_ref,
                    nq_ref, kslc_ref, kwin_ref, dq_ref, dk_ref,
                    kcmp_ref, vcmp_ref, vslc_ref, vwin_ref, dv_ref, fq_ref, fk_ref, fv_ref,
                    gate_ref, fls_ref):
    xb = x_ref[0].astype(BF16)
    ra = _dot(xb, wa_ref[...]) * cos_ref[...] + _dot(xb, war_ref[...]) * sin_ref[...]
    ra = ra.astype(BF16)
    for h in range(NSA_HEADS):
        nq_ref[0, h] = ra[:, h * HEAD_DIM:(h + 1) * HEAD_DIM]
    off = NSA_WIDTH
    kslc_ref[0] = ra[:, off:off + HEAD_DIM]
    kwin_ref[0] = ra[:, off + HEAD_DIM:off + 2 * HEAD_DIM]
    off += 2 * HEAD_DIM
    for h in range(2 * DIFF_HEADS):
        dq_ref[0, h] = ra[:, off + h * DIFF_QK_DIM:off + (h + 1) * DIFF_QK_DIM]
    off += DIFF_WIDTH
    for h in range(2 * DIFF_HEADS):
        dk_ref[0, h] = ra[:, off + h * DIFF_QK_DIM:off + (h + 1) * DIFF_QK_DIM]

    pb = _dot(xb, wb_ref[...]).astype(BF16)
    kcmp_ref[0] = pb[:, 0:HEAD_DIM]
    vcmp_ref[0] = pb[:, HEAD_DIM:2 * HEAD_DIM]
    vslc_ref[0] = pb[:, 2 * HEAD_DIM:3 * HEAD_DIM]
    vwin_ref[0] = pb[:, 3 * HEAD_DIM:4 * HEAD_DIM]
    off = 4 * HEAD_DIM
    for h in range(DIFF_HEADS):
        dv_ref[0, h] = pb[:, off + h * HEAD_DIM:off + (h + 1) * HEAD_DIM]
    off += DIFF_WIDTH
    for h in range(FOX_HEADS):
        fq_ref[0, h] = pb[:, off + h * HEAD_DIM:off + (h + 1) * HEAD_DIM]
    off += FOX_WIDTH
    for h in range(FOX_HEADS):
        fk_ref[0, h] = pb[:, off + h * HEAD_DIM:off + (h + 1) * HEAD_DIM]
    off += FOX_WIDTH
    for h in range(FOX_HEADS):
        fv_ref[0, h] = pb[:, off + h * HEAD_DIM:off + (h + 1) * HEAD_DIM]

    pc = _dot(xb, wc_ref[...])
    gate_ref[0] = _sigmoid(pc)
    z = pc + bc_ref[...]
    fls_ref[0] = jnp.minimum(z, 0.0) - jnp.log(1.0 + jnp.exp(-jnp.abs(z)))


def _in_proj(x, wa, war, wb, wc, cos, sin, bc):
    b, s, _ = x.shape
    tm = ROW_TILE
    grid = (b, s // tm)
    w2 = lambda i, j: (0, 0)
    seq = lambda i, j: (j, 0)
    tok = lambda i, j: (i, j, 0)
    head = lambda i, j: (i, 0, j, 0)
    in_specs = [
        pl.BlockSpec((1, tm, D_MODEL), tok),
        pl.BlockSpec((D_MODEL, ROPE_COLS), w2),
        pl.BlockSpec((D_MODEL, ROPE_COLS), w2),
        pl.BlockSpec((D_MODEL, PLAIN_COLS), w2),
        pl.BlockSpec((D_MODEL, GATE_COLS), w2),
        pl.BlockSpec((tm, ROPE_COLS), seq),
        pl.BlockSpec((tm, ROPE_COLS), seq),
        pl.BlockSpec((1, GATE_COLS), w2),
    ]

    def tok_out(width, dtype=BF16):
        return pl.BlockSpec((1, tm, width), tok), jax.ShapeDtypeStruct((b, s, width), dtype)

    def head_out(heads, width):
        return pl.BlockSpec((1, heads, tm, width), head), jax.ShapeDtypeStruct((b, heads, s, width), BF16)

    outs = [
        head_out(NSA_HEADS, HEAD_DIM),
        tok_out(HEAD_DIM), tok_out(HEAD_DIM),
        head_out(2 * DIFF_HEADS, DIFF_QK_DIM),
        head_out(2 * DIFF_HEADS, DIFF_QK_DIM),
        tok_out(HEAD_DIM), tok_out(HEAD_DIM), tok_out(HEAD_DIM), tok_out(HEAD_DIM),
        head_out(DIFF_HEADS, HEAD_DIM),
        head_out(FOX_HEADS, HEAD_DIM), head_out(FOX_HEADS, HEAD_DIM), head_out(FOX_HEADS, HEAD_DIM),
        tok_out(GATE_COLS, F32), tok_out(GATE_COLS, F32),
    ]
    return pl.pallas_call(
        _in_proj_kernel,
        grid=grid,
        in_specs=in_specs,
        out_specs=[o[0] for o in outs],
        out_shape=[o[1] for o in outs],
        compiler_params=_cparams("parallel", "parallel"),
        name="in_proj",
    )(x, wa, war, wb, wc, cos, sin, bc)


def _gelu_tanh(x):
    return 0.5 * x * (1.0 + jnp.tanh(math.sqrt(2.0 / math.pi) * (x + 0.044715 * (x * x * x))))


def _compress_kernel(k16_ref, v16_ref, pek_ref, pev_ref, wk1_ref, wk2_ref, wk2r_ref, wv1_ref, wv2_ref,
                     cos_ref, sin_ref, kc_ref, vc_ref):
    def hidden(x16, pe_ref, w1_ref):
        top = _dot(x16, w1_ref[0])
        bot = _dot(x16, w1_ref[1])
        bias = _dot(pe_ref[0], w1_ref[0]) + _dot(pe_ref[1], w1_ref[1])
        bot = pltpu.roll(bot, N_CMP - 1, 0)
        return _gelu_tanh(top + bot + bias[0:1, :]).astype(BF16)

    hk = hidden(k16_ref[0], pek_ref, wk1_ref)
    kc = _dot(hk, wk2_ref[...]) * cos_ref[...] + _dot(hk, wk2r_ref[...]) * sin_ref[...]
    kc_ref[0] = kc.astype(BF16)
    hv = hidden(v16_ref[0], pev_ref, wv1_ref)
    vc_ref[0] = _dot(hv, wv2_ref[...]).astype(BF16)


def _compress(k16, v16, pek, pev, wk1, wk2, wk2r, wv1, wv2, cos_c, sin_c):
    b = k16.shape[0]
    half = CMP_STRIDE * HEAD_DIM
    bat = lambda i: (i, 0, 0)
    c2 = lambda i: (0, 0)
    c3 = lambda i: (0, 0, 0)
    in_specs = [
        pl.BlockSpec((1, N_CMP, half), bat),
        pl.BlockSpec((1, N_CMP, half), bat),
        pl.BlockSpec((2, 8, half), c3),
        pl.BlockSpec((2, 8, half), c3),
        pl.BlockSpec((2, half, CMP_HIDDEN), c3),
        pl.BlockSpec((CMP_HIDDEN, HEAD_DIM), c2),
        pl.BlockSpec((CMP_HIDDEN, HEAD_DIM), c2),
        pl.BlockSpec((2, half, CMP_HIDDEN), c3),
        pl.BlockSpec((CMP_HIDDEN, HEAD_DIM), c2),
        pl.BlockSpec((N_CMP, HEAD_DIM), c2),
        pl.BlockSpec((N_CMP, HEAD_DIM), c2),
    ]
    out = pl.BlockSpec((1, N_CMP, HEAD_DIM), bat)
    shp = jax.ShapeDtypeStruct((b, N_CMP, HEAD_DIM), BF16)
    return pl.pallas_call(
        _compress_kernel,
        grid=(b,),
        in_specs=in_specs,
        out_specs=[out, out],
        out_shape=[shp, shp],
        compiler_params=_cparams("parallel"),
        name="nsa_compress",
    )(k16, v16, pek, pev, wk1, wk2, wk2r, wv1, wv2, cos_c, sin_c)


def _decay_kernel(fls_ref, tri_ref, ccol_ref, crow_ref, *, n_tiles):
    carry = jnp.zeros((1, LANES), F32)
    for i in range(n_tiles):
        blk = fls_ref[0, i * CUM_TILE:(i + 1) * CUM_TILE, :]
        c = jnp.dot(tri_ref[...], blk, preferred_element_type=F32, precision=lax.Precision.HIGHEST) + carry
        ccol_ref[0, i * CUM_TILE:(i + 1) * CUM_TILE, :] = c
        crow_ref[0, :, i * CUM_TILE:(i + 1) * CUM_TILE] = c.T[0:FOX_HEADS, :]
        carry = c[CUM_TILE - 1:CUM_TILE, :]


def _decay(fls, tri):
    b, s, _ = fls.shape
    return pl.pallas_call(
        functools.partial(_decay_kernel, n_tiles=s // CUM_TILE),
        grid=(b,),
        in_specs=[pl.BlockSpec((1, s, LANES), lambda i: (i, 0, 0)),
                  pl.BlockSpec((CUM_TILE, CUM_TILE), lambda i: (0, 0))],
        out_specs=[pl.BlockSpec((1, s, LANES), lambda i: (i, 0, 0)),
                   pl.BlockSpec((1, FOX_HEADS, s), lambda i: (i, 0, 0))],
        out_shape=[jax.ShapeDtypeStruct((b, s, LANES), F32), jax.ShapeDtypeStruct((b, FOX_HEADS, s), F32)],
        compiler_params=_cparams("parallel"),
        name="fox_decay",
    )(fls, tri)


def _flash_init(m_ref, l_ref, acc_ref):
    m_ref[...] = jnp.full(m_ref.shape, NEG_BIG, F32)
    l_ref[...] = jnp.zeros(l_ref.shape, F32)
    acc_ref[...] = jnp.zeros(acc_ref.shape, F32)


def _flash_step(s, v, m_ref, l_ref, acc_ref):
    m_prev = m_ref[...]
    m_new = jnp.maximum(m_prev, jnp.max(s, axis=1, keepdims=True))
    a = jnp.exp(m_prev - m_new)
    p = jnp.exp(s - m_new)
    l_ref[...] = a * l_ref[...] + jnp.sum(p, axis=1, keepdims=True)
    acc_ref[...] = a * acc_ref[...] + _dot(p.astype(BF16), v)
    m_ref[...] = m_new


def _causal_mask(rows, keys, row0, key0):
    r = row0 + lax.broadcasted_iota(jnp.int32, (rows, keys), 0)
    k = key0 + lax.broadcasted_iota(jnp.int32, (rows, keys), 1)
    return k <= r


def _fox_kernel(q_ref, k_ref, v_ref, ccol_ref, crow_ref, o_ref, m_ref, l_ref, acc_ref, *, tile):
    qi = pl.program_id(1)
    lane = lax.broadcasted_iota(jnp.int32, (1, LANES), 1)
    for h in range(FOX_HEADS):
        q = q_ref[0, h]
        cq = jnp.sum(jnp.where(lane == h, ccol_ref[0], 0.0), axis=1, keepdims=True)
        _flash_init(m_ref, l_ref, acc_ref)

        def scores(c):
            k0 = pl.multiple_of(c * tile, tile)
            k = k_ref[0, h, pl.ds(k0, tile), :]
            ck = crow_ref[0, h:h + 1, pl.ds(k0, tile)]
            return _dot_nt(q, k) + (cq - ck), v_ref[0, h, pl.ds(k0, tile), :]

        def body(c, carry):
            s, v = scores(c)
            _flash_step(s, v, m_ref, l_ref, acc_ref)
            return carry

        lax.fori_loop(0, qi, body, 0)
        s, v = scores(qi)
        s = jnp.where(_causal_mask(tile, tile, 0, 0), s, NEG_BIG)
        _flash_step(s, v, m_ref, l_ref, acc_ref)
        o_ref[0, :, h * HEAD_DIM:(h + 1) * HEAD_DIM] = (acc_ref[...] / l_ref[...]).astype(BF16)


def _fox_attention(fq, fk, fv, ccol, crow):
    b, _, s, _ = fq.shape
    tile = ATT_TILE
    return pl.pallas_call(
        functools.partial(_fox_kernel, tile=tile),
        grid=(b, s // tile),
        in_specs=[
            pl.BlockSpec((1, FOX_HEADS, tile, HEAD_DIM), lambda i, j: (i, 0, j, 0)),
            pl.BlockSpec((1, FOX_HEADS, s, HEAD_DIM), lambda i, j: (i, 0, 0, 0)),
            pl.BlockSpec((1, FOX_HEADS, s, HEAD_DIM), lambda i, j: (i, 0, 0, 0)),
            pl.BlockSpec((1, tile, LANES), lambda i, j: (i, j, 0)),
            pl.BlockSpec((1, FOX_HEADS, s), lambda i, j: (i, 0, 0)),
        ],
        out_specs=pl.BlockSpec((1, tile, FOX_WIDTH), lambda i, j: (i, j, 0)),
        out_shape=jax.ShapeDtypeStruct((b, s, FOX_WIDTH), BF16),
        scratch_shapes=[pltpu.VMEM((tile, 1), F32), pltpu.VMEM((tile, 1), F32), pltpu.VMEM((tile, HEAD_DIM), F32)],
        compiler_params=_cparams("parallel", "arbitrary"),
        name="fox_attention",
    )(fq, fk, fv, ccol, crow)


def _diff_kernel(q_ref, k_ref, v_ref, lam_ref, g_ref, o_ref, m_ref, l_ref, acc_ref, *, tile, lambda_init):
    qi = pl.program_id(1)
    lp = lam_ref[...]
    lam = (jnp.exp(jnp.sum(lp[0:1] * lp[1:2], axis=1, keepdims=True))
           - jnp.exp(jnp.sum(lp[2:3] * lp[3:4], axis=1, keepdims=True)) + lambda_init)
    for h in range(DIFF_HEADS):
        maps = []
        for c2 in range(2):
            vh = 2 * h + c2
            q = q_ref[0, vh]
            _flash_init(m_ref, l_ref, acc_ref)

            def scores(c):
                k0 = pl.multiple_of(c * tile, tile)
                return _dot_nt(q, k_ref[0, vh, pl.ds(k0, tile), :]), v_ref[0, h, pl.ds(k0, tile), :]

            def body(c, carry):
                s, v = scores(c)
                _flash_step(s, v, m_ref, l_ref, acc_ref)
                return carry

            lax.fori_loop(0, qi, body, 0)
            s, v = scores(qi)
            s = jnp.where(_causal_mask(tile, tile, 0, 0), s, NEG_BIG)
            _flash_step(s, v, m_ref, l_ref, acc_ref)
            maps.append(acc_ref[...] / l_ref[...])
        o = maps[0] - lam * maps[1]
        o = o * lax.rsqrt(jnp.mean(o * o, axis=-1, keepdims=True) + LN_EPS) * g_ref[...]
        o = o * (1.0 - lambda_init)
        o_ref[0, :, h * HEAD_DIM:(h + 1) * HEAD_DIM] = o.astype(BF16)


def _diff_attention(dq, dk, dv, lam_params, subln_g, lambda_init):
    b, _, s, _ = dq.shape
    tile = ATT_TILE
    return pl.pallas_call(
        functools.partial(_diff_kernel, tile=tile, lambda_init=lambda_init),
        grid=(b, s // tile),
        in_specs=[
            pl.BlockSpec((1, 2 * DIFF_HEADS, tile, DIFF_QK_DIM), lambda i, j: (i, 0, j, 0)),
            pl.BlockSpec((1, 2 * DIFF_HEADS, s, DIFF_QK_DIM), lambda i, j: (i, 0, 0, 0)),
            pl.BlockSpec((1, DIFF_HEADS, s, HEAD_DIM), lambda i, j: (i, 0, 0, 0)),
            pl.BlockSpec((4, DIFF_QK_DIM), lambda i, j: (0, 0)),
            pl.BlockSpec((1, HEAD_DIM), lambda i, j: (0, 0)),
        ],
        out_specs=pl.BlockSpec((1, tile, DIFF_WIDTH), lambda i, j: (i, j, 0)),
        out_shape=jax.ShapeDtypeStruct((b, s, DIFF_WIDTH), BF16),
        scratch_shapes=[pltpu.VMEM((tile, 1), F32), pltpu.VMEM((tile, 1), F32), pltpu.VMEM((tile, HEAD_DIM), F32)],
        compiler_params=_cparams("parallel", "arbitrary"),
        name="diff_attention",
    )(dq, dk, dv, lam_params, subln_g)


def _nsa_kernel(q_ref, kc_ref, vc_ref, ks_ref, vs_ref, kw_ref, vw_ref, g_ref, ovl_ref, blk_ref,
                o_ref, m_ref, l_ref, acc_ref, out_ref, *, tq, tk):
    qi = pl.program_id(1)
    t0 = qi * tq
    rows = NSA_HEADS * tq
    q = q_ref[0].reshape(rows, HEAD_DIM)
    t_row = t0 + lax.broadcasted_iota(jnp.int32, (NSA_HEADS, tq, 1), 1).reshape(rows, 1)
    gates = g_ref[0]

    def add_gated(branch, o, first=False):
        for h in range(NSA_HEADS):
            col = NSA_GATE_OFF + 3 * h + branch
            term = gates[:, col:col + 1] * o[h * tq:(h + 1) * tq]
            if first:
                out_ref[h * tq:(h + 1) * tq, :] = term
            else:
                out_ref[h * tq:(h + 1) * tq, :] += term

    s = _dot_nt(q, kc_ref[0])
    block_end = CMP_STRIDE * lax.broadcasted_iota(jnp.int32, (1, N_CMP), 1) + (CMP_BLOCK - 1)
    vis = block_end <= t_row
    s = jnp.where(vis, s, NEG_BIG)
    p = jnp.where(vis, jnp.exp(s - jnp.max(s, axis=1, keepdims=True)), 0.0)
    l = jnp.sum(p, axis=1, keepdims=True)
    p = p / jnp.where(l > 0.0, l, 1.0)
    add_gated(0, _dot(p.astype(BF16), vc_ref[0]), first=True)

    p_sum = jnp.sum(p.reshape(NSA_HEADS, tq, N_CMP), axis=0)
    imp_t = lax.dot_general(ovl_ref[...], p_sum, (((1,), (1,)), ((), ())), preferred_element_type=F32,
                            precision=lax.Precision.HIGHEST)
    j_idx = lax.broadcasted_iota(jnp.int32, (N_SLC, 1), 0)
    t_col = t0 + lax.broadcasted_iota(jnp.int32, (1, tq), 1)
    blk_t = t_col // SLC_BLOCK
    forced = (j_idx == 0) | (j_idx == blk_t) | (j_idx == blk_t - 1)
    valid = j_idx * SLC_BLOCK <= t_col
    score = jnp.where(forced, 1e9, jnp.where(valid, imp_t, -1.0))
    rank = jnp.zeros((N_SLC, tq), F32)
    for i in range(N_SLC):
        si = score[i:i + 1, :]
        tie = jnp.where(j_idx > i, 1.0, 0.0)
        rank = rank + jnp.where(si > score, 1.0, 0.0) + jnp.where(si == score, tie, 0.0)
    sel_neg_t = jnp.where(rank < float(SLC_TOPK), 0.0, NEG_BIG)
    sel_neg_t = jnp.concatenate([sel_neg_t, jnp.zeros((LANES - N_SLC, tq), F32)], axis=0)
    sel_neg = sel_neg_t.T.astype(BF16)
    sel_neg = jnp.concatenate([sel_neg] * NSA_HEADS, axis=0)

    _flash_init(m_ref, l_ref, acc_ref)

    def slc_scores(c):
        k0 = pl.multiple_of(c * tk, tk)
        s = _dot_nt(q, ks_ref[0, pl.ds(k0, tk), :]) + _dot(sel_neg, blk_ref[:, pl.ds(k0, tk)])
        return s, vs_ref[0, pl.ds(k0, tk), :]

    def slc_body(c, carry):
        s, v = slc_scores(c)
        _flash_step(s, v, m_ref, l_ref, acc_ref)
        return carry

    lax.fori_loop(0, qi, slc_body, 0)
    s, v = slc_scores(qi)
    k_pos = t0 + lax.broadcasted_iota(jnp.int32, (1, tk), 1)
    s = jnp.where(k_pos <= t_row, s, NEG_BIG)
    _flash_step(s, v, m_ref, l_ref, acc_ref)
    add_gated(1, acc_ref[...] / l_ref[...])

    _flash_init(m_ref, l_ref, acc_ref)
    n_back = WINDOW // tk

    def win_body(c, carry):
        k0 = pl.multiple_of(c * tk, tk)
        s = _dot_nt(q, kw_ref[0, pl.ds(k0, tk), :])
        k_pos = k0 + lax.broadcasted_iota(jnp.int32, (1, tk), 1)
        s = jnp.where((k_pos <= t_row) & (k_pos > t_row - WINDOW), s, NEG_BIG)
        _flash_step(s, vw_ref[0, pl.ds(k0, tk), :], m_ref, l_ref, acc_ref)
        return carry

    lax.fori_loop(jnp.maximum(qi - n_back, 0), qi + 1, win_body, 0)
    add_gated(2, acc_ref[...] / l_ref[...])
    out = out_ref[...]
    for h in range(NSA_HEADS):
        o_ref[0, :, h * HEAD_DIM:(h + 1) * HEAD_DIM] = out[h * tq:(h + 1) * tq].astype(BF16)


def _nsa_attention(nq, kc, vc, kslc, vslc, kwin, vwin, gates, ovl, blk):
    b, _, s, _ = nq.shape
    tq, tk = NSA_Q_TILE, NSA_K_TILE
    rows = NSA_HEADS * tq
    seq = lambda i, j: (i, 0, 0)
    return pl.pallas_call(
        functools.partial(_nsa_kernel, tq=tq, tk=tk),
        grid=(b, s // tq),
        in_specs=[
            pl.BlockSpec((1, NSA_HEADS, tq, HEAD_DIM), lambda i, j: (i, 0, j, 0)),
            pl.BlockSpec((1, N_CMP, HEAD_DIM), seq),
            pl.BlockSpec((1, N_CMP, HEAD_DIM), seq),
            pl.BlockSpec((1, s, HEAD_DIM), seq),
            pl.BlockSpec((1, s, HEAD_DIM), seq),
            pl.BlockSpec((1, s, HEAD_DIM), seq),
            pl.BlockSpec((1, s, HEAD_DIM), seq),
            pl.BlockSpec((1, tq, GATE_COLS), lambda i, j: (i, j, 0)),
            pl.BlockSpec((N_SLC, N_CMP), lambda i, j: (0, 0)),
            pl.BlockSpec((LANES, s), lambda i, j: (0, 0)),
        ],
        out_specs=pl.BlockSpec((1, tq, NSA_WIDTH), lambda i, j: (i, j, 0)),
        out_shape=jax.ShapeDtypeStruct((b, s, NSA_WIDTH), BF16),
        scratch_shapes=[pltpu.VMEM((rows, 1), F32), pltpu.VMEM((rows, 1), F32),
                        pltpu.VMEM((rows, HEAD_DIM), F32), pltpu.VMEM((rows, HEAD_DIM), F32)],
        compiler_params=_cparams("parallel", "arbitrary"),
        name="nsa_attention",
    )(nq, kc, vc, kslc, vslc, kwin, vwin, gates, ovl, blk)


def _out_ln_kernel(x_ref, on_ref, od_ref, of_ref, wn_ref, wd_ref, wf_ref, lng_ref, lnb_ref, o_ref):
    h = _dot(on_ref[...], wn_ref[...]) + _dot(od_ref[...], wd_ref[...]) + _dot(of_ref[...], wf_ref[...])
    o_ref[...] = _layer_norm(DEEPNORM_ALPHA * x_ref[...] + h, lng_ref[...], lnb_ref[...])


def _out_ln(x, o_nsa, o_diff, o_fox, wn, wd, wf, ln_g, ln_b):
    t = x.shape[0]
    tm = ROW_TILE
    row = lambda i: (i, 0)
    c2 = lambda i: (0, 0)
    return pl.pallas_call(
        _out_ln_kernel,
        grid=(t // tm,),
        in_specs=[
            pl.BlockSpec((tm, D_MODEL), row),
            pl.BlockSpec((tm, NSA_WIDTH), row),
            pl.BlockSpec((tm, DIFF_WIDTH), row),
            pl.BlockSpec((tm, FOX_WIDTH), row),
            pl.BlockSpec((NSA_WIDTH, D_MODEL), c2),
            pl.BlockSpec((DIFF_WIDTH, D_MODEL), c2),
            pl.BlockSpec((FOX_WIDTH, D_MODEL), c2),
            pl.BlockSpec((1, D_MODEL), c2),
            pl.BlockSpec((1, D_MODEL), c2),
        ],
        out_specs=pl.BlockSpec((tm, D_MODEL), row),
        out_shape=jax.ShapeDtypeStruct((t, D_MODEL), F32),
        compiler_params=_cparams("parallel"),
        name="out_ln",
    )(x, o_nsa, o_diff, o_fox, wn, wd, wf, ln_g, ln_b)


def _prep_ffn(wg, wu, wd):
    pad = D_FF_PAD - D_FF

    def up(w):
        w = jnp.pad(w, ((0, 0), (0, pad))).astype(BF16)
        return w.reshape(D_MODEL, FF_CHUNKS, FF_CHUNK).transpose(1, 0, 2)

    wd = jnp.pad(wd, ((0, pad), (0, 0))).astype(BF16).reshape(FF_CHUNKS, FF_CHUNK, D_MODEL)
    return up(wg), up(wu), wd


def _rot_cols(w, d):
    k, n = w.shape
    w = w.reshape(k, n // d, d)
    return jnp.concatenate([-w[..., d // 2:], w[..., :d // 2]], axis=-1).reshape(k, n)


def _rope_table(pos, d, width):
    inv = ROPE_THETA ** (-jnp.arange(0, d, 2, dtype=F32) / d)
    ang = pos.astype(F32)[:, None] * inv[None, :]
    cos = jnp.concatenate([jnp.cos(ang), jnp.cos(ang)], axis=-1)
    sin = jnp.concatenate([jnp.sin(ang), jnp.sin(ang)], axis=-1)
    rep = width // d
    return jnp.tile(cos, (1, rep)), jnp.tile(sin, (1, rep))


def _prep_in_proj(w_in, fox_b_f, s):
    o = 0
    nsa_q = w_in[:, o:o + NSA_WIDTH]; o += NSA_WIDTH
    kv = [w_in[:, o + i * HEAD_DIM:o + (i + 1) * HEAD_DIM] for i in range(6)]; o += 6 * HEAD_DIM
    k_cmp, v_cmp, k_slc, v_slc, k_win, v_win = kv
    nsa_g = w_in[:, o:o + N_NSA_GATES]; o += N_NSA_GATES
    diff_q = w_in[:, o:o + DIFF_WIDTH]; o += DIFF_WIDTH
    diff_k = w_in[:, o:o + DIFF_WIDTH]; o += DIFF_WIDTH
    diff_v = w_in[:, o:o + DIFF_WIDTH]; o += DIFF_WIDTH
    fox_q = w_in[:, o:o + FOX_WIDTH]; o += FOX_WIDTH
    fox_k = w_in[:, o:o + FOX_WIDTH]; o += FOX_WIDTH
    fox_v = w_in[:, o:o + FOX_WIDTH]; o += FOX_WIDTH
    fox_f = w_in[:, o:o + FOX_HEADS]

    wa = jnp.concatenate([nsa_q, k_slc, k_win, diff_q, diff_k], axis=1)
    war = jnp.concatenate([_rot_cols(nsa_q, HEAD_DIM), _rot_cols(k_slc, HEAD_DIM), _rot_cols(k_win, HEAD_DIM),
                           _rot_cols(diff_q, DIFF_QK_DIM), _rot_cols(diff_k, DIFF_QK_DIM)], axis=1)
    wb = jnp.concatenate([k_cmp, v_cmp, v_slc, v_win, diff_v, fox_q * HEAD_DIM ** -0.5, fox_k, fox_v], axis=1)
    wc = jnp.concatenate([fox_f, nsa_g, jnp.zeros((D_MODEL, GATE_COLS - N_NSA_GATES - FOX_HEADS), F32)], axis=1)
    bc = jnp.concatenate([fox_b_f, jnp.zeros((GATE_COLS - FOX_HEADS,), F32)])[None, :]

    pos = jnp.arange(s, dtype=jnp.int32)
    c64, s64 = _rope_table(pos, HEAD_DIM, HEAD_DIM)
    c32, s32 = _rope_table(pos, DIFF_QK_DIM, DIFF_QK_DIM)
    nsa_scale = HEAD_DIM ** -0.5
    diff_scale = DIFF_QK_DIM ** -0.5

    def table(t64, t32):
        return jnp.concatenate([jnp.tile(t64, (1, NSA_HEADS)) * nsa_scale, t64, t64,
                                jnp.tile(t32, (1, 2 * DIFF_HEADS)) * diff_scale,
                                jnp.tile(t32, (1, 2 * DIFF_HEADS))], axis=1)

    return (wa.astype(BF16), war.astype(BF16), wb.astype(BF16), wc.astype(BF16),
            table(c64, c32), table(s64, s32), bc)


def _prep_compress(pos_k, pos_v, phi_k1, phi_k2, phi_v1, phi_v2):
    half = CMP_STRIDE * HEAD_DIM

    def pe(p):
        return jnp.broadcast_to(p.reshape(2, 1, half), (2, 8, half)).astype(BF16)

    block_end = jnp.arange(N_CMP, dtype=jnp.int32) * CMP_STRIDE + (CMP_BLOCK - 1)
    cos_c, sin_c = _rope_table(block_end, HEAD_DIM, HEAD_DIM)
    return (pe(pos_k), pe(pos_v), phi_k1.reshape(2, half, CMP_HIDDEN).astype(BF16), phi_k2.astype(BF16),
            _rot_cols(phi_k2, HEAD_DIM).astype(BF16), phi_v1.reshape(2, half, CMP_HIDDEN).astype(BF16),
            phi_v2.astype(BF16), cos_c, sin_c)


def _selection_constants(s):
    c0 = np.arange(N_CMP) * CMP_STRIDE
    s0 = np.arange(N_SLC) * SLC_BLOCK
    ovl = (c0[None, :] < s0[:, None] + SLC_BLOCK) & (c0[None, :] + CMP_BLOCK > s0[:, None])
    ovl[:, (s - CMP_BLOCK) // CMP_STRIDE + 1:] = False
    blk = np.zeros((LANES, s), np.float32)
    blk[np.arange(s) // SLC_BLOCK, np.arange(s)] = 1.0
    return jnp.asarray(ovl.astype(np.float32)), jnp.asarray(blk, dtype=BF16)


def kernel(x, p, ln_g, ln_b, ffn1_w_gate, ffn1_w_up, ffn1_w_down, ffn2_w_gate, ffn2_w_up, ffn2_w_down, w_in, fox_b_f, nsa_pos_k, nsa_pos_v, nsa_phi_k1, nsa_phi_k2, nsa_phi_v1, nsa_phi_v2, diff_lambda, diff_subln_g, w_out, ple_w_gate, ple_b_gate, ple_w_proj):
    b, s, _ = x.shape
    assert s // SLC_BLOCK == N_SLC and (s - CMP_BLOCK) // CMP_STRIDE + 1 <= N_CMP
    t = b * s
    ovl, blk = _selection_constants(s)
    tri = jnp.asarray(np.tril(np.ones((CUM_TILE, CUM_TILE), np.float32)))
    x = x.reshape(t, D_MODEL)
    for i in range(DEPTH):
        lambda_init = 0.8 - 0.6 * math.exp(-0.3 * i)
        lng = ln_g[i][:, None, :]
        lnb = ln_b[i][:, None, :]
        x = _ffn_ln(x, *_prep_ffn(ffn1_w_gate[i], ffn1_w_up[i], ffn1_w_down[i]), lng[0], lnb[0])
        proj = _in_proj(x.reshape(b, s, D_MODEL), *_prep_in_proj(w_in[i], fox_b_f[i], s))
        nq, kslc, kwin, dq, dk, kcmp, vcmp, vslc, vwin, dv, fq, fk, fv, gates, fls = proj
        half = CMP_STRIDE * HEAD_DIM
        kc, vc = _compress(kcmp.reshape(b, s // CMP_STRIDE, half), vcmp.reshape(b, s // CMP_STRIDE, half),
                           *_prep_compress(nsa_pos_k[i], nsa_pos_v[i], nsa_phi_k1[i], nsa_phi_k2[i],
                                           nsa_phi_v1[i], nsa_phi_v2[i]))
        ccol, crow = _decay(fls, tri)
        o_nsa = _nsa_attention(nq, kc, vc, kslc, vslc, kwin, vwin, gates, ovl, blk)
        o_diff = _diff_attention(dq, dk, dv, diff_lambda[i], diff_subln_g[i][None, :], lambda_init)
        o_fox = _fox_attention(fq, fk, fv, ccol, crow)
        wo = w_out[i].astype(BF16)
        x = _out_ln(x, o_nsa.reshape(t, NSA_WIDTH), o_diff.reshape(t, DIFF_WIDTH), o_fox.reshape(t, FOX_WIDTH),
                    wo[:NSA_WIDTH], wo[NSA_WIDTH:NSA_WIDTH + DIFF_WIDTH], wo[NSA_WIDTH + DIFF_WIDTH:],
                    lng[1], lnb[1])
        ple = (p[i].reshape(t, PLE_DIM), ple_w_gate[i].astype(BF16), ple_b_gate[i][None, :],
               ple_w_proj[i].astype(BF16))
        x = _ffn_ln(x, *_prep_ffn(ffn2_w_gate[i], ffn2_w_up[i], ffn2_w_down[i]), lng[2], lnb[2], ple=ple)
    return x.reshape(b, s, D_MODEL)
```

```python
import functools
import math

import numpy as np
import jax
import jax.numpy as jnp
from jax import lax
from jax.experimental import pallas as pl
from jax.experimental.pallas import tpu as pltpu

F32 = jnp.float32
BF16 = jnp.bfloat16

D_MODEL = 1024
HEAD_DIM = 64
NSA_HEADS = 4
DIFF_HEADS = 4
FOX_HEADS = 8
NSA_WIDTH = NSA_HEADS * HEAD_DIM
DIFF_WIDTH = DIFF_HEADS * HEAD_DIM
FOX_WIDTH = FOX_HEADS * HEAD_DIM
CMP_BLOCK = 32
CMP_STRIDE = 16
CMP_HIDDEN = 256
SLC_BLOCK = 64
SLC_TOPK = 16
WINDOW = 512
DIFF_QK_DIM = HEAD_DIM // 2
DIFF_MAPS = 2 * DIFF_HEADS
D_FF = 2752
PLE_DIM = 256
ROPE_THETA = 10000.0
LN_EPS = 1e-5
NEG_BIG = -1e30
DEPTH = 2
DEEPNORM_ALPHA = (2.0 * DEPTH) ** 0.25
LOG2E = math.log2(math.e)

LANES = 128
MXU_COLS = 256
BF16_SUBLANES = 16
VMEM_LIMIT_BYTES = 56 * 1024 * 1024

FF_CHUNK = MXU_COLS
D_FF_PAD = ((D_FF + FF_CHUNK - 1) // FF_CHUNK) * FF_CHUNK
FF_CHUNKS = D_FF_PAD // FF_CHUNK

ROW_TILE = 512
ATT_TILE = 256
CUM_TILE = 256
QK_AHEAD = 2

TM_ROPE_COLS = 2 * LANES + DIFF_WIDTH
TM_PLAIN_COLS = FOX_WIDTH + 2 * HEAD_DIM
FM_ROPE_ROWS = NSA_WIDTH + DIFF_WIDTH
FM_PLAIN_ROWS = 2 * HEAD_DIM + DIFF_WIDTH + 2 * FOX_WIDTH
N_NSA_GATES = 3 * NSA_HEADS
GATE_ROWS = BF16_SUBLANES
N_CMP = 128
N_SLC = 32


def _cparams(*sem):
    return pltpu.CompilerParams(dimension_semantics=sem, vmem_limit_bytes=VMEM_LIMIT_BYTES)


def _dot(a, b):
    return jnp.dot(a, b, preferred_element_type=F32)


def _dot_nt(a, b):
    return lax.dot_general(a, b, (((1,), (1,)), ((), ())), preferred_element_type=F32)


def _sigmoid(z):
    return 1.0 / (1.0 + jnp.exp(-z))


def _layer_norm(y, g, b):
    mu = jnp.mean(y, axis=-1, keepdims=True)
    yc = y - mu
    var = jnp.mean(yc * yc, axis=-1, keepdims=True)
    return yc * lax.rsqrt(var + LN_EPS) * g + b


def _ffn_kernel(*refs, with_ple):
    if with_ple:
        (x_ref, wg_ref, wu_ref, wd_ref, lng_ref, lnb_ref, p_ref, pwg_ref, pbg_ref, pwp_ref,
         o_ref, xb_ref, acc_ref) = refs
    else:
        x_ref, wg_ref, wu_ref, wd_ref, lng_ref, lnb_ref, o_ref, xb_ref, acc_ref = refs
    xb_ref[...] = x_ref[...].astype(BF16)
    acc_ref[...] = jnp.zeros_like(acc_ref)

    def chunk(c, carry):
        xb = xb_ref[...]
        g = _dot(xb, wg_ref[c])
        u = _dot(xb, wu_ref[c])
        h = (g * _sigmoid(g)) * u
        acc_ref[...] += _dot(h.astype(BF16), wd_ref[c])
        return carry

    lax.fori_loop(0, FF_CHUNKS, chunk, 0)
    y = DEEPNORM_ALPHA * x_ref[...] + 0.5 * acc_ref[...]
    z = _layer_norm(y, lng_ref[...], lnb_ref[...])
    if with_ple:
        gate = _sigmoid(_dot(z.astype(BF16), pwg_ref[...]) + pbg_ref[...])
        z = z + gate * _dot(p_ref[...].astype(BF16), pwp_ref[...])
    o_ref[...] = z


def _ffn_ln(x, wg, wu, wd, ln_g, ln_b, ple=None):
    t = x.shape[0]
    tm = ROW_TILE
    row = lambda i: (i, 0)
    full2 = lambda i: (0, 0)
    full3 = lambda i: (0, 0, 0)
    in_specs = [
        pl.BlockSpec((tm, D_MODEL), row),
        pl.BlockSpec((FF_CHUNKS, D_MODEL, FF_CHUNK), full3),
        pl.BlockSpec((FF_CHUNKS, D_MODEL, FF_CHUNK), full3),
        pl.BlockSpec((FF_CHUNKS, FF_CHUNK, D_MODEL), full3),
        pl.BlockSpec((1, D_MODEL), full2),
        pl.BlockSpec((1, D_MODEL), full2),
    ]
    args = [x, wg, wu, wd, ln_g, ln_b]
    if ple is not None:
        p, pwg, pbg, pwp = ple
        in_specs += [
            pl.BlockSpec((tm, PLE_DIM), row),
            pl.BlockSpec((D_MODEL, D_MODEL), full2),
            pl.BlockSpec((1, D_MODEL), full2),
            pl.BlockSpec((PLE_DIM, D_MODEL), full2),
        ]
        args += [p, pwg, pbg, pwp]
    return pl.pallas_call(
        functools.partial(_ffn_kernel, with_ple=ple is not None),
        grid=(t // tm,),
        in_specs=in_specs,
        out_specs=pl.BlockSpec((tm, D_MODEL), row),
        out_shape=jax.ShapeDtypeStruct((t, D_MODEL), F32),
        scratch_shapes=[pltpu.VMEM((tm, D_MODEL), BF16), pltpu.VMEM((tm, D_MODEL), F32)],
        compiler_params=_cparams("parallel"),
        name="ffn_ln_ple" if ple is not None else "ffn_ln",
    )(*args)


def _in_proj_kernel(x_ref, wa_ref, war_ref, wb_ref, wc_ref, wd_ref, wdr_ref, we_ref, wf_ref,
                    cosa_ref, sina_ref, cosd_ref, sind_ref, bc_ref,
                    kslc_ref, kwin_ref, dk_ref, fk_ref, kcmp_ref, vcmp_ref, fls_ref,
                    nq_ref, dq_ref, vslc_ref, vwin_ref, dv_ref, fq_ref, fv_ref, gate_ref, *, tm):
    xb = x_ref[0].astype(BF16)
    ra = _dot(xb, wa_ref[...]) * cosa_ref[...] + _dot(xb, war_ref[...]) * sina_ref[...]
    pos = pl.program_id(1) * tm + lax.broadcasted_iota(jnp.int32, (tm, 1), 0)
    lane = lax.broadcasted_iota(jnp.int32, (1, LANES), 1)
    block_id = jnp.where(lane - HEAD_DIM == pos // SLC_BLOCK, 1.0, 0.0)
    kslc_ref[0] = (ra[:, 0:LANES] + block_id).astype(BF16)
    kwin_ref[0] = ra[:, LANES:2 * LANES].astype(BF16)
    dk_ref[0] = ra[:, 2 * LANES:].astype(BF16)
    pb = _dot(xb, wb_ref[...]).astype(BF16)
    fk_ref[0] = pb[:, 0:FOX_WIDTH]
    kcmp_ref[0] = pb[:, FOX_WIDTH:FOX_WIDTH + HEAD_DIM]
    vcmp_ref[0] = pb[:, FOX_WIDTH + HEAD_DIM:]
    z = _dot(xb, wc_ref[...]) + bc_ref[...]
    fls_ref[0] = jnp.minimum(z, 0.0) - jnp.log(1.0 + jnp.exp(-jnp.abs(z)))

    rd = _dot_nt(wd_ref[...], xb) * cosd_ref[...] + _dot_nt(wdr_ref[...], xb) * sind_ref[...]
    nq_ref[0] = rd[0:NSA_WIDTH].astype(BF16)
    dq_ref[0] = rd[NSA_WIDTH:].astype(BF16)
    pe = _dot_nt(we_ref[...], xb).astype(BF16)
    vslc_ref[0] = pe[0:HEAD_DIM]
    vwin_ref[0] = pe[HEAD_DIM:2 * HEAD_DIM]
    off = 2 * HEAD_DIM
    dv_ref[0] = pe[off:off + DIFF_WIDTH]
    off += DIFF_WIDTH
    fq_ref[0] = pe[off:off + FOX_WIDTH]
    off += FOX_WIDTH
    fv_ref[0] = pe[off:off + FOX_WIDTH]
    gate_ref[0] = _sigmoid(_dot_nt(wf_ref[...], xb))


def _in_proj(x, wa, war, wb, wc, wd, wdr, we, wf, cosa, sina, cosd, sind, bc):
    b, s, _ = x.shape
    tm = ROW_TILE
    grid = (b, s // tm)
    w2 = lambda i, j: (0, 0)
    tok = lambda i, j: (i, j, 0)
    fm = lambda i, j: (i, 0, j)
    in_specs = [
        pl.BlockSpec((1, tm, D_MODEL), tok),
        pl.BlockSpec((D_MODEL, TM_ROPE_COLS), w2),
        pl.BlockSpec((D_MODEL, TM_ROPE_COLS), w2),
        pl.BlockSpec((D_MODEL, TM_PLAIN_COLS), w2),
        pl.BlockSpec((D_MODEL, LANES), w2),
        pl.BlockSpec((FM_ROPE_ROWS, D_MODEL), w2),
        pl.BlockSpec((FM_ROPE_ROWS, D_MODEL), w2),
        pl.BlockSpec((FM_PLAIN_ROWS, D_MODEL), w2),
        pl.BlockSpec((GATE_ROWS, D_MODEL), w2),
        pl.BlockSpec((tm, TM_ROPE_COLS), lambda i, j: (j, 0)),
        pl.BlockSpec((tm, TM_ROPE_COLS), lambda i, j: (j, 0)),
        pl.BlockSpec((FM_ROPE_ROWS, tm), lambda i, j: (0, j)),
        pl.BlockSpec((FM_ROPE_ROWS, tm), lambda i, j: (0, j)),
        pl.BlockSpec((1, LANES), w2),
    ]

    def tok_out(width, dtype=BF16):
        return pl.BlockSpec((1, tm, width), tok), jax.ShapeDtypeStruct((b, s, width), dtype)

    def fm_out(rows, dtype=BF16):
        return pl.BlockSpec((1, rows, tm), fm), jax.ShapeDtypeStruct((b, rows, s), dtype)

    outs = [
        tok_out(LANES), tok_out(LANES), tok_out(DIFF_WIDTH), tok_out(FOX_WIDTH),
        tok_out(HEAD_DIM), tok_out(HEAD_DIM), tok_out(LANES, F32),
        fm_out(NSA_WIDTH), fm_out(DIFF_WIDTH),
        fm_out(HEAD_DIM), fm_out(HEAD_DIM), fm_out(DIFF_WIDTH),
        fm_out(FOX_WIDTH), fm_out(FOX_WIDTH), fm_out(GATE_ROWS, F32),
    ]
    return pl.pallas_call(
        functools.partial(_in_proj_kernel, tm=tm),
        grid=grid,
        in_specs=in_specs,
        out_specs=[o[0] for o in outs],
        out_shape=[o[1] for o in outs],
        compiler_params=_cparams("parallel", "parallel"),
        name="in_proj",
    )(x, wa, war, wb, wc, wd, wdr, we, wf, cosa, sina, cosd, sind, bc)


def _gelu_tanh(x):
    return 0.5 * x * (1.0 + jnp.tanh(math.sqrt(2.0 / math.pi) * (x + 0.044715 * (x * x * x))))


def _compress_kernel(k16_ref, v16_ref, pek_ref, pev_ref, wk1_ref, wk2_ref, wk2r_ref, wv1_ref, wv2t_ref,
                     cos_ref, sin_ref, kc_ref, vct_ref):
    def hidden(x16, pe_ref, w1_ref):
        top = _dot(x16, w1_ref[0])
        bot = _dot(x16, w1_ref[1])
        bias = _dot(pe_ref[0], w1_ref[0]) + _dot(pe_ref[1], w1_ref[1])
        bot = pltpu.roll(bot, N_CMP - 1, 0)
        return _gelu_tanh(top + bot + bias[0:1, :]).astype(BF16)

    hk = hidden(k16_ref[0], pek_ref, wk1_ref)
    kc = _dot(hk, wk2_ref[...]) * cos_ref[...] + _dot(hk, wk2r_ref[...]) * sin_ref[...]
    kc_ref[0] = kc.astype(BF16)
    hv = hidden(v16_ref[0], pev_ref, wv1_ref)
    vct_ref[0] = _dot_nt(wv2t_ref[...], hv).astype(BF16)


def _compress(k16, v16, pek, pev, wk1, wk2, wk2r, wv1, wv2t, cos_c, sin_c):
    b = k16.shape[0]
    half = CMP_STRIDE * HEAD_DIM
    bat = lambda i: (i, 0, 0)
    c2 = lambda i: (0, 0)
    c3 = lambda i: (0, 0, 0)
    in_specs = [
        pl.BlockSpec((1, N_CMP, half), bat),
        pl.BlockSpec((1, N_CMP, half), bat),
        pl.BlockSpec((2, 8, half), c3),
        pl.BlockSpec((2, 8, half), c3),
        pl.BlockSpec((2, half, CMP_HIDDEN), c3),
        pl.BlockSpec((CMP_HIDDEN, HEAD_DIM), c2),
        pl.BlockSpec((CMP_HIDDEN, HEAD_DIM), c2),
        pl.BlockSpec((2, half, CMP_HIDDEN), c3),
        pl.BlockSpec((HEAD_DIM, CMP_HIDDEN), c2),
        pl.BlockSpec((N_CMP, HEAD_DIM), c2),
        pl.BlockSpec((N_CMP, HEAD_DIM), c2),
    ]
    return pl.pallas_call(
        _compress_kernel,
        grid=(b,),
        in_specs=in_specs,
        out_specs=[pl.BlockSpec((1, N_CMP, HEAD_DIM), bat), pl.BlockSpec((1, HEAD_DIM, N_CMP), bat)],
        out_shape=[jax.ShapeDtypeStruct((b, N_CMP, HEAD_DIM), BF16), jax.ShapeDtypeStruct((b, HEAD_DIM, N_CMP), BF16)],
        compiler_params=_cparams("parallel"),
        name="nsa_compress",
    )(k16, v16, pek, pev, wk1, wk2, wk2r, wv1, wv2t, cos_c, sin_c)


def _decay_kernel(fls_ref, tri_ref, ccol_ref, crow_ref, *, n_tiles):
    carry = jnp.zeros((1, LANES), F32)
    for i in range(n_tiles):
        blk = fls_ref[0, i * CUM_TILE:(i + 1) * CUM_TILE, :]
        c = jnp.dot(tri_ref[...], blk, preferred_element_type=F32, precision=lax.Precision.HIGHEST) + carry
        c2 = c * LOG2E
        ccol_ref[0, i * CUM_TILE:(i + 1) * CUM_TILE, :] = c2
        crow_ref[0, :, i * CUM_TILE:(i + 1) * CUM_TILE] = c2.T[0:FOX_HEADS, :]
        carry = c[CUM_TILE - 1:CUM_TILE, :]


def _decay(fls, tri):
    b, s, _ = fls.shape
    return pl.pallas_call(
        functools.partial(_decay_kernel, n_tiles=s // CUM_TILE),
        grid=(b,),
        in_specs=[pl.BlockSpec((1, s, LANES), lambda i: (i, 0, 0)),
                  pl.BlockSpec((CUM_TILE, CUM_TILE), lambda i: (0, 0))],
        out_specs=[pl.BlockSpec((1, s, LANES), lambda i: (i, 0, 0)),
                   pl.BlockSpec((1, FOX_HEADS, s), lambda i: (i, 0, 0))],
        out_shape=[jax.ShapeDtypeStruct((b, s, LANES), F32), jax.ShapeDtypeStruct((b, FOX_HEADS, s), F32)],
        compiler_params=_cparams("parallel"),
        name="fox_decay",
    )(fls, tri)


def _flash_step_t(s, v_t, m, l, acc_ref, rows, shift=None):
    s_max = jnp.max(s, axis=0, keepdims=True)
    if shift is not None:
        s_max = s_max + shift
    m_new = jnp.maximum(m, s_max)
    a = jnp.exp2(m - m_new)
    p = jnp.exp2(s + ((shift - m_new) if shift is not None else -m_new))
    l_new = a * l + jnp.sum(p, axis=0, keepdims=True)
    acc_ref[rows, :] = a * acc_ref[rows, :] + _dot(v_t, p.astype(BF16))
    return m_new, l_new


def _causal_t(tk, tq):
    k = lax.broadcasted_iota(jnp.int32, (tk, tq), 0)
    q = lax.broadcasted_iota(jnp.int32, (tk, tq), 1)
    return k <= q


def _stats_init(n, tq):
    return (tuple(jnp.full((1, tq), NEG_BIG, F32) for _ in range(n)),
            tuple(jnp.zeros((1, tq), F32) for _ in range(n)))


def _head_rows(h, width):
    return slice(h * width, (h + 1) * width)


def _pad_queries(q_t, qpad_ref, n, width):
    per = MXU_COLS // width
    row = lax.broadcasted_iota(jnp.int32, (MXU_COLS, q_t.shape[1]), 0)
    for i in range(n):
        g, r = divmod(i, per)
        qg = q_t[g * MXU_COLS:(g + 1) * MXU_COLS]
        qpad_ref[i] = jnp.where((row >= r * width) & (row < (r + 1) * width), qg, jnp.zeros_like(qg))


def _fox_kernel(qt_ref, k_ref, vt_ref, ccol_ref, crow_ref, o_ref, qpad_ref, acc_ref, *, tile):
    qi = pl.program_id(1)
    q0 = pl.multiple_of(qi * tile, tile)
    _pad_queries(qt_ref[0], qpad_ref, FOX_HEADS, HEAD_DIM)
    acc_ref[...] = jnp.zeros(acc_ref.shape, F32)
    per = MXU_COLS // HEAD_DIM

    def step(c, masked, ms, ls):
        k0 = pl.multiple_of(c * tile, tile)
        ms, ls = list(ms), list(ls)

        def scores(h):
            g = h // per
            return _dot(k_ref[0, pl.ds(k0, tile), g * MXU_COLS:(g + 1) * MXU_COLS], qpad_ref[h])

        pend = [scores(h) for h in range(QK_AHEAD)]
        for h in range(FOX_HEADS):
            s = pend.pop(0)
            if h + QK_AHEAD < FOX_HEADS:
                pend.append(scores(h + QK_AHEAD))
            s = s - ccol_ref[0, pl.ds(k0, tile), h:h + 1]
            if masked:
                s = jnp.where(_causal_t(tile, tile), s, NEG_BIG)
            cq = crow_ref[0, h:h + 1, pl.ds(q0, tile)]
            rows = _head_rows(h, HEAD_DIM)
            ms[h], ls[h] = _flash_step_t(s, vt_ref[0, rows, pl.ds(k0, tile)], ms[h], ls[h], acc_ref, rows, shift=cq)
        return tuple(ms), tuple(ls)

    ms, ls = lax.fori_loop(0, qi, lambda c, carry: step(c, False, *carry), _stats_init(FOX_HEADS, tile))
    ms, ls = step(qi, True, ms, ls)
    for h in range(FOX_HEADS):
        rows = _head_rows(h, HEAD_DIM)
        acc_ref[rows, :] = acc_ref[rows, :] * (1.0 / ls[h])
    o_ref[0] = acc_ref[...].T.astype(BF16)


def _fox_attention(fq_t, fk, fv_t, ccol, crow):
    b, w, s = fq_t.shape
    tile = ATT_TILE
    return pl.pallas_call(
        functools.partial(_fox_kernel, tile=tile),
        grid=(b, s // tile),
        in_specs=[
            pl.BlockSpec((1, w, tile), lambda i, j: (i, 0, j)),
            pl.BlockSpec((1, s, w), lambda i, j: (i, 0, 0)),
            pl.BlockSpec((1, w, s), lambda i, j: (i, 0, 0)),
            pl.BlockSpec((1, s, LANES), lambda i, j: (i, 0, 0)),
            pl.BlockSpec((1, FOX_HEADS, s), lambda i, j: (i, 0, 0)),
        ],
        out_specs=pl.BlockSpec((1, tile, w), lambda i, j: (i, j, 0)),
        out_shape=jax.ShapeDtypeStruct((b, s, w), BF16),
        scratch_shapes=[pltpu.VMEM((FOX_HEADS, MXU_COLS, tile), BF16), pltpu.VMEM((w, tile), F32)],
        compiler_params=_cparams("parallel", "arbitrary"),
        name="fox_attention",
    )(fq_t, fk, fv_t, ccol, crow)


def _diff_kernel(qt_ref, k_ref, vt_ref, lam_ref, g_ref, o_ref, qpad_ref, acc_ref, *, tile, lambda_init):
    qi = pl.program_id(1)
    _pad_queries(qt_ref[0], qpad_ref, DIFF_MAPS, DIFF_QK_DIM)
    acc_ref[...] = jnp.zeros(acc_ref.shape, F32)

    def step(c, masked, ms, ls):
        k0 = pl.multiple_of(c * tile, tile)
        ms, ls = list(ms), list(ls)
        k_all = k_ref[0, pl.ds(k0, tile), :]

        def scores(i):
            return _dot(k_all, qpad_ref[i])

        pend = [scores(i) for i in range(QK_AHEAD)]
        for i in range(DIFF_MAPS):
            s = pend.pop(0)
            if i + QK_AHEAD < DIFF_MAPS:
                pend.append(scores(i + QK_AHEAD))
            if masked:
                s = jnp.where(_causal_t(tile, tile), s, NEG_BIG)
            v_t = vt_ref[0, _head_rows(i // 2, HEAD_DIM), pl.ds(k0, tile)]
            ms[i], ls[i] = _flash_step_t(s, v_t, ms[i], ls[i], acc_ref, _head_rows(i, HEAD_DIM))
        return tuple(ms), tuple(ls)

    ms, ls = lax.fori_loop(0, qi, lambda c, carry: step(c, False, *carry), _stats_init(DIFF_MAPS, tile))
    ms, ls = step(qi, True, ms, ls)
    lp = lam_ref[...]
    lam = (jnp.exp(jnp.sum(lp[0:1] * lp[1:2], axis=1, keepdims=True))
           - jnp.exp(jnp.sum(lp[2:3] * lp[3:4], axis=1, keepdims=True)) + lambda_init)
    heads = []
    for h in range(DIFF_HEADS):
        o1 = acc_ref[_head_rows(2 * h, HEAD_DIM), :] * (1.0 / ls[2 * h])
        o2 = acc_ref[_head_rows(2 * h + 1, HEAD_DIM), :] * (1.0 / ls[2 * h + 1])
        o = o1 - lam * o2
        o = o * lax.rsqrt(jnp.mean(o * o, axis=0, keepdims=True) + LN_EPS)
        heads.append(o * (1.0 - lambda_init))
    o_t = jnp.concatenate(heads, axis=0)
    o_ref[0] = (o_t.T * g_ref[...]).astype(BF16)


def _diff_attention(dq_t, dk, dv_t, lam_params, subln_g, lambda_init):
    b, w, s = dq_t.shape
    tile = ATT_TILE
    return pl.pallas_call(
        functools.partial(_diff_kernel, tile=tile, lambda_init=lambda_init),
        grid=(b, s // tile),
        in_specs=[
            pl.BlockSpec((1, w, tile), lambda i, j: (i, 0, j)),
            pl.BlockSpec((1, s, w), lambda i, j: (i, 0, 0)),
            pl.BlockSpec((1, w, s), lambda i, j: (i, 0, 0)),
            pl.BlockSpec((4, DIFF_QK_DIM), lambda i, j: (0, 0)),
            pl.BlockSpec((1, w), lambda i, j: (0, 0)),
        ],
        out_specs=pl.BlockSpec((1, tile, w), lambda i, j: (i, j, 0)),
        out_shape=jax.ShapeDtypeStruct((b, s, w), BF16),
        scratch_shapes=[pltpu.VMEM((DIFF_MAPS, MXU_COLS, tile), BF16), pltpu.VMEM((DIFF_MAPS * HEAD_DIM, tile), F32)],
        compiler_params=_cparams("parallel", "arbitrary"),
        name="diff_attention",
    )(dq_t, dk, dv_t, lam_params, subln_g)


def _nsa_kernel(qt_ref, kc_ref, vct_ref, ks_ref, vst_ref, kw_ref, vwt_ref, g_ref, ovl_ref,
                o_ref, acc_ref, out_ref, *, tile):
    qi = pl.program_id(1)
    t0 = qi * tile
    lanes = NSA_HEADS * tile
    stack = lambda x: jnp.concatenate([x] * NSA_HEADS, axis=1)
    qs = jnp.concatenate([qt_ref[0, _head_rows(h, HEAD_DIM), :] for h in range(NSA_HEADS)], axis=1)
    t_q = t0 + lax.broadcasted_iota(jnp.int32, (1, tile), 1)
    t_lane = stack(t_q)
    gates = g_ref[0]

    def gate_row(branch):
        return jnp.concatenate([gates[3 * h + branch:3 * h + branch + 1, :] for h in range(NSA_HEADS)], axis=1)

    s = _dot(kc_ref[0], qs)
    block_end = CMP_STRIDE * lax.broadcasted_iota(jnp.int32, (N_CMP, 1), 0) + (CMP_BLOCK - 1)
    vis = block_end <= t_lane
    s = jnp.where(vis, s, NEG_BIG)
    p = jnp.where(vis, jnp.exp2(s - jnp.max(s, axis=0, keepdims=True)), 0.0)
    l = jnp.sum(p, axis=0, keepdims=True)
    p = p * (1.0 / jnp.where(l > 0.0, l, 1.0))
    out_ref[...] = gate_row(0) * _dot(vct_ref[0], p.astype(BF16))

    p_sum = p[:, 0:tile]
    for h in range(1, NSA_HEADS):
        p_sum = p_sum + p[:, h * tile:(h + 1) * tile]
    imp = jnp.dot(ovl_ref[...], p_sum, preferred_element_type=F32, precision=lax.Precision.HIGHEST)
    j_idx = lax.broadcasted_iota(jnp.int32, (N_SLC, 1), 0)
    blk_t = t_q // SLC_BLOCK
    forced = (j_idx == 0) | (j_idx == blk_t) | (j_idx == blk_t - 1)
    valid = j_idx * SLC_BLOCK <= t_q
    score = jnp.where(forced, 1e9, jnp.where(valid, imp, -1.0))
    rank = jnp.zeros((N_SLC, tile), F32)
    for i in range(N_SLC):
        si = score[i:i + 1, :]
        tie = jnp.where(j_idx > i, 1.0, 0.0)
        rank = rank + jnp.where(si > score, 1.0, 0.0) + jnp.where(si == score, tie, 0.0)
    sel_neg = jnp.where(rank < float(SLC_TOPK), 0.0, NEG_BIG).astype(BF16)
    q_aug = jnp.concatenate([qs, stack(sel_neg), jnp.zeros((LANES - HEAD_DIM - N_SLC, lanes), BF16)], axis=0)

    def branch(k_ref, vt_ref, lo, window):
        acc_ref[...] = jnp.zeros(acc_ref.shape, F32)

        def step(c, masked, m, l):
            k0 = pl.multiple_of(c * tile, tile)
            s = _dot(k_ref[0, pl.ds(k0, tile), :], q_aug)
            if masked:
                k_pos = k0 + lax.broadcasted_iota(jnp.int32, (tile, 1), 0)
                ok = k_pos <= t_lane
                if window:
                    ok = ok & (k_pos > t_lane - WINDOW)
                s = jnp.where(ok, s, NEG_BIG)
            return _flash_step_t(s, vt_ref[0, :, pl.ds(k0, tile)], m, l, acc_ref, slice(None))

        init = (jnp.full((1, lanes), NEG_BIG, F32), jnp.zeros((1, lanes), F32))
        m, l = lax.fori_loop(lo, qi, lambda c, carry: step(c, window, *carry), init)
        m, l = step(qi, True, m, l)
        return acc_ref[...] * (1.0 / l)

    out_ref[...] += gate_row(1) * branch(ks_ref, vst_ref, 0, False)
    out_ref[...] += gate_row(2) * branch(kw_ref, vwt_ref, jnp.maximum(qi - WINDOW // tile, 0), True)
    out = out_ref[...]
    o_t = jnp.concatenate([out[:, h * tile:(h + 1) * tile] for h in range(NSA_HEADS)], axis=0)
    o_ref[0] = o_t.T.astype(BF16)


def _nsa_attention(nq_t, kc, vc_t, kslc, vslc_t, kwin, vwin_t, gates_t, ovl):
    b, w, s = nq_t.shape
    tile = ATT_TILE
    lanes = NSA_HEADS * tile
    seq = lambda i, j: (i, 0, 0)
    return pl.pallas_call(
        functools.partial(_nsa_kernel, tile=tile),
        grid=(b, s // tile),
        in_specs=[
            pl.BlockSpec((1, w, tile), lambda i, j: (i, 0, j)),
            pl.BlockSpec((1, N_CMP, HEAD_DIM), seq),
            pl.BlockSpec((1, HEAD_DIM, N_CMP), seq),
            pl.BlockSpec((1, s, LANES), seq),
            pl.BlockSpec((1, HEAD_DIM, s), seq),
            pl.BlockSpec((1, s, LANES), seq),
            pl.BlockSpec((1, HEAD_DIM, s), seq),
            pl.BlockSpec((1, GATE_ROWS, tile), lambda i, j: (i, 0, j)),
            pl.BlockSpec((N_SLC, N_CMP), lambda i, j: (0, 0)),
        ],
        out_specs=pl.BlockSpec((1, tile, w), lambda i, j: (i, j, 0)),
        out_shape=jax.ShapeDtypeStruct((b, s, w), BF16),
        scratch_shapes=[pltpu.VMEM((HEAD_DIM, lanes), F32), pltpu.VMEM((HEAD_DIM, lanes), F32)],
        compiler_params=_cparams("parallel", "arbitrary"),
        name="nsa_attention",
    )(nq_t, kc, vc_t, kslc, vslc_t, kwin, vwin_t, gates_t, ovl)


def _out_ln_kernel(x_ref, on_ref, od_ref, of_ref, wn_ref, wd_ref, wf_ref, lng_ref, lnb_ref, o_ref):
    h = _dot(on_ref[...], wn_ref[...]) + _dot(od_ref[...], wd_ref[...]) + _dot(of_ref[...], wf_ref[...])
    o_ref[...] = _layer_norm(DEEPNORM_ALPHA * x_ref[...] + h, lng_ref[...], lnb_ref[...])


def _out_ln(x, o_nsa, o_diff, o_fox, wn, wd, wf, ln_g, ln_b):
    t = x.shape[0]
    tm = ROW_TILE
    row = lambda i: (i, 0)
    c2 = lambda i: (0, 0)
    return pl.pallas_call(
        _out_ln_kernel,
        grid=(t // tm,),
        in_specs=[
            pl.BlockSpec((tm, D_MODEL), row),
            pl.BlockSpec((tm, NSA_WIDTH), row),
            pl.BlockSpec((tm, DIFF_WIDTH), row),
            pl.BlockSpec((tm, FOX_WIDTH), row),
            pl.BlockSpec((NSA_WIDTH, D_MODEL), c2),
            pl.BlockSpec((DIFF_WIDTH, D_MODEL), c2),
            pl.BlockSpec((FOX_WIDTH, D_MODEL), c2),
            pl.BlockSpec((1, D_MODEL), c2),
            pl.BlockSpec((1, D_MODEL), c2),
        ],
        out_specs=pl.BlockSpec((tm, D_MODEL), row),
        out_shape=jax.ShapeDtypeStruct((t, D_MODEL), F32),
        compiler_params=_cparams("parallel"),
        name="out_ln",
    )(x, o_nsa, o_diff, o_fox, wn, wd, wf, ln_g, ln_b)


def _prep_ffn(wg, wu, wd):
    pad = D_FF_PAD - D_FF

    def up(w):
        w = jnp.pad(w, ((0, 0), (0, pad))).astype(BF16)
        return w.reshape(D_MODEL, FF_CHUNKS, FF_CHUNK).transpose(1, 0, 2)

    wd = jnp.pad(wd, ((0, pad), (0, 0))).astype(BF16).reshape(FF_CHUNKS, FF_CHUNK, D_MODEL)
    return up(wg), up(wu), wd


def _rot_cols(w, d):
    k, n = w.shape
    w = w.reshape(k, n // d, d)
    return jnp.concatenate([-w[..., d // 2:], w[..., :d // 2]], axis=-1).reshape(k, n)


def _rope_table(pos, d):
    inv = ROPE_THETA ** (-jnp.arange(0, d, 2, dtype=F32) / d)
    ang = pos.astype(F32)[:, None] * inv[None, :]
    cos = jnp.concatenate([jnp.cos(ang), jnp.cos(ang)], axis=-1)
    sin = jnp.concatenate([jnp.sin(ang), jnp.sin(ang)], axis=-1)
    return cos, sin


def _prep_in_proj(w_in, fox_b_f, s):
    o = 0
    nsa_q = w_in[:, o:o + NSA_WIDTH]; o += NSA_WIDTH
    kv = [w_in[:, o + i * HEAD_DIM:o + (i + 1) * HEAD_DIM] for i in range(6)]; o += 6 * HEAD_DIM
    k_cmp, v_cmp, k_slc, v_slc, k_win, v_win = kv
    nsa_g = w_in[:, o:o + N_NSA_GATES]; o += N_NSA_GATES
    diff_q = w_in[:, o:o + DIFF_WIDTH]; o += DIFF_WIDTH
    diff_k = w_in[:, o:o + DIFF_WIDTH]; o += DIFF_WIDTH
    diff_v = w_in[:, o:o + DIFF_WIDTH]; o += DIFF_WIDTH
    fox_q = w_in[:, o:o + FOX_WIDTH]; o += FOX_WIDTH
    fox_k = w_in[:, o:o + FOX_WIDTH]; o += FOX_WIDTH
    fox_v = w_in[:, o:o + FOX_WIDTH]; o += FOX_WIDTH
    fox_f = w_in[:, o:o + FOX_HEADS]
    zero = jnp.zeros((D_MODEL, LANES - HEAD_DIM), F32)

    wa = jnp.concatenate([k_slc, zero, k_win, zero, diff_k], axis=1)
    war = jnp.concatenate([_rot_cols(k_slc, HEAD_DIM), zero, _rot_cols(k_win, HEAD_DIM), zero,
                           _rot_cols(diff_k, DIFF_QK_DIM)], axis=1)
    wb = jnp.concatenate([fox_k, k_cmp, v_cmp], axis=1)
    wc = jnp.concatenate([fox_f, jnp.zeros((D_MODEL, LANES - FOX_HEADS), F32)], axis=1)
    bc = jnp.concatenate([fox_b_f, jnp.zeros((LANES - FOX_HEADS,), F32)])[None, :]
    wd = jnp.concatenate([nsa_q, diff_q], axis=1).T
    wdr = jnp.concatenate([_rot_cols(nsa_q, HEAD_DIM), _rot_cols(diff_q, DIFF_QK_DIM)], axis=1).T
    we = jnp.concatenate([v_slc, v_win, diff_v, fox_q * (HEAD_DIM ** -0.5 * LOG2E), fox_v], axis=1).T
    wf = jnp.concatenate([nsa_g, jnp.zeros((D_MODEL, GATE_ROWS - N_NSA_GATES), F32)], axis=1).T

    pos = jnp.arange(s, dtype=jnp.int32)
    c64, s64 = _rope_table(pos, HEAD_DIM)
    c32, s32 = _rope_table(pos, DIFF_QK_DIM)
    nsa_scale = HEAD_DIM ** -0.5 * LOG2E
    diff_scale = DIFF_QK_DIM ** -0.5 * LOG2E

    def tm_table(t64, t32):
        return jnp.concatenate([t64, t64, t64, t64, jnp.tile(t32, (1, DIFF_MAPS))], axis=1)

    def fm_table(t64, t32):
        return jnp.concatenate([jnp.tile(t64, (1, NSA_HEADS)) * nsa_scale,
                                jnp.tile(t32, (1, DIFF_MAPS)) * diff_scale], axis=1).T

    bf = lambda w: w.astype(BF16)
    return (bf(wa), bf(war), bf(wb), bf(wc), bf(wd), bf(wdr), bf(we), bf(wf),
            tm_table(c64, c32), tm_table(s64, s32), fm_table(c64, c32), fm_table(s64, s32), bc)


def _prep_compress(pos_k, pos_v, phi_k1, phi_k2, phi_v1, phi_v2):
    half = CMP_STRIDE * HEAD_DIM

    def pe(p):
        return jnp.broadcast_to(p.reshape(2, 1, half), (2, 8, half)).astype(BF16)

    block_end = jnp.arange(N_CMP, dtype=jnp.int32) * CMP_STRIDE + (CMP_BLOCK - 1)
    cos_c, sin_c = _rope_table(block_end, HEAD_DIM)
    return (pe(pos_k), pe(pos_v), phi_k1.reshape(2, half, CMP_HIDDEN).astype(BF16), phi_k2.astype(BF16),
            _rot_cols(phi_k2, HEAD_DIM).astype(BF16), phi_v1.reshape(2, half, CMP_HIDDEN).astype(BF16),
            phi_v2.T.astype(BF16), cos_c, sin_c)


def _overlap_matrix(s):
    c0 = np.arange(N_CMP) * CMP_STRIDE
    s0 = np.arange(N_SLC) * SLC_BLOCK
    ovl = (c0[None, :] < s0[:, None] + SLC_BLOCK) & (c0[None, :] + CMP_BLOCK > s0[:, None])
    ovl[:, (s - CMP_BLOCK) // CMP_STRIDE + 1:] = False
    return jnp.asarray(ovl.astype(np.float32))


def kernel(x, p, ln_g, ln_b, ffn1_w_gate, ffn1_w_up, ffn1_w_down, ffn2_w_gate, ffn2_w_up, ffn2_w_down, w_in, fox_b_f, nsa_pos_k, nsa_pos_v, nsa_phi_k1, nsa_phi_k2, nsa_phi_v1, nsa_phi_v2, diff_lambda, diff_subln_g, w_out, ple_w_gate, ple_b_gate, ple_w_proj):
    b, s, _ = x.shape
    assert s // SLC_BLOCK == N_SLC and (s - CMP_BLOCK) // CMP_STRIDE + 1 <= N_CMP
    t = b * s
    ovl = _overlap_matrix(s)
    tri = jnp.asarray(np.tril(np.ones((CUM_TILE, CUM_TILE), np.float32)))
    x = x.reshape(t, D_MODEL)
    for i in range(DEPTH):
        lambda_init = 0.8 - 0.6 * math.exp(-0.3 * i)
        lng = ln_g[i][:, None, :]
        lnb = ln_b[i][:, None, :]
        x = _ffn_ln(x, *_prep_ffn(ffn1_w_gate[i], ffn1_w_up[i], ffn1_w_down[i]), lng[0], lnb[0])
        proj = _in_proj(x.reshape(b, s, D_MODEL), *_prep_in_proj(w_in[i], fox_b_f[i], s))
        kslc, kwin, dk, fk, kcmp, vcmp, fls, nq_t, dq_t, vslc_t, vwin_t, dv_t, fq_t, fv_t, gates_t = proj
        half = CMP_STRIDE * HEAD_DIM
        kc, vc_t = _compress(kcmp.reshape(b, s // CMP_STRIDE, half), vcmp.reshape(b, s // CMP_STRIDE, half),
                             *_prep_compress(nsa_pos_k[i], nsa_pos_v[i], nsa_phi_k1[i], nsa_phi_k2[i],
                                             nsa_phi_v1[i], nsa_phi_v2[i]))
        ccol, crow = _decay(fls, tri)
        o_nsa = _nsa_attention(nq_t, kc, vc_t, kslc, vslc_t, kwin, vwin_t, gates_t, ovl)
        o_diff = _diff_attention(dq_t, dk, dv_t, diff_lambda[i], jnp.tile(diff_subln_g[i], DIFF_HEADS)[None, :],
                                 lambda_init)
        o_fox = _fox_attention(fq_t, fk, fv_t, ccol, crow)
        wo = w_out[i].astype(BF16)
        x = _out_ln(x, o_nsa.reshape(t, NSA_WIDTH), o_diff.reshape(t, DIFF_WIDTH), o_fox.reshape(t, FOX_WIDTH),
                    wo[:NSA_WIDTH], wo[NSA_WIDTH:NSA_WIDTH + DIFF_WIDTH], wo[NSA_WIDTH + DIFF_WIDTH:],
                    lng[1], lnb[1])
        ple = (p[i].reshape(t, PLE_DIM), ple_w_gate[i].astype(BF16), ple_b_gate[i][None, :],
               ple_w_proj[i].astype(BF16))
        x = _ffn_ln(x, *_prep_ffn(ffn2_w_gate[i], ffn2_w_up[i], ffn2_w_down[i]), lng[2], lnb[2], ple=ple)
    return x.reshape(b, s, D_MODEL)
```

```python
import functools
import math

import numpy as np
import jax
import jax.numpy as jnp
from jax import lax
from jax.experimental import pallas as pl
from jax.experimental.pallas import tpu as pltpu

F32 = jnp.float32
BF16 = jnp.bfloat16

D_MODEL = 1024
HEAD_DIM = 64
NSA_HEADS = 4
DIFF_HEADS = 4
FOX_HEADS = 8
NSA_WIDTH = NSA_HEADS * HEAD_DIM
DIFF_WIDTH = DIFF_HEADS * HEAD_DIM
FOX_WIDTH = FOX_HEADS * HEAD_DIM
CMP_BLOCK = 32
CMP_STRIDE = 16
CMP_HIDDEN = 256
SLC_BLOCK = 64
SLC_TOPK = 16
WINDOW = 512
DIFF_QK_DIM = HEAD_DIM // 2
DIFF_MAPS = 2 * DIFF_HEADS
D_FF = 2752
PLE_DIM = 256
ROPE_THETA = 10000.0
LN_EPS = 1e-5
NEG_BIG = -1e30
DEPTH = 2
DEEPNORM_ALPHA = (2.0 * DEPTH) ** 0.25
LOG2E = math.log2(math.e)

LANES = 128
MXU_COLS = 256
BF16_SUBLANES = 16
VMEM_LIMIT_BYTES = 56 * 1024 * 1024

D_FF_PAD = ((D_FF + MXU_COLS - 1) // MXU_COLS) * MXU_COLS
FF_CHUNK = 2 * MXU_COLS
FF_BOUNDS = tuple((c0, min(c0 + FF_CHUNK, D_FF_PAD)) for c0 in range(0, D_FF_PAD, FF_CHUNK))

ROW_TILE = 512
ATT_TILE = 256
CUM_TILE = 256
QK_AHEAD = 4

TM_ROPE_COLS = 2 * LANES + DIFF_WIDTH
TM_PLAIN_COLS = FOX_WIDTH + 2 * HEAD_DIM
FM_ROPE_ROWS = NSA_WIDTH + DIFF_WIDTH
FM_PLAIN_ROWS = 2 * HEAD_DIM + DIFF_WIDTH + 2 * FOX_WIDTH
N_NSA_GATES = 3 * NSA_HEADS
GATE_ROWS = BF16_SUBLANES
N_CMP = 128
N_SLC = 32


def _cparams(*sem):
    return pltpu.CompilerParams(dimension_semantics=sem, vmem_limit_bytes=VMEM_LIMIT_BYTES)


def _dot(a, b):
    return jnp.dot(a, b, preferred_element_type=F32)


def _dot_nt(a, b):
    return lax.dot_general(a, b, (((1,), (1,)), ((), ())), preferred_element_type=F32)


def _sigmoid(z):
    return 1.0 / (1.0 + jnp.exp(-z))


def _layer_norm(y, g, b):
    mu = jnp.mean(y, axis=-1, keepdims=True)
    yc = y - mu
    var = jnp.mean(yc * yc, axis=-1, keepdims=True)
    return yc * lax.rsqrt(var + LN_EPS) * g + b


def _ffn_kernel(*refs, with_ple):
    if with_ple:
        (x_ref, wg_ref, wu_ref, wd_ref, lng_ref, lnb_ref, p_ref, pwg_ref, pbg_ref, pwp_ref,
         o_ref, xb_ref) = refs
    else:
        x_ref, wg_ref, wu_ref, wd_ref, lng_ref, lnb_ref, o_ref, xb_ref = refs
    xb_ref[...] = x_ref[...].astype(BF16)
    xb = xb_ref[...]

    def gate_up(c):
        c0, c1 = FF_BOUNDS[c]
        return _dot(xb, wg_ref[:, c0:c1]), _dot(xb, wu_ref[:, c0:c1])

    nxt = gate_up(0)
    acc = None
    for c, (c0, c1) in enumerate(FF_BOUNDS):
        g, u = nxt
        if c + 1 < len(FF_BOUNDS):
            nxt = gate_up(c + 1)
        h = (g * _sigmoid(g)) * u
        part = _dot(h.astype(BF16), wd_ref[c0:c1, :])
        acc = part if acc is None else acc + part
    y = DEEPNORM_ALPHA * x_ref[...] + 0.5 * acc
    z = _layer_norm(y, lng_ref[...], lnb_ref[...])
    if with_ple:
        gate = _sigmoid(_dot(z.astype(BF16), pwg_ref[...]) + pbg_ref[...])
        z = z + gate * _dot(p_ref[...].astype(BF16), pwp_ref[...])
    o_ref[...] = z


def _ffn_ln(x, wg, wu, wd, ln_g, ln_b, ple=None):
    t = x.shape[0]
    tm = ROW_TILE
    row = lambda i: (i, 0)
    full2 = lambda i: (0, 0)
    in_specs = [
        pl.BlockSpec((tm, D_MODEL), row),
        pl.BlockSpec((D_MODEL, D_FF_PAD), full2),
        pl.BlockSpec((D_MODEL, D_FF_PAD), full2),
        pl.BlockSpec((D_FF_PAD, D_MODEL), full2),
        pl.BlockSpec((1, D_MODEL), full2),
        pl.BlockSpec((1, D_MODEL), full2),
    ]
    args = [x, wg, wu, wd, ln_g, ln_b]
    if ple is not None:
        p, pwg, pbg, pwp = ple
        in_specs += [
            pl.BlockSpec((tm, PLE_DIM), row),
            pl.BlockSpec((D_MODEL, D_MODEL), full2),
            pl.BlockSpec((1, D_MODEL), full2),
            pl.BlockSpec((PLE_DIM, D_MODEL), full2),
        ]
        args += [p, pwg, pbg, pwp]
    return pl.pallas_call(
        functools.partial(_ffn_kernel, with_ple=ple is not None),
        grid=(t // tm,),
        in_specs=in_specs,
        out_specs=pl.BlockSpec((tm, D_MODEL), row),
        out_shape=jax.ShapeDtypeStruct((t, D_MODEL), F32),
        scratch_shapes=[pltpu.VMEM((tm, D_MODEL), BF16)],
        compiler_params=_cparams("parallel"),
        name="ffn_ln_ple" if ple is not None else "ffn_ln",
    )(*args)


def _in_proj_kernel(x_ref, wa_ref, war_ref, wb_ref, wc_ref, wd_ref, wdr_ref, we_ref, wf_ref,
                    cosa_ref, sina_ref, cosd_ref, sind_ref, bc_ref,
                    kslc_ref, kwin_ref, dk_ref, fk_ref, kcmp_ref, vcmp_ref, fls_ref,
                    nq_ref, dq_ref, vslc_ref, vwin_ref, dv_ref, fq_ref, fv_ref, gate_ref, *, tm):
    xb = x_ref[0].astype(BF16)
    ra = _dot(xb, wa_ref[...]) * cosa_ref[...] + _dot(xb, war_ref[...]) * sina_ref[...]
    pos = pl.program_id(1) * tm + lax.broadcasted_iota(jnp.int32, (tm, 1), 0)
    lane = lax.broadcasted_iota(jnp.int32, (1, LANES), 1)
    block_id = jnp.where(lane - HEAD_DIM == pos // SLC_BLOCK, 1.0, 0.0)
    kslc_ref[0] = (ra[:, 0:LANES] + block_id).astype(BF16)
    kwin_ref[0] = ra[:, LANES:2 * LANES].astype(BF16)
    dk_ref[0] = ra[:, 2 * LANES:].astype(BF16)
    pb = _dot(xb, wb_ref[...]).astype(BF16)
    fk_ref[0] = pb[:, 0:FOX_WIDTH]
    kcmp_ref[0] = pb[:, FOX_WIDTH:FOX_WIDTH + HEAD_DIM]
    vcmp_ref[0] = pb[:, FOX_WIDTH + HEAD_DIM:]
    z = _dot(xb, wc_ref[...]) + bc_ref[...]
    fls_ref[0] = jnp.minimum(z, 0.0) - jnp.log(1.0 + jnp.exp(-jnp.abs(z)))

    rd = _dot_nt(wd_ref[...], xb) * cosd_ref[...] + _dot_nt(wdr_ref[...], xb) * sind_ref[...]
    nq_ref[0] = rd[0:NSA_WIDTH].astype(BF16)
    dq_ref[0] = rd[NSA_WIDTH:].astype(BF16)
    pe = _dot_nt(we_ref[...], xb).astype(BF16)
    vslc_ref[0] = pe[0:HEAD_DIM]
    vwin_ref[0] = pe[HEAD_DIM:2 * HEAD_DIM]
    off = 2 * HEAD_DIM
    dv_ref[0] = pe[off:off + DIFF_WIDTH]
    off += DIFF_WIDTH
    fq_ref[0] = pe[off:off + FOX_WIDTH]
    off += FOX_WIDTH
    fv_ref[0] = pe[off:off + FOX_WIDTH]
    gate_ref[0] = _sigmoid(_dot_nt(wf_ref[...], xb))


def _in_proj(x, wa, war, wb, wc, wd, wdr, we, wf, cosa, sina, cosd, sind, bc):
    b, s, _ = x.shape
    tm = ROW_TILE
    grid = (b, s // tm)
    w2 = lambda i, j: (0, 0)
    tok = lambda i, j: (i, j, 0)
    fm = lambda i, j: (i, 0, j)
    in_specs = [
        pl.BlockSpec((1, tm, D_MODEL), tok),
        pl.BlockSpec((D_MODEL, TM_ROPE_COLS), w2),
        pl.BlockSpec((D_MODEL, TM_ROPE_COLS), w2),
        pl.BlockSpec((D_MODEL, TM_PLAIN_COLS), w2),
        pl.BlockSpec((D_MODEL, LANES), w2),
        pl.BlockSpec((FM_ROPE_ROWS, D_MODEL), w2),
        pl.BlockSpec((FM_ROPE_ROWS, D_MODEL), w2),
        pl.BlockSpec((FM_PLAIN_ROWS, D_MODEL), w2),
        pl.BlockSpec((GATE_ROWS, D_MODEL), w2),
        pl.BlockSpec((tm, TM_ROPE_COLS), lambda i, j: (j, 0)),
        pl.BlockSpec((tm, TM_ROPE_COLS), lambda i, j: (j, 0)),
        pl.BlockSpec((FM_ROPE_ROWS, tm), lambda i, j: (0, j)),
        pl.BlockSpec((FM_ROPE_ROWS, tm), lambda i, j: (0, j)),
        pl.BlockSpec((1, LANES), w2),
    ]

    def tok_out(width, dtype=BF16):
        return pl.BlockSpec((1, tm, width), tok), jax.ShapeDtypeStruct((b, s, width), dtype)

    def fm_out(rows, dtype=BF16):
        return pl.BlockSpec((1, rows, tm), fm), jax.ShapeDtypeStruct((b, rows, s), dtype)

    outs = [
        tok_out(LANES), tok_out(LANES), tok_out(DIFF_WIDTH), tok_out(FOX_WIDTH),
        tok_out(HEAD_DIM), tok_out(HEAD_DIM), tok_out(LANES, F32),
        fm_out(NSA_WIDTH), fm_out(DIFF_WIDTH),
        fm_out(HEAD_DIM), fm_out(HEAD_DIM), fm_out(DIFF_WIDTH),
        fm_out(FOX_WIDTH), fm_out(FOX_WIDTH), fm_out(GATE_ROWS, F32),
    ]
    return pl.pallas_call(
        functools.partial(_in_proj_kernel, tm=tm),
        grid=grid,
        in_specs=in_specs,
        out_specs=[o[0] for o in outs],
        out_shape=[o[1] for o in outs],
        compiler_params=_cparams("parallel", "parallel"),
        name="in_proj",
    )(x, wa, war, wb, wc, wd, wdr, we, wf, cosa, sina, cosd, sind, bc)


def _gelu_tanh(x):
    return 0.5 * x * (1.0 + jnp.tanh(math.sqrt(2.0 / math.pi) * (x + 0.044715 * (x * x * x))))


def _compress_kernel(k16_ref, v16_ref, pek_ref, pev_ref, wk1_ref, wk2_ref, wk2r_ref, wv1_ref, wv2t_ref,
                     cos_ref, sin_ref, kc_ref, vct_ref):
    def hidden(x16, pe_ref, w1_ref):
        top = _dot(x16, w1_ref[0])
        bot = _dot(x16, w1_ref[1])
        bias = _dot(pe_ref[0], w1_ref[0]) + _dot(pe_ref[1], w1_ref[1])
        bot = pltpu.roll(bot, N_CMP - 1, 0)
        return _gelu_tanh(top + bot + bias[0:1, :]).astype(BF16)

    hk = hidden(k16_ref[0], pek_ref, wk1_ref)
    kc = _dot(hk, wk2_ref[...]) * cos_ref[...] + _dot(hk, wk2r_ref[...]) * sin_ref[...]
    kc_ref[0] = kc.astype(BF16)
    hv = hidden(v16_ref[0], pev_ref, wv1_ref)
    vct_ref[0] = _dot_nt(wv2t_ref[...], hv).astype(BF16)


def _compress(k16, v16, pek, pev, wk1, wk2, wk2r, wv1, wv2t, cos_c, sin_c):
    b = k16.shape[0]
    half = CMP_STRIDE * HEAD_DIM
    bat = lambda i: (i, 0, 0)
    c2 = lambda i: (0, 0)
    c3 = lambda i: (0, 0, 0)
    in_specs = [
        pl.BlockSpec((1, N_CMP, half), bat),
        pl.BlockSpec((1, N_CMP, half), bat),
        pl.BlockSpec((2, 8, half), c3),
        pl.BlockSpec((2, 8, half), c3),
        pl.BlockSpec((2, half, CMP_HIDDEN), c3),
        pl.BlockSpec((CMP_HIDDEN, HEAD_DIM), c2),
        pl.BlockSpec((CMP_HIDDEN, HEAD_DIM), c2),
        pl.BlockSpec((2, half, CMP_HIDDEN), c3),
        pl.BlockSpec((HEAD_DIM, CMP_HIDDEN), c2),
        pl.BlockSpec((N_CMP, HEAD_DIM), c2),
        pl.BlockSpec((N_CMP, HEAD_DIM), c2),
    ]
    return pl.pallas_call(
        _compress_kernel,
        grid=(b,),
        in_specs=in_specs,
        out_specs=[pl.BlockSpec((1, N_CMP, HEAD_DIM), bat), pl.BlockSpec((1, HEAD_DIM, N_CMP), bat)],
        out_shape=[jax.ShapeDtypeStruct((b, N_CMP, HEAD_DIM), BF16), jax.ShapeDtypeStruct((b, HEAD_DIM, N_CMP), BF16)],
        compiler_params=_cparams("parallel"),
        name="nsa_compress",
    )(k16, v16, pek, pev, wk1, wk2, wk2r, wv1, wv2t, cos_c, sin_c)


def _decay_kernel(fls_ref, tri_ref, ccol_ref, crow_ref, *, n_tiles):
    carry = jnp.zeros((1, LANES), F32)
    for i in range(n_tiles):
        blk = fls_ref[0, i * CUM_TILE:(i + 1) * CUM_TILE, :]
        c = jnp.dot(tri_ref[...], blk, preferred_element_type=F32, precision=lax.Precision.HIGHEST) + carry
        c2 = c * LOG2E
        ccol_ref[0, i * CUM_TILE:(i + 1) * CUM_TILE, :] = c2
        crow_ref[0, :, i * CUM_TILE:(i + 1) * CUM_TILE] = c2.T[0:FOX_HEADS, :]
        carry = c[CUM_TILE - 1:CUM_TILE, :]


def _decay(fls, tri):
    b, s, _ = fls.shape
    return pl.pallas_call(
        functools.partial(_decay_kernel, n_tiles=s // CUM_TILE),
        grid=(b,),
        in_specs=[pl.BlockSpec((1, s, LANES), lambda i: (i, 0, 0)),
                  pl.BlockSpec((CUM_TILE, CUM_TILE), lambda i: (0, 0))],
        out_specs=[pl.BlockSpec((1, s, LANES), lambda i: (i, 0, 0)),
                   pl.BlockSpec((1, FOX_HEADS, s), lambda i: (i, 0, 0))],
        out_shape=[jax.ShapeDtypeStruct((b, s, LANES), F32), jax.ShapeDtypeStruct((b, FOX_HEADS, s), F32)],
        compiler_params=_cparams("parallel"),
        name="fox_decay",
    )(fls, tri)


def _key_minus_query(tile):
    return (lax.broadcasted_iota(jnp.int32, (tile, tile), 0) - lax.broadcasted_iota(jnp.int32, (tile, tile), 1))


def _head_rows(h, width):
    return slice(h * width, (h + 1) * width)


def _attend_t(n, tile, qi, lo, scores, values, acc_ref, sbuf_ref, *, post=None, shift=None, mask=None):
    ahead = min(QK_AHEAD, n)
    acc_ref[...] = jnp.zeros(acc_ref.shape, F32)
    k_lo = pl.multiple_of(lo * tile, tile)
    for i in range(ahead):
        sbuf_ref[i] = scores(i, k_lo)
    ones = jnp.ones((BF16_SUBLANES, tile), BF16)

    def step(c, last, ms, ls):
        k0 = pl.multiple_of(c * tile, tile)
        k1 = pl.multiple_of(c * tile + tile, tile)
        ms, ls = list(ms), list(ls)
        pend = [sbuf_ref[i] for i in range(ahead)]
        for i in range(n):
            s = pend.pop(0)
            j = i + ahead
            if j < n:
                pend.append(scores(j, k0))
            elif not last:
                sbuf_ref[j - n] = scores(j - n, k1)
            if post is not None:
                s = post(i, s, k0)
            mk = None if mask is None else mask(k0, last)
            if mk is not None:
                s = jnp.where(mk, s, NEG_BIG)
            s_max = jnp.max(s, axis=0, keepdims=True)
            sh = None if shift is None else shift(i)
            if sh is not None:
                s_max = s_max + sh
            m_new = jnp.maximum(ms[i], s_max)
            a = jnp.exp2(ms[i] - m_new)
            p = jnp.exp2(s + ((sh - m_new) if sh is not None else -m_new))
            pv = _dot(jnp.concatenate([values(i, k0), ones], axis=0), p.astype(BF16))
            rows = _head_rows(i, HEAD_DIM)
            acc_ref[rows, :] = a * acc_ref[rows, :] + pv[0:HEAD_DIM]
            ls[i] = a * ls[i] + pv[HEAD_DIM:HEAD_DIM + 1]
            ms[i] = m_new
        return tuple(ms), tuple(ls)

    init = (tuple(jnp.full((1, tile), NEG_BIG, F32) for _ in range(n)),
            tuple(jnp.zeros((1, tile), F32) for _ in range(n)))
    ms, ls = lax.fori_loop(lo, qi, lambda c, carry: step(c, False, *carry), init)
    _, ls = step(qi, True, ms, ls)
    return ls


def _pad_queries(q_t, qpad_ref, n, width):
    per = MXU_COLS // width
    row = lax.broadcasted_iota(jnp.int32, (MXU_COLS, q_t.shape[1]), 0)
    for i in range(n):
        g, r = divmod(i, per)
        qg = q_t[g * MXU_COLS:(g + 1) * MXU_COLS]
        qpad_ref[i] = jnp.where((row >= r * width) & (row < (r + 1) * width), qg, jnp.zeros_like(qg))


def _fox_kernel(qt_ref, k_ref, vt_ref, ccol_ref, crow_ref, o_ref, qpad_ref, acc_ref, sbuf_ref, *, tile):
    qi = pl.program_id(1)
    q0 = pl.multiple_of(qi * tile, tile)
    _pad_queries(qt_ref[0], qpad_ref, FOX_HEADS, HEAD_DIM)
    per = MXU_COLS // HEAD_DIM

    def scores(h, k0):
        g = h // per
        return _dot(k_ref[0, pl.ds(k0, tile), g * MXU_COLS:(g + 1) * MXU_COLS], qpad_ref[h])

    ls = _attend_t(
        FOX_HEADS, tile, qi, 0, scores,
        lambda h, k0: vt_ref[0, _head_rows(h, HEAD_DIM), pl.ds(k0, tile)],
        acc_ref, sbuf_ref,
        post=lambda h, s, k0: s - ccol_ref[0, pl.ds(k0, tile), h:h + 1],
        shift=lambda h: crow_ref[0, h:h + 1, pl.ds(q0, tile)],
        mask=lambda k0, last: (_key_minus_query(tile) <= 0) if last else None)
    for h in range(FOX_HEADS):
        rows = _head_rows(h, HEAD_DIM)
        acc_ref[rows, :] = acc_ref[rows, :] * (1.0 / ls[h])
    o_ref[0] = acc_ref[...].T.astype(BF16)


def _fox_attention(fq_t, fk, fv_t, ccol, crow):
    b, w, s = fq_t.shape
    tile = ATT_TILE
    return pl.pallas_call(
        functools.partial(_fox_kernel, tile=tile),
        grid=(b, s // tile),
        in_specs=[
            pl.BlockSpec((1, w, tile), lambda i, j: (i, 0, j)),
            pl.BlockSpec((1, s, w), lambda i, j: (i, 0, 0)),
            pl.BlockSpec((1, w, s), lambda i, j: (i, 0, 0)),
            pl.BlockSpec((1, s, LANES), lambda i, j: (i, 0, 0)),
            pl.BlockSpec((1, FOX_HEADS, s), lambda i, j: (i, 0, 0)),
        ],
        out_specs=pl.BlockSpec((1, tile, w), lambda i, j: (i, j, 0)),
        out_shape=jax.ShapeDtypeStruct((b, s, w), BF16),
        scratch_shapes=[pltpu.VMEM((FOX_HEADS, MXU_COLS, tile), BF16), pltpu.VMEM((w, tile), F32),
                        pltpu.VMEM((QK_AHEAD, tile, tile), F32)],
        compiler_params=_cparams("parallel", "arbitrary"),
        name="fox_attention",
    )(fq_t, fk, fv_t, ccol, crow)


def _diff_kernel(qt_ref, k_ref, vt_ref, lam_ref, g_ref, o_ref, qpad_ref, acc_ref, sbuf_ref, *, tile, lambda_init):
    qi = pl.program_id(1)
    _pad_queries(qt_ref[0], qpad_ref, DIFF_MAPS, DIFF_QK_DIM)
    ls = _attend_t(
        DIFF_MAPS, tile, qi, 0,
        lambda i, k0: _dot(k_ref[0, pl.ds(k0, tile), :], qpad_ref[i]),
        lambda i, k0: vt_ref[0, _head_rows(i // 2, HEAD_DIM), pl.ds(k0, tile)],
        acc_ref, sbuf_ref,
        mask=lambda k0, last: (_key_minus_query(tile) <= 0) if last else None)
    lp = lam_ref[...]
    lam = (jnp.exp(jnp.sum(lp[0:1] * lp[1:2], axis=1, keepdims=True))
           - jnp.exp(jnp.sum(lp[2:3] * lp[3:4], axis=1, keepdims=True)) + lambda_init)
    heads = []
    for h in range(DIFF_HEADS):
        o1 = acc_ref[_head_rows(2 * h, HEAD_DIM), :] * (1.0 / ls[2 * h])
        o2 = acc_ref[_head_rows(2 * h + 1, HEAD_DIM), :] * (1.0 / ls[2 * h + 1])
        o = o1 - lam * o2
        o = o * lax.rsqrt(jnp.mean(o * o, axis=0, keepdims=True) + LN_EPS)
        heads.append(o * (1.0 - lambda_init))
    o_t = jnp.concatenate(heads, axis=0)
    o_ref[0] = (o_t.T * g_ref[...]).astype(BF16)


def _diff_attention(dq_t, dk, dv_t, lam_params, subln_g, lambda_init):
    b, w, s = dq_t.shape
    tile = ATT_TILE
    return pl.pallas_call(
        functools.partial(_diff_kernel, tile=tile, lambda_init=lambda_init),
        grid=(b, s // tile),
        in_specs=[
            pl.BlockSpec((1, w, tile), lambda i, j: (i, 0, j)),
            pl.BlockSpec((1, s, w), lambda i, j: (i, 0, 0)),
            pl.BlockSpec((1, w, s), lambda i, j: (i, 0, 0)),
            pl.BlockSpec((4, DIFF_QK_DIM), lambda i, j: (0, 0)),
            pl.BlockSpec((1, w), lambda i, j: (0, 0)),
        ],
        out_specs=pl.BlockSpec((1, tile, w), lambda i, j: (i, j, 0)),
        out_shape=jax.ShapeDtypeStruct((b, s, w), BF16),
        scratch_shapes=[pltpu.VMEM((DIFF_MAPS, MXU_COLS, tile), BF16), pltpu.VMEM((DIFF_MAPS * HEAD_DIM, tile), F32),
                        pltpu.VMEM((QK_AHEAD, tile, tile), F32)],
        compiler_params=_cparams("parallel", "arbitrary"),
        name="diff_attention",
    )(dq_t, dk, dv_t, lam_params, subln_g)


def _nsa_kernel(qt_ref, kc_ref, vct_ref, ks_ref, vst_ref, kw_ref, vwt_ref, g_ref, ovl_ref,
                o_ref, qaug_ref, acc_ref, out_ref, sbuf_ref, *, tile):
    qi = pl.program_id(1)
    t0 = qi * tile
    t_q = t0 + lax.broadcasted_iota(jnp.int32, (1, tile), 1)
    gates = g_ref[0]

    def add_gated(branch, get, first=False):
        for h in range(NSA_HEADS):
            rows = _head_rows(h, HEAD_DIM)
            term = gates[3 * h + branch:3 * h + branch + 1, :] * get(h)
            out_ref[rows, :] = term if first else out_ref[rows, :] + term

    block_end = CMP_STRIDE * lax.broadcasted_iota(jnp.int32, (N_CMP, 1), 0) + (CMP_BLOCK - 1)
    vis = block_end <= t_q
    p_sum = jnp.zeros((N_CMP, tile), F32)
    o_cmp = []
    for h in range(NSA_HEADS):
        s = jnp.where(vis, _dot(kc_ref[0], qt_ref[0, _head_rows(h, HEAD_DIM), :]), NEG_BIG)
        p = jnp.where(vis, jnp.exp2(s - jnp.max(s, axis=0, keepdims=True)), 0.0)
        l = jnp.sum(p, axis=0, keepdims=True)
        p = p * (1.0 / jnp.where(l > 0.0, l, 1.0))
        p_sum = p_sum + p
        o_cmp.append(_dot(vct_ref[0], p.astype(BF16)))
    add_gated(0, lambda h: o_cmp[h], first=True)

    imp = jnp.dot(ovl_ref[...], p_sum, preferred_element_type=F32, precision=lax.Precision.HIGHEST)
    j_idx = lax.broadcasted_iota(jnp.int32, (N_SLC, 1), 0)
    blk_t = t_q // SLC_BLOCK
    forced = (j_idx == 0) | (j_idx == blk_t) | (j_idx == blk_t - 1)
    valid = j_idx * SLC_BLOCK <= t_q
    score = jnp.where(forced, 1e9, jnp.where(valid, imp, -1.0))
    rank = jnp.zeros((N_SLC, tile), F32)
    for i in range(N_SLC):
        si = score[i:i + 1, :]
        tie = jnp.where(j_idx > i, 1.0, 0.0)
        rank = rank + jnp.where(si > score, 1.0, 0.0) + jnp.where(si == score, tie, 0.0)
    sel_neg = jnp.where(rank < float(SLC_TOPK), 0.0, NEG_BIG).astype(BF16)
    pad = jnp.zeros((LANES - HEAD_DIM - N_SLC, tile), BF16)
    for h in range(NSA_HEADS):
        qaug_ref[h] = jnp.concatenate([qt_ref[0, _head_rows(h, HEAD_DIM), :], sel_neg, pad], axis=0)

    def branch(k_ref, vt_ref, lo, mask):
        ls = _attend_t(
            NSA_HEADS, tile, qi, lo,
            lambda h, k0: _dot(k_ref[0, pl.ds(k0, tile), :], qaug_ref[h]),
            lambda h, k0: vt_ref[0, :, pl.ds(k0, tile)],
            acc_ref, sbuf_ref, mask=mask)
        return lambda h: acc_ref[_head_rows(h, HEAD_DIM), :] * (1.0 / ls[h])

    causal = lambda k0, last: (_key_minus_query(tile) <= 0) if last else None
    add_gated(1, branch(ks_ref, vst_ref, 0, causal))

    def window(k0, last):
        d = _key_minus_query(tile)
        return (d <= 0) if last else (d > t0 - k0 - WINDOW)

    add_gated(2, branch(kw_ref, vwt_ref, jnp.maximum(qi - WINDOW // tile, 0), window))
    o_ref[0] = out_ref[...].T.astype(BF16)


def _nsa_attention(nq_t, kc, vc_t, kslc, vslc_t, kwin, vwin_t, gates_t, ovl):
    b, w, s = nq_t.shape
    tile = ATT_TILE
    seq = lambda i, j: (i, 0, 0)
    return pl.pallas_call(
        functools.partial(_nsa_kernel, tile=tile),
        grid=(b, s // tile),
        in_specs=[
            pl.BlockSpec((1, w, tile), lambda i, j: (i, 0, j)),
            pl.BlockSpec((1, N_CMP, HEAD_DIM), seq),
            pl.BlockSpec((1, HEAD_DIM, N_CMP), seq),
            pl.BlockSpec((1, s, LANES), seq),
            pl.BlockSpec((1, HEAD_DIM, s), seq),
            pl.BlockSpec((1, s, LANES), seq),
            pl.BlockSpec((1, HEAD_DIM, s), seq),
            pl.BlockSpec((1, GATE_ROWS, tile), lambda i, j: (i, 0, j)),
            pl.BlockSpec((N_SLC, N_CMP), lambda i, j: (0, 0)),
        ],
        out_specs=pl.BlockSpec((1, tile, w), lambda i, j: (i, j, 0)),
        out_shape=jax.ShapeDtypeStruct((b, s, w), BF16),
        scratch_shapes=[pltpu.VMEM((NSA_HEADS, LANES, tile), BF16), pltpu.VMEM((w, tile), F32),
                        pltpu.VMEM((w, tile), F32), pltpu.VMEM((QK_AHEAD, tile, tile), F32)],
        compiler_params=_cparams("parallel", "arbitrary"),
        name="nsa_attention",
    )(nq_t, kc, vc_t, kslc, vslc_t, kwin, vwin_t, gates_t, ovl)


def _out_ln_kernel(x_ref, on_ref, od_ref, of_ref, wn_ref, wd_ref, wf_ref, lng_ref, lnb_ref, o_ref):
    h = _dot(on_ref[...], wn_ref[...]) + _dot(od_ref[...], wd_ref[...]) + _dot(of_ref[...], wf_ref[...])
    o_ref[...] = _layer_norm(DEEPNORM_ALPHA * x_ref[...] + h, lng_ref[...], lnb_ref[...])


def _out_ln(x, o_nsa, o_diff, o_fox, wn, wd, wf, ln_g, ln_b):
    t = x.shape[0]
    tm = ROW_TILE
    row = lambda i: (i, 0)
    c2 = lambda i: (0, 0)
    return pl.pallas_call(
        _out_ln_kernel,
        grid=(t // tm,),
        in_specs=[
            pl.BlockSpec((tm, D_MODEL), row),
            pl.BlockSpec((tm, NSA_WIDTH), row),
            pl.BlockSpec((tm, DIFF_WIDTH), row),
            pl.BlockSpec((tm, FOX_WIDTH), row),
            pl.BlockSpec((NSA_WIDTH, D_MODEL), c2),
            pl.BlockSpec((DIFF_WIDTH, D_MODEL), c2),
            pl.BlockSpec((FOX_WIDTH, D_MODEL), c2),
            pl.BlockSpec((1, D_MODEL), c2),
            pl.BlockSpec((1, D_MODEL), c2),
        ],
        out_specs=pl.BlockSpec((tm, D_MODEL), row),
        out_shape=jax.ShapeDtypeStruct((t, D_MODEL), F32),
        compiler_params=_cparams("parallel"),
        name="out_ln",
    )(x, o_nsa, o_diff, o_fox, wn, wd, wf, ln_g, ln_b)


def _prep_ffn(wg, wu, wd):
    pad = D_FF_PAD - D_FF
    up = lambda w: jnp.pad(w, ((0, 0), (0, pad))).astype(BF16)
    return up(wg), up(wu), jnp.pad(wd, ((0, pad), (0, 0))).astype(BF16)


def _rot_cols(w, d):
    k, n = w.shape
    w = w.reshape(k, n // d, d)
    return jnp.concatenate([-w[..., d // 2:], w[..., :d // 2]], axis=-1).reshape(k, n)


def _rope_table(pos, d):
    inv = ROPE_THETA ** (-jnp.arange(0, d, 2, dtype=F32) / d)
    ang = pos.astype(F32)[:, None] * inv[None, :]
    cos = jnp.concatenate([jnp.cos(ang), jnp.cos(ang)], axis=-1)
    sin = jnp.concatenate([jnp.sin(ang), jnp.sin(ang)], axis=-1)
    return cos, sin


def _prep_in_proj(w_in, fox_b_f, s):
    o = 0
    nsa_q = w_in[:, o:o + NSA_WIDTH]; o += NSA_WIDTH
    kv = [w_in[:, o + i * HEAD_DIM:o + (i + 1) * HEAD_DIM] for i in range(6)]; o += 6 * HEAD_DIM
    k_cmp, v_cmp, k_slc, v_slc, k_win, v_win = kv
    nsa_g = w_in[:, o:o + N_NSA_GATES]; o += N_NSA_GATES
    diff_q = w_in[:, o:o + DIFF_WIDTH]; o += DIFF_WIDTH
    diff_k = w_in[:, o:o + DIFF_WIDTH]; o += DIFF_WIDTH
    diff_v = w_in[:, o:o + DIFF_WIDTH]; o += DIFF_WIDTH
    fox_q = w_in[:, o:o + FOX_WIDTH]; o += FOX_WIDTH
    fox_k = w_in[:, o:o + FOX_WIDTH]; o += FOX_WIDTH
    fox_v = w_in[:, o:o + FOX_WIDTH]; o += FOX_WIDTH
    fox_f = w_in[:, o:o + FOX_HEADS]
    zero = jnp.zeros((D_MODEL, LANES - HEAD_DIM), F32)

    wa = jnp.concatenate([k_slc, zero, k_win, zero, diff_k], axis=1)
    war = jnp.concatenate([_rot_cols(k_slc, HEAD_DIM), zero, _rot_cols(k_win, HEAD_DIM), zero,
                           _rot_cols(diff_k, DIFF_QK_DIM)], axis=1)
    wb = jnp.concatenate([fox_k, k_cmp, v_cmp], axis=1)
    wc = jnp.concatenate([fox_f, jnp.zeros((D_MODEL, LANES - FOX_HEADS), F32)], axis=1)
    bc = jnp.concatenate([fox_b_f, jnp.zeros((LANES - FOX_HEADS,), F32)])[None, :]
    wd = jnp.concatenate([nsa_q, diff_q], axis=1).T
    wdr = jnp.concatenate([_rot_cols(nsa_q, HEAD_DIM), _rot_cols(diff_q, DIFF_QK_DIM)], axis=1).T
    we = jnp.concatenate([v_slc, v_win, diff_v, fox_q * (HEAD_DIM ** -0.5 * LOG2E), fox_v], axis=1).T
    wf = jnp.concatenate([nsa_g, jnp.zeros((D_MODEL, GATE_ROWS - N_NSA_GATES), F32)], axis=1).T

    pos = jnp.arange(s, dtype=jnp.int32)
    c64, s64 = _rope_table(pos, HEAD_DIM)
    c32, s32 = _rope_table(pos, DIFF_QK_DIM)
    nsa_scale = HEAD_DIM ** -0.5 * LOG2E
    diff_scale = DIFF_QK_DIM ** -0.5 * LOG2E

    def tm_table(t64, t32):
        return jnp.concatenate([t64, t64, t64, t64, jnp.tile(t32, (1, DIFF_MAPS))], axis=1)

    def fm_table(t64, t32):
        return jnp.concatenate([jnp.tile(t64, (1, NSA_HEADS)) * nsa_scale,
                                jnp.tile(t32, (1, DIFF_MAPS)) * diff_scale], axis=1).T

    bf = lambda w: w.astype(BF16)
    return (bf(wa), bf(war), bf(wb), bf(wc), bf(wd), bf(wdr), bf(we), bf(wf),
            tm_table(c64, c32), tm_table(s64, s32), fm_table(c64, c32), fm_table(s64, s32), bc)


def _prep_compress(pos_k, pos_v, phi_k1, phi_k2, phi_v1, phi_v2):
    half = CMP_STRIDE * HEAD_DIM

    def pe(p):
        return jnp.broadcast_to(p.reshape(2, 1, half), (2, 8, half)).astype(BF16)

    block_end = jnp.arange(N_CMP, dtype=jnp.int32) * CMP_STRIDE + (CMP_BLOCK - 1)
    cos_c, sin_c = _rope_table(block_end, HEAD_DIM)
    return (pe(pos_k), pe(pos_v), phi_k1.reshape(2, half, CMP_HIDDEN).astype(BF16), phi_k2.astype(BF16),
            _rot_cols(phi_k2, HEAD_DIM).astype(BF16), phi_v1.reshape(2, half, CMP_HIDDEN).astype(BF16),
            phi_v2.T.astype(BF16), cos_c, sin_c)


def _overlap_matrix(s):
    c0 = np.arange(N_CMP) * CMP_STRIDE
    s0 = np.arange(N_SLC) * SLC_BLOCK
    ovl = (c0[None, :] < s0[:, None] + SLC_BLOCK) & (c0[None, :] + CMP_BLOCK > s0[:, None])
    ovl[:, (s - CMP_BLOCK) // CMP_STRIDE + 1:] = False
    return jnp.asarray(ovl.astype(np.float32))


def kernel(x, p, ln_g, ln_b, ffn1_w_gate, ffn1_w_up, ffn1_w_down, ffn2_w_gate, ffn2_w_up, ffn2_w_down, w_in, fox_b_f, nsa_pos_k, nsa_pos_v, nsa_phi_k1, nsa_phi_k2, nsa_phi_v1, nsa_phi_v2, diff_lambda, diff_subln_g, w_out, ple_w_gate, ple_b_gate, ple_w_proj):
    b, s, _ = x.shape
    assert s // SLC_BLOCK == N_SLC and (s - CMP_BLOCK) // CMP_STRIDE + 1 <= N_CMP
    t = b * s
    ovl = _overlap_matrix(s)
    tri = jnp.asarray(np.tril(np.ones((CUM_TILE, CUM_TILE), np.float32)))
    x = x.reshape(t, D_MODEL)
    for i in range(DEPTH):
        lambda_init = 0.8 - 0.6 * math.exp(-0.3 * i)
        lng = ln_g[i][:, None, :]
        lnb = ln_b[i][:, None, :]
        x = _ffn_ln(x, *_prep_ffn(ffn1_w_gate[i], ffn1_w_up[i], ffn1_w_down[i]), lng[0], lnb[0])
        proj = _in_proj(x.reshape(b, s, D_MODEL), *_prep_in_proj(w_in[i], fox_b_f[i], s))
        kslc, kwin, dk, fk, kcmp, vcmp, fls, nq_t, dq_t, vslc_t, vwin_t, dv_t, fq_t, fv_t, gates_t = proj
        half = CMP_STRIDE * HEAD_DIM
        kc, vc_t = _compress(kcmp.reshape(b, s // CMP_STRIDE, half), vcmp.reshape(b, s // CMP_STRIDE, half),
                             *_prep_compress(nsa_pos_k[i], nsa_pos_v[i], nsa_phi_k1[i], nsa_phi_k2[i],
                                             nsa_phi_v1[i], nsa_phi_v2[i]))
        ccol, crow = _decay(fls, tri)
        o_nsa = _nsa_attention(nq_t, kc, vc_t, kslc, vslc_t, kwin, vwin_t, gates_t, ovl)
        o_diff = _diff_attention(dq_t, dk, dv_t, diff_lambda[i], jnp.tile(diff_subln_g[i], DIFF_HEADS)[None, :],
                                 lambda_init)
        o_fox = _fox_attention(fq_t, fk, fv_t, ccol, crow)
        wo = w_out[i].astype(BF16)
        x = _out_ln(x, o_nsa.reshape(t, NSA_WIDTH), o_diff.reshape(t, DIFF_WIDTH), o_fox.reshape(t, FOX_WIDTH),
                    wo[:NSA_WIDTH], wo[NSA_WIDTH:NSA_WIDTH + DIFF_WIDTH], wo[NSA_WIDTH + DIFF_WIDTH:],
                    lng[1], lnb[1])
        ple = (p[i].reshape(t, PLE_DIM), ple_w_gate[i].astype(BF16), ple_b_gate[i][None, :],
               ple_w_proj[i].astype(BF16))
        x = _ffn_ln(x, *_prep_ffn(ffn2_w_gate[i], ffn2_w_up[i], ffn2_w_down[i]), lng[2], lnb[2], ple=ple)
    return x.reshape(b, s, D_MODEL)
```

```python
import functools
import math

import numpy as np
import jax
import jax.numpy as jnp
from jax import lax
from jax.experimental import pallas as pl
from jax.experimental.pallas import tpu as pltpu

F32 = jnp.float32
BF16 = jnp.bfloat16

D_MODEL = 1024
HEAD_DIM = 64
NSA_HEADS = 4
DIFF_HEADS = 4
FOX_HEADS = 8
NSA_WIDTH = NSA_HEADS * HEAD_DIM
DIFF_WIDTH = DIFF_HEADS * HEAD_DIM
FOX_WIDTH = FOX_HEADS * HEAD_DIM
CMP_BLOCK = 32
CMP_STRIDE = 16
CMP_HIDDEN = 256
SLC_BLOCK = 64
SLC_TOPK = 16
WINDOW = 512
DIFF_QK_DIM = HEAD_DIM // 2
DIFF_MAPS = 2 * DIFF_HEADS
D_FF = 2752
PLE_DIM = 256
ROPE_THETA = 10000.0
LN_EPS = 1e-5
NEG_BIG = -1e30
DEPTH = 2
DEEPNORM_ALPHA = (2.0 * DEPTH) ** 0.25
LOG2E = math.log2(math.e)

LANES = 128
MXU_COLS = 256
BF16_SUBLANES = 16
VMEM_LIMIT_BYTES = 56 * 1024 * 1024

D_FF_PAD = ((D_FF + MXU_COLS - 1) // MXU_COLS) * MXU_COLS
FF_CHUNK = 2 * MXU_COLS
FF_BOUNDS = tuple((c0, min(c0 + FF_CHUNK, D_FF_PAD)) for c0 in range(0, D_FF_PAD, FF_CHUNK))

ROW_TILE = 512
ATT_Q_TILE = 256
ATT_K_TILE = 256
NSA_Q_TILE = 256
NSA_PRE_TILE = 512
CUM_TILE = 256
QK_AHEAD = 4

TM_ROPE_COLS = 2 * LANES + DIFF_WIDTH
TM_PLAIN_COLS = FOX_WIDTH + 2 * HEAD_DIM
FM_ROPE_ROWS = NSA_WIDTH + DIFF_WIDTH
FM_PLAIN_ROWS = 2 * HEAD_DIM + DIFF_WIDTH + 2 * FOX_WIDTH
N_NSA_GATES = 3 * NSA_HEADS
GATE_ROWS = BF16_SUBLANES
N_CMP = 128
N_SLC = 32


def _cparams(*sem):
    return pltpu.CompilerParams(dimension_semantics=sem, vmem_limit_bytes=VMEM_LIMIT_BYTES)


def _dot(a, b):
    return jnp.dot(a, b, preferred_element_type=F32)


def _dot_nt(a, b):
    return lax.dot_general(a, b, (((1,), (1,)), ((), ())), preferred_element_type=F32)


def _sigmoid(z):
    return 1.0 / (1.0 + jnp.exp(-z))


def _layer_norm(y, g, b):
    mu = jnp.mean(y, axis=-1, keepdims=True)
    yc = y - mu
    var = jnp.mean(yc * yc, axis=-1, keepdims=True)
    return yc * lax.rsqrt(var + LN_EPS) * g + b


def _ffn_kernel(*refs, with_ple):
    if with_ple:
        (x_ref, wg_ref, wu_ref, wd_ref, lng_ref, lnb_ref, p_ref, pwg_ref, pbg_ref, pwp_ref,
         o_ref, xb_ref) = refs
    else:
        x_ref, wg_ref, wu_ref, wd_ref, lng_ref, lnb_ref, o_ref, xb_ref = refs
    xb_ref[...] = x_ref[...].astype(BF16)
    xb = xb_ref[...]

    def gate_up(c):
        c0, c1 = FF_BOUNDS[c]
        return _dot(xb, wg_ref[:, c0:c1]), _dot(xb, wu_ref[:, c0:c1])

    nxt = gate_up(0)
    acc = None
    for c, (c0, c1) in enumerate(FF_BOUNDS):
        g, u = nxt
        if c + 1 < len(FF_BOUNDS):
            nxt = gate_up(c + 1)
        h = (g * _sigmoid(g)) * u
        part = _dot(h.astype(BF16), wd_ref[c0:c1, :])
        acc = part if acc is None else acc + part
    y = DEEPNORM_ALPHA * x_ref[...] + 0.5 * acc
    z = _layer_norm(y, lng_ref[...], lnb_ref[...])
    if with_ple:
        gate = _sigmoid(_dot(z.astype(BF16), pwg_ref[...]) + pbg_ref[...])
        z = z + gate * _dot(p_ref[...].astype(BF16), pwp_ref[...])
    o_ref[...] = z


def _ffn_ln(x, wg, wu, wd, ln_g, ln_b, ple=None):
    t = x.shape[0]
    tm = ROW_TILE
    row = lambda i: (i, 0)
    full2 = lambda i: (0, 0)
    in_specs = [
        pl.BlockSpec((tm, D_MODEL), row),
        pl.BlockSpec((D_MODEL, D_FF_PAD), full2),
        pl.BlockSpec((D_MODEL, D_FF_PAD), full2),
        pl.BlockSpec((D_FF_PAD, D_MODEL), full2),
        pl.BlockSpec((1, D_MODEL), full2),
        pl.BlockSpec((1, D_MODEL), full2),
    ]
    args = [x, wg, wu, wd, ln_g, ln_b]
    if ple is not None:
        p, pwg, pbg, pwp = ple
        in_specs += [
            pl.BlockSpec((tm, PLE_DIM), row),
            pl.BlockSpec((D_MODEL, D_MODEL), full2),
            pl.BlockSpec((1, D_MODEL), full2),
            pl.BlockSpec((PLE_DIM, D_MODEL), full2),
        ]
        args += [p, pwg, pbg, pwp]
    return pl.pallas_call(
        functools.partial(_ffn_kernel, with_ple=ple is not None),
        grid=(t // tm,),
        in_specs=in_specs,
        out_specs=pl.BlockSpec((tm, D_MODEL), row),
        out_shape=jax.ShapeDtypeStruct((t, D_MODEL), F32),
        scratch_shapes=[pltpu.VMEM((tm, D_MODEL), BF16)],
        compiler_params=_cparams("parallel"),
        name="ffn_ln_ple" if ple is not None else "ffn_ln",
    )(*args)


def _swap_halves_lanes(x, d):
    w = x.shape[1]
    lane = lax.broadcasted_iota(jnp.int32, (1, w), 1)
    return jnp.where(lane % d < d // 2, pltpu.roll(x, w - d // 2, 1), pltpu.roll(x, d // 2, 1))


def _swap_halves_rows(x, d):
    parts = []
    for r0 in range(0, x.shape[0], d):
        parts += [x[r0 + d // 2:r0 + d], x[r0:r0 + d // 2]]
    return jnp.concatenate(parts, axis=0)


def _in_proj_kernel(x_ref, wa_ref, wb_ref, wc_ref, wd_ref, we_ref, wf_ref,
                    cosa_ref, sina_ref, cosd_ref, sind_ref, bc_ref,
                    kslc_ref, kwin_ref, dk_ref, fk_ref, kcmp_ref, vcmp_ref, fls_ref,
                    nq_ref, dq_ref, vslc_ref, vwin_ref, dv_ref, fq_ref, fv_ref, gate_ref, *, tm):
    xb = x_ref[0].astype(BF16)
    pa = _dot(xb, wa_ref[...])
    swapped = jnp.concatenate([_swap_halves_lanes(pa[:, 0:2 * LANES], HEAD_DIM),
                               _swap_halves_lanes(pa[:, 2 * LANES:], DIFF_QK_DIM)], axis=1)
    ra = pa * cosa_ref[...] + swapped * sina_ref[...]
    pos = pl.program_id(1) * tm + lax.broadcasted_iota(jnp.int32, (tm, 1), 0)
    lane = lax.broadcasted_iota(jnp.int32, (1, LANES), 1)
    block_id = jnp.where(lane - HEAD_DIM == pos // SLC_BLOCK, 1.0, 0.0)
    kslc_ref[0] = (ra[:, 0:LANES] + block_id).astype(BF16)
    kwin_ref[0] = ra[:, LANES:2 * LANES].astype(BF16)
    dk_ref[0] = ra[:, 2 * LANES:].astype(BF16)
    pb = _dot(xb, wb_ref[...]).astype(BF16)
    fk_ref[0] = pb[:, 0:FOX_WIDTH]
    kcmp_ref[0] = pb[:, FOX_WIDTH:FOX_WIDTH + HEAD_DIM]
    vcmp_ref[0] = pb[:, FOX_WIDTH + HEAD_DIM:]
    z = _dot(xb, wc_ref[...]) + bc_ref[...]
    fls_ref[0] = jnp.minimum(z, 0.0) - jnp.log(1.0 + jnp.exp(-jnp.abs(z)))

    pd = _dot_nt(wd_ref[...], xb)
    swapped = jnp.concatenate([_swap_halves_rows(pd[0:NSA_WIDTH], HEAD_DIM),
                               _swap_halves_rows(pd[NSA_WIDTH:], DIFF_QK_DIM)], axis=0)
    rd = pd * cosd_ref[...] + swapped * sind_ref[...]
    nq_ref[0] = rd[0:NSA_WIDTH].astype(BF16)
    dq_ref[0] = rd[NSA_WIDTH:].astype(BF16)
    pe = _dot_nt(we_ref[...], xb).astype(BF16)
    vslc_ref[0] = pe[0:HEAD_DIM]
    vwin_ref[0] = pe[HEAD_DIM:2 * HEAD_DIM]
    off = 2 * HEAD_DIM
    dv_ref[0] = pe[off:off + DIFF_WIDTH]
    off += DIFF_WIDTH
    fq_ref[0] = pe[off:off + FOX_WIDTH]
    off += FOX_WIDTH
    fv_ref[0] = pe[off:off + FOX_WIDTH]
    gate_ref[0] = _sigmoid(_dot_nt(wf_ref[...], xb))


def _in_proj(x, wa, wb, wc, wd, we, wf, cosa, sina, cosd, sind, bc):
    b, s, _ = x.shape
    tm = ROW_TILE
    grid = (b, s // tm)
    w2 = lambda i, j: (0, 0)
    tok = lambda i, j: (i, j, 0)
    fm = lambda i, j: (i, 0, j)
    in_specs = [
        pl.BlockSpec((1, tm, D_MODEL), tok),
        pl.BlockSpec((D_MODEL, TM_ROPE_COLS), w2),
        pl.BlockSpec((D_MODEL, TM_PLAIN_COLS), w2),
        pl.BlockSpec((D_MODEL, LANES), w2),
        pl.BlockSpec((FM_ROPE_ROWS, D_MODEL), w2),
        pl.BlockSpec((FM_PLAIN_ROWS, D_MODEL), w2),
        pl.BlockSpec((GATE_ROWS, D_MODEL), w2),
        pl.BlockSpec((tm, TM_ROPE_COLS), lambda i, j: (j, 0)),
        pl.BlockSpec((tm, TM_ROPE_COLS), lambda i, j: (j, 0)),
        pl.BlockSpec((FM_ROPE_ROWS, tm), lambda i, j: (0, j)),
        pl.BlockSpec((FM_ROPE_ROWS, tm), lambda i, j: (0, j)),
        pl.BlockSpec((1, LANES), w2),
    ]

    def tok_out(width, dtype=BF16):
        return pl.BlockSpec((1, tm, width), tok), jax.ShapeDtypeStruct((b, s, width), dtype)

    def fm_out(rows, dtype=BF16):
        return pl.BlockSpec((1, rows, tm), fm), jax.ShapeDtypeStruct((b, rows, s), dtype)

    outs = [
        tok_out(LANES), tok_out(LANES), tok_out(DIFF_WIDTH), tok_out(FOX_WIDTH),
        tok_out(HEAD_DIM), tok_out(HEAD_DIM), tok_out(LANES, F32),
        fm_out(NSA_WIDTH), fm_out(DIFF_WIDTH),
        fm_out(HEAD_DIM), fm_out(HEAD_DIM), fm_out(DIFF_WIDTH),
        fm_out(FOX_WIDTH), fm_out(FOX_WIDTH), fm_out(GATE_ROWS, F32),
    ]
    return pl.pallas_call(
        functools.partial(_in_proj_kernel, tm=tm),
        grid=grid,
        in_specs=in_specs,
        out_specs=[o[0] for o in outs],
        out_shape=[o[1] for o in outs],
        compiler_params=_cparams("parallel", "parallel"),
        name="in_proj",
    )(x, wa, wb, wc, wd, we, wf, cosa, sina, cosd, sind, bc)


def _gelu_tanh(x):
    return 0.5 * x * (1.0 + jnp.tanh(math.sqrt(2.0 / math.pi) * (x + 0.044715 * (x * x * x))))


def _compress_kernel(k16_ref, v16_ref, pek_ref, pev_ref, wk1_ref, wk2_ref, wk2r_ref, wv1_ref, wv2t_ref,
                     cos_ref, sin_ref, kc_ref, vct_ref):
    def hidden(x16, pe_ref, w1_ref):
        top = _dot(x16, w1_ref[0])
        bot = _dot(x16, w1_ref[1])
        bias = _dot(pe_ref[0], w1_ref[0]) + _dot(pe_ref[1], w1_ref[1])
        bot = pltpu.roll(bot, N_CMP - 1, 0)
        return _gelu_tanh(top + bot + bias[0:1, :]).astype(BF16)

    hk = hidden(k16_ref[0], pek_ref, wk1_ref)
    kc = _dot(hk, wk2_ref[...]) * cos_ref[...] + _dot(hk, wk2r_ref[...]) * sin_ref[...]
    kc_ref[0] = kc.astype(BF16)
    hv = hidden(v16_ref[0], pev_ref, wv1_ref)
    vct_ref[0] = _dot_nt(wv2t_ref[...], hv).astype(BF16)


def _compress(k16, v16, pek, pev, wk1, wk2, wk2r, wv1, wv2t, cos_c, sin_c):
    b = k16.shape[0]
    half = CMP_STRIDE * HEAD_DIM
    bat = lambda i: (i, 0, 0)
    c2 = lambda i: (0, 0)
    c3 = lambda i: (0, 0, 0)
    in_specs = [
        pl.BlockSpec((1, N_CMP, half), bat),
        pl.BlockSpec((1, N_CMP, half), bat),
        pl.BlockSpec((2, 8, half), c3),
        pl.BlockSpec((2, 8, half), c3),
        pl.BlockSpec((2, half, CMP_HIDDEN), c3),
        pl.BlockSpec((CMP_HIDDEN, HEAD_DIM), c2),
        pl.BlockSpec((CMP_HIDDEN, HEAD_DIM), c2),
        pl.BlockSpec((2, half, CMP_HIDDEN), c3),
        pl.BlockSpec((HEAD_DIM, CMP_HIDDEN), c2),
        pl.BlockSpec((N_CMP, HEAD_DIM), c2),
        pl.BlockSpec((N_CMP, HEAD_DIM), c2),
    ]
    return pl.pallas_call(
        _compress_kernel,
        grid=(b,),
        in_specs=in_specs,
        out_specs=[pl.BlockSpec((1, N_CMP, HEAD_DIM), bat), pl.BlockSpec((1, HEAD_DIM, N_CMP), bat)],
        out_shape=[jax.ShapeDtypeStruct((b, N_CMP, HEAD_DIM), BF16), jax.ShapeDtypeStruct((b, HEAD_DIM, N_CMP), BF16)],
        compiler_params=_cparams("parallel"),
        name="nsa_compress",
    )(k16, v16, pek, pev, wk1, wk2, wk2r, wv1, wv2t, cos_c, sin_c)


def _decay_kernel(fls_ref, tri_ref, ccol_ref, crow_ref, *, n_tiles):
    carry = jnp.zeros((1, LANES), F32)
    for i in range(n_tiles):
        blk = fls_ref[0, i * CUM_TILE:(i + 1) * CUM_TILE, :]
        c = jnp.dot(tri_ref[...], blk, preferred_element_type=F32, precision=lax.Precision.HIGHEST) + carry
        c2 = c * LOG2E
        ccol_ref[0, i * CUM_TILE:(i + 1) * CUM_TILE, :] = c2
        crow_ref[0, :, i * CUM_TILE:(i + 1) * CUM_TILE] = c2.T[0:FOX_HEADS, :]
        carry = c[CUM_TILE - 1:CUM_TILE, :]


def _decay(fls, tri):
    b, s, _ = fls.shape
    return pl.pallas_call(
        functools.partial(_decay_kernel, n_tiles=s // CUM_TILE),
        grid=(b,),
        in_specs=[pl.BlockSpec((1, s, LANES), lambda i: (i, 0, 0)),
                  pl.BlockSpec((CUM_TILE, CUM_TILE), lambda i: (0, 0))],
        out_specs=[pl.BlockSpec((1, s, LANES), lambda i: (i, 0, 0)),
                   pl.BlockSpec((1, FOX_HEADS, s), lambda i: (i, 0, 0))],
        out_shape=[jax.ShapeDtypeStruct((b, s, LANES), F32), jax.ShapeDtypeStruct((b, FOX_HEADS, s), F32)],
        compiler_params=_cparams("parallel"),
        name="fox_decay",
    )(fls, tri)


def _key_minus_query(tk, tq, k0, q0):
    d = lax.broadcasted_iota(jnp.int32, (tk, tq), 0) - lax.broadcasted_iota(jnp.int32, (tk, tq), 1)
    return d, q0 - k0


def _causal_mask(tk, tq, q0):
    def mask(k0, diagonal):
        if not diagonal:
            return None
        d, off = _key_minus_query(tk, tq, k0, q0)
        return d <= off
    return mask


def _head_rows(h, width):
    return slice(h * width, (h + 1) * width)


def _attend_t(n, tq, tk, qi, lo, scores, values, acc_ref, sbuf_ref, *, post=None, shift=None, mask=None):
    ahead = min(QK_AHEAD, n)
    per_tile = tq // tk
    acc_ref[...] = jnp.zeros(acc_ref.shape, F32)
    k_lo = pl.multiple_of(lo * tk, tk)
    for i in range(ahead):
        sbuf_ref[i] = scores(i, k_lo)
    ones = jnp.ones((BF16_SUBLANES, tk), BF16)

    def step(c, diagonal, last, ms, ls):
        k0 = pl.multiple_of(c * tk, tk)
        k1 = pl.multiple_of(c * tk + tk, tk)
        ms, ls = list(ms), list(ls)
        pend = [sbuf_ref[i] for i in range(ahead)]
        for i in range(n):
            s = pend.pop(0)
            j = i + ahead
            if j < n:
                pend.append(scores(j, k0))
            elif not last:
                sbuf_ref[j - n] = scores(j - n, k1)
            if post is not None:
                s = post(i, s, k0)
            mk = None if mask is None else mask(k0, diagonal)
            if mk is not None:
                s = jnp.where(mk, s, NEG_BIG)
            s_max = jnp.max(s, axis=0, keepdims=True)
            sh = None if shift is None else shift(i)
            if sh is not None:
                s_max = s_max + sh
            m_new = jnp.maximum(ms[i], s_max)
            a = jnp.exp2(ms[i] - m_new)
            p = jnp.exp2(s + ((sh - m_new) if sh is not None else -m_new))
            pv = _dot(jnp.concatenate([values(i, k0), ones], axis=0), p.astype(BF16))
            rows = _head_rows(i, HEAD_DIM)
            acc_ref[rows, :] = a * acc_ref[rows, :] + pv[0:HEAD_DIM]
            ls[i] = a * ls[i] + pv[HEAD_DIM:HEAD_DIM + 1]
            ms[i] = m_new
        return tuple(ms), tuple(ls)

    init = (tuple(jnp.full((1, tq), NEG_BIG, F32) for _ in range(n)),
            tuple(jnp.zeros((1, tq), F32) for _ in range(n)))
    first_diag = qi * per_tile
    ms, ls = lax.fori_loop(lo, first_diag, lambda c, carry: step(c, False, False, *carry), init)
    for d in range(per_tile):
        ms, ls = step(first_diag + d, True, d == per_tile - 1, ms, ls)
    return ls


def _pad_queries(q_t, qpad_ref, n, width):
    per = MXU_COLS // width
    row = lax.broadcasted_iota(jnp.int32, (MXU_COLS, q_t.shape[1]), 0)
    for i in range(n):
        g, r = divmod(i, per)
        qg = q_t[g * MXU_COLS:(g + 1) * MXU_COLS]
        qpad_ref[i] = jnp.where((row >= r * width) & (row < (r + 1) * width), qg, jnp.zeros_like(qg))


def _fox_kernel(qt_ref, k_ref, vt_ref, ccol_ref, crow_ref, o_ref, qpad_ref, acc_ref, sbuf_ref, *, tq, tk):
    qi = pl.program_id(1)
    q0 = pl.multiple_of(qi * tq, tq)
    _pad_queries(qt_ref[0], qpad_ref, FOX_HEADS, HEAD_DIM)
    per = MXU_COLS // HEAD_DIM

    def scores(h, k0):
        g = h // per
        return _dot(k_ref[0, pl.ds(k0, tk), g * MXU_COLS:(g + 1) * MXU_COLS], qpad_ref[h])

    ls = _attend_t(
        FOX_HEADS, tq, tk, qi, 0, scores,
        lambda h, k0: vt_ref[0, _head_rows(h, HEAD_DIM), pl.ds(k0, tk)],
        acc_ref, sbuf_ref,
        post=lambda h, s, k0: s - ccol_ref[0, pl.ds(k0, tk), h:h + 1],
        shift=lambda h: crow_ref[0, h:h + 1, pl.ds(q0, tq)],
        mask=_causal_mask(tk, tq, q0))
    for h in range(FOX_HEADS):
        rows = _head_rows(h, HEAD_DIM)
        acc_ref[rows, :] = acc_ref[rows, :] * (1.0 / ls[h])
    o_ref[0] = acc_ref[...].T.astype(BF16)


def _fox_attention(fq_t, fk, fv_t, ccol, crow):
    b, w, s = fq_t.shape
    tq, tk = ATT_Q_TILE, ATT_K_TILE
    return pl.pallas_call(
        functools.partial(_fox_kernel, tq=tq, tk=tk),
        grid=(b, s // tq),
        in_specs=[
            pl.BlockSpec((1, w, tq), lambda i, j: (i, 0, j)),
            pl.BlockSpec((1, s, w), lambda i, j: (i, 0, 0)),
            pl.BlockSpec((1, w, s), lambda i, j: (i, 0, 0)),
            pl.BlockSpec((1, s, LANES), lambda i, j: (i, 0, 0)),
            pl.BlockSpec((1, FOX_HEADS, s), lambda i, j: (i, 0, 0)),
        ],
        out_specs=pl.BlockSpec((1, tq, w), lambda i, j: (i, j, 0)),
        out_shape=jax.ShapeDtypeStruct((b, s, w), BF16),
        scratch_shapes=[pltpu.VMEM((FOX_HEADS, MXU_COLS, tq), BF16), pltpu.VMEM((w, tq), F32),
                        pltpu.VMEM((QK_AHEAD, tk, tq), F32)],
        compiler_params=_cparams("parallel", "arbitrary"),
        name="fox_attention",
    )(fq_t, fk, fv_t, ccol, crow)


def _diff_kernel(qt_ref, k_ref, vt_ref, lam_ref, g_ref, o_ref, qpad_ref, acc_ref, sbuf_ref, *, tq, tk, lambda_init):
    qi = pl.program_id(1)
    _pad_queries(qt_ref[0], qpad_ref, DIFF_MAPS, DIFF_QK_DIM)
    ls = _attend_t(
        DIFF_MAPS, tq, tk, qi, 0,
        lambda i, k0: _dot(k_ref[0, pl.ds(k0, tk), :], qpad_ref[i]),
        lambda i, k0: vt_ref[0, _head_rows(i // 2, HEAD_DIM), pl.ds(k0, tk)],
        acc_ref, sbuf_ref, mask=_causal_mask(tk, tq, qi * tq))
    lp = lam_ref[...]
    lam = (jnp.exp(jnp.sum(lp[0:1] * lp[1:2], axis=1, keepdims=True))
           - jnp.exp(jnp.sum(lp[2:3] * lp[3:4], axis=1, keepdims=True)) + lambda_init)
    heads = []
    for h in range(DIFF_HEADS):
        o1 = acc_ref[_head_rows(2 * h, HEAD_DIM), :] * (1.0 / ls[2 * h])
        o2 = acc_ref[_head_rows(2 * h + 1, HEAD_DIM), :] * (1.0 / ls[2 * h + 1])
        o = o1 - lam * o2
        o = o * lax.rsqrt(jnp.mean(o * o, axis=0, keepdims=True) + LN_EPS)
        heads.append(o * (1.0 - lambda_init))
    o_t = jnp.concatenate(heads, axis=0)
    o_ref[0] = (o_t.T * g_ref[...]).astype(BF16)


def _diff_attention(dq_t, dk, dv_t, lam_params, subln_g, lambda_init):
    b, w, s = dq_t.shape
    tq, tk = ATT_Q_TILE, ATT_K_TILE
    return pl.pallas_call(
        functools.partial(_diff_kernel, tq=tq, tk=tk, lambda_init=lambda_init),
        grid=(b, s // tq),
        in_specs=[
            pl.BlockSpec((1, w, tq), lambda i, j: (i, 0, j)),
            pl.BlockSpec((1, s, w), lambda i, j: (i, 0, 0)),
            pl.BlockSpec((1, w, s), lambda i, j: (i, 0, 0)),
            pl.BlockSpec((4, DIFF_QK_DIM), lambda i, j: (0, 0)),
            pl.BlockSpec((1, w), lambda i, j: (0, 0)),
        ],
        out_specs=pl.BlockSpec((1, tq, w), lambda i, j: (i, j, 0)),
        out_shape=jax.ShapeDtypeStruct((b, s, w), BF16),
        scratch_shapes=[pltpu.VMEM((DIFF_MAPS, MXU_COLS, tq), BF16), pltpu.VMEM((DIFF_MAPS * HEAD_DIM, tq), F32),
                        pltpu.VMEM((QK_AHEAD, tk, tq), F32)],
        compiler_params=_cparams("parallel", "arbitrary"),
        name="diff_attention",
    )(dq_t, dk, dv_t, lam_params, subln_g)


def _nsa_cmp_scores(qt_ref, kc_ref, t0, width):
    cols = slice(t0, t0 + width)
    return [_dot(kc_ref[0], qt_ref[0, _head_rows(h, HEAD_DIM), cols]) for h in range(NSA_HEADS)]


def _nsa_select(scores, vct_ref, g_ref, ovl_ref, sel_ref, ocmp_ref, t0, width):
    cols = slice(t0, t0 + width)
    t_q = t0 + lax.broadcasted_iota(jnp.int32, (1, width), 1)
    block_end = CMP_STRIDE * lax.broadcasted_iota(jnp.int32, (N_CMP, 1), 0) + (CMP_BLOCK - 1)
    vis = block_end <= t_q
    any_vis = t_q >= CMP_BLOCK - 1
    lhs = jnp.concatenate([vct_ref[0], ovl_ref[...], jnp.ones((BF16_SUBLANES, N_CMP), BF16)], axis=0)
    imp = jnp.zeros((N_SLC, width), F32)
    for h in range(NSA_HEADS):
        rows = _head_rows(h, HEAD_DIM)
        s = jnp.where(vis, scores[h], NEG_BIG)
        p = jnp.exp2(s - jnp.max(s, axis=0, keepdims=True))
        r = _dot(lhs, p.astype(BF16))
        l = r[HEAD_DIM + N_SLC:HEAD_DIM + N_SLC + 1]
        scale = jnp.where(any_vis, 1.0 / l, 0.0)
        ocmp_ref[rows, cols] = (g_ref[0, 3 * h:3 * h + 1, cols] * scale) * r[0:HEAD_DIM]
        imp = imp + scale * r[HEAD_DIM:HEAD_DIM + N_SLC]
    j_idx = lax.broadcasted_iota(jnp.int32, (N_SLC, 1), 0)
    blk_t = t_q // SLC_BLOCK
    forced = (j_idx == 0) | (j_idx == blk_t) | (j_idx == blk_t - 1)
    valid = j_idx * SLC_BLOCK <= t_q
    score = jnp.where(forced, 1e9, jnp.where(valid, imp, -1.0))
    groups = [score[g * 8:(g + 1) * 8] for g in range(N_SLC // 8)]
    ranks = [jnp.zeros((8, width), F32) for _ in groups]
    row = lax.broadcasted_iota(jnp.int32, (8, 1), 0)
    for i in range(N_SLC):
        gi, ri = divmod(i, 8)
        si = groups[gi][ri:ri + 1, :]
        for g, sg in enumerate(groups):
            if g < gi:
                ranks[g] = ranks[g] + jnp.where(si > sg, 1.0, 0.0)
            elif g > gi:
                ranks[g] = ranks[g] + jnp.where(si >= sg, 1.0, 0.0)
            else:
                tie = jnp.where(row > ri, 1.0, 0.0)
                ranks[g] = ranks[g] + jnp.where(si > sg, 1.0, 0.0) + jnp.where(si == sg, tie, 0.0)
    rank = jnp.concatenate(ranks, axis=0)
    sel_ref[:, cols] = jnp.where(rank < float(SLC_TOPK), 0.0, NEG_BIG).astype(BF16)


def _nsa_kernel(qt_ref, kc_ref, vct_ref, ks_ref, vst_ref, kw_ref, vwt_ref, g_ref, ovl_ref,
                o_ref, sel_ref, ocmp_ref, qaug_ref, acc_ref, out_ref, sbuf_ref, *, tq, tk, seq):
    qi = pl.program_id(1)
    q0 = pl.multiple_of(qi * tq, tq)

    @pl.when(qi == 0)
    def _():
        nxt = _nsa_cmp_scores(qt_ref, kc_ref, 0, NSA_PRE_TILE)
        for t0 in range(0, seq, NSA_PRE_TILE):
            scores = nxt
            if t0 + NSA_PRE_TILE < seq:
                nxt = _nsa_cmp_scores(qt_ref, kc_ref, t0 + NSA_PRE_TILE, NSA_PRE_TILE)
            _nsa_select(scores, vct_ref, g_ref, ovl_ref, sel_ref, ocmp_ref, t0, NSA_PRE_TILE)

    pad = jnp.zeros((LANES - HEAD_DIM - N_SLC, tq), BF16)
    sel_neg = sel_ref[:, pl.ds(q0, tq)]
    for h in range(NSA_HEADS):
        qaug_ref[h] = jnp.concatenate([qt_ref[0, _head_rows(h, HEAD_DIM), pl.ds(q0, tq)], sel_neg, pad], axis=0)
    out_ref[...] = ocmp_ref[:, pl.ds(q0, tq)]

    def branch(number, k_ref, vt_ref, lo, mask):
        ls = _attend_t(
            NSA_HEADS, tq, tk, qi, lo,
            lambda h, k0: _dot(k_ref[0, pl.ds(k0, tk), :], qaug_ref[h]),
            lambda h, k0: vt_ref[0, :, pl.ds(k0, tk)],
            acc_ref, sbuf_ref, mask=mask)
        for h in range(NSA_HEADS):
            rows = _head_rows(h, HEAD_DIM)
            gate = g_ref[0, 3 * h + number:3 * h + number + 1, pl.ds(q0, tq)]
            out_ref[rows, :] += (gate * (1.0 / ls[h])) * acc_ref[rows, :]

    branch(1, ks_ref, vst_ref, 0, _causal_mask(tk, tq, q0))

    def window(k0, diagonal):
        d, off = _key_minus_query(tk, tq, k0, q0)
        return (d <= off) if diagonal else (d > off - WINDOW)

    branch(2, kw_ref, vwt_ref, jnp.maximum(q0 - WINDOW, 0) // tk, window)
    o_ref[0] = out_ref[...].T.astype(BF16)


def _nsa_attention(nq_t, kc, vc_t, kslc, vslc_t, kwin, vwin_t, gates_t, ovl):
    b, w, s = nq_t.shape
    tq, tk = NSA_Q_TILE, ATT_K_TILE
    seq = lambda i, j: (i, 0, 0)
    return pl.pallas_call(
        functools.partial(_nsa_kernel, tq=tq, tk=tk, seq=s),
        grid=(b, s // tq),
        in_specs=[
            pl.BlockSpec((1, w, s), seq),
            pl.BlockSpec((1, N_CMP, HEAD_DIM), seq),
            pl.BlockSpec((1, HEAD_DIM, N_CMP), seq),
            pl.BlockSpec((1, s, LANES), seq),
            pl.BlockSpec((1, HEAD_DIM, s), seq),
            pl.BlockSpec((1, s, LANES), seq),
            pl.BlockSpec((1, HEAD_DIM, s), seq),
            pl.BlockSpec((1, GATE_ROWS, s), seq),
            pl.BlockSpec((N_SLC, N_CMP), lambda i, j: (0, 0)),
        ],
        out_specs=pl.BlockSpec((1, tq, w), lambda i, j: (i, j, 0)),
        out_shape=jax.ShapeDtypeStruct((b, s, w), BF16),
        scratch_shapes=[pltpu.VMEM((N_SLC, s), BF16), pltpu.VMEM((w, s), F32),
                        pltpu.VMEM((NSA_HEADS, LANES, tq), BF16), pltpu.VMEM((w, tq), F32),
                        pltpu.VMEM((w, tq), F32), pltpu.VMEM((QK_AHEAD, tk, tq), F32)],
        compiler_params=_cparams("parallel", "arbitrary"),
        name="nsa_attention",
    )(nq_t, kc, vc_t, kslc, vslc_t, kwin, vwin_t, gates_t, ovl)


def _out_ln_kernel(x_ref, on_ref, od_ref, of_ref, wn_ref, wd_ref, wf_ref, lng_ref, lnb_ref, o_ref):
    h = _dot(on_ref[...], wn_ref[...]) + _dot(od_ref[...], wd_ref[...]) + _dot(of_ref[...], wf_ref[...])
    o_ref[...] = _layer_norm(DEEPNORM_ALPHA * x_ref[...] + h, lng_ref[...], lnb_ref[...])


def _out_ln(x, o_nsa, o_diff, o_fox, wn, wd, wf, ln_g, ln_b):
    t = x.shape[0]
    tm = ROW_TILE
    row = lambda i: (i, 0)
    c2 = lambda i: (0, 0)
    return pl.pallas_call(
        _out_ln_kernel,
        grid=(t // tm,),
        in_specs=[
            pl.BlockSpec((tm, D_MODEL), row),
            pl.BlockSpec((tm, NSA_WIDTH), row),
            pl.BlockSpec((tm, DIFF_WIDTH), row),
            pl.BlockSpec((tm, FOX_WIDTH), row),
            pl.BlockSpec((NSA_WIDTH, D_MODEL), c2),
            pl.BlockSpec((DIFF_WIDTH, D_MODEL), c2),
            pl.BlockSpec((FOX_WIDTH, D_MODEL), c2),
            pl.BlockSpec((1, D_MODEL), c2),
            pl.BlockSpec((1, D_MODEL), c2),
        ],
        out_specs=pl.BlockSpec((tm, D_MODEL), row),
        out_shape=jax.ShapeDtypeStruct((t, D_MODEL), F32),
        compiler_params=_cparams("parallel"),
        name="out_ln",
    )(x, o_nsa, o_diff, o_fox, wn, wd, wf, ln_g, ln_b)


def _prep_ffn(wg, wu, wd):
    pad = D_FF_PAD - D_FF
    up = lambda w: jnp.pad(w, ((0, 0), (0, pad))).astype(BF16)
    return up(wg), up(wu), jnp.pad(wd, ((0, pad), (0, 0))).astype(BF16)


def _rot_cols(w, d):
    k, n = w.shape
    w = w.reshape(k, n // d, d)
    return jnp.concatenate([-w[..., d // 2:], w[..., :d // 2]], axis=-1).reshape(k, n)


def _rope_table(pos, d, signed=False):
    inv = ROPE_THETA ** (-jnp.arange(0, d, 2, dtype=F32) / d)
    ang = pos.astype(F32)[:, None] * inv[None, :]
    cos = jnp.concatenate([jnp.cos(ang), jnp.cos(ang)], axis=-1)
    sin = jnp.concatenate([-jnp.sin(ang) if signed else jnp.sin(ang), jnp.sin(ang)], axis=-1)
    return cos, sin


def _prep_in_proj(w_in, fox_b_f, s):
    o = 0
    nsa_q = w_in[:, o:o + NSA_WIDTH]; o += NSA_WIDTH
    kv = [w_in[:, o + i * HEAD_DIM:o + (i + 1) * HEAD_DIM] for i in range(6)]; o += 6 * HEAD_DIM
    k_cmp, v_cmp, k_slc, v_slc, k_win, v_win = kv
    nsa_g = w_in[:, o:o + N_NSA_GATES]; o += N_NSA_GATES
    diff_q = w_in[:, o:o + DIFF_WIDTH]; o += DIFF_WIDTH
    diff_k = w_in[:, o:o + DIFF_WIDTH]; o += DIFF_WIDTH
    diff_v = w_in[:, o:o + DIFF_WIDTH]; o += DIFF_WIDTH
    fox_q = w_in[:, o:o + FOX_WIDTH]; o += FOX_WIDTH
    fox_k = w_in[:, o:o + FOX_WIDTH]; o += FOX_WIDTH
    fox_v = w_in[:, o:o + FOX_WIDTH]; o += FOX_WIDTH
    fox_f = w_in[:, o:o + FOX_HEADS]
    zero = jnp.zeros((D_MODEL, LANES - HEAD_DIM), F32)

    wa = jnp.concatenate([k_slc, zero, k_win, zero, diff_k], axis=1)
    wb = jnp.concatenate([fox_k, k_cmp, v_cmp], axis=1)
    wc = jnp.concatenate([fox_f, jnp.zeros((D_MODEL, LANES - FOX_HEADS), F32)], axis=1)
    bc = jnp.concatenate([fox_b_f, jnp.zeros((LANES - FOX_HEADS,), F32)])[None, :]
    wd = jnp.concatenate([nsa_q, diff_q], axis=1).T
    we = jnp.concatenate([v_slc, v_win, diff_v, fox_q * (HEAD_DIM ** -0.5 * LOG2E), fox_v], axis=1).T
    wf = jnp.concatenate([nsa_g, jnp.zeros((D_MODEL, GATE_ROWS - N_NSA_GATES), F32)], axis=1).T

    pos = jnp.arange(s, dtype=jnp.int32)
    c64, s64 = _rope_table(pos, HEAD_DIM, signed=True)
    c32, s32 = _rope_table(pos, DIFF_QK_DIM, signed=True)
    nsa_scale = HEAD_DIM ** -0.5 * LOG2E
    diff_scale = DIFF_QK_DIM ** -0.5 * LOG2E

    def tm_table(t64, t32):
        return jnp.concatenate([t64, t64, t64, t64, jnp.tile(t32, (1, DIFF_MAPS))], axis=1)

    def fm_table(t64, t32):
        return jnp.concatenate([jnp.tile(t64, (1, NSA_HEADS)) * nsa_scale,
                                jnp.tile(t32, (1, DIFF_MAPS)) * diff_scale], axis=1).T

    bf = lambda w: w.astype(BF16)
    return (bf(wa), bf(wb), bf(wc), bf(wd), bf(we), bf(wf),
            tm_table(c64, c32), tm_table(s64, s32), fm_table(c64, c32), fm_table(s64, s32), bc)


def _prep_compress(pos_k, pos_v, phi_k1, phi_k2, phi_v1, phi_v2):
    half = CMP_STRIDE * HEAD_DIM

    def pe(p):
        return jnp.broadcast_to(p.reshape(2, 1, half), (2, 8, half)).astype(BF16)

    block_end = jnp.arange(N_CMP, dtype=jnp.int32) * CMP_STRIDE + (CMP_BLOCK - 1)
    cos_c, sin_c = _rope_table(block_end, HEAD_DIM)
    return (pe(pos_k), pe(pos_v), phi_k1.reshape(2, half, CMP_HIDDEN).astype(BF16), phi_k2.astype(BF16),
            _rot_cols(phi_k2, HEAD_DIM).astype(BF16), phi_v1.reshape(2, half, CMP_HIDDEN).astype(BF16),
            phi_v2.T.astype(BF16), cos_c, sin_c)


def _overlap_matrix(s):
    c0 = np.arange(N_CMP) * CMP_STRIDE
    s0 = np.arange(N_SLC) * SLC_BLOCK
    ovl = (c0[None, :] < s0[:, None] + SLC_BLOCK) & (c0[None, :] + CMP_BLOCK > s0[:, None])
    ovl[:, (s - CMP_BLOCK) // CMP_STRIDE + 1:] = False
    return jnp.asarray(ovl.astype(np.float32), dtype=BF16)


def kernel(x, p, ln_g, ln_b, ffn1_w_gate, ffn1_w_up, ffn1_w_down, ffn2_w_gate, ffn2_w_up, ffn2_w_down, w_in, fox_b_f, nsa_pos_k, nsa_pos_v, nsa_phi_k1, nsa_phi_k2, nsa_phi_v1, nsa_phi_v2, diff_lambda, diff_subln_g, w_out, ple_w_gate, ple_b_gate, ple_w_proj):
    b, s, _ = x.shape
    assert s // SLC_BLOCK == N_SLC and (s - CMP_BLOCK) // CMP_STRIDE + 1 <= N_CMP
    t = b * s
    ovl = _overlap_matrix(s)
    tri = jnp.asarray(np.tril(np.ones((CUM_TILE, CUM_TILE), np.float32)))
    x = x.reshape(t, D_MODEL)
    for i in range(DEPTH):
        lambda_init = 0.8 - 0.6 * math.exp(-0.3 * i)
        lng = ln_g[i][:, None, :]
        lnb = ln_b[i][:, None, :]
        x = _ffn_ln(x, *_prep_ffn(ffn1_w_gate[i], ffn1_w_up[i], ffn1_w_down[i]), lng[0], lnb[0])
        proj = _in_proj(x.reshape(b, s, D_MODEL), *_prep_in_proj(w_in[i], fox_b_f[i], s))
        kslc, kwin, dk, fk, kcmp, vcmp, fls, nq_t, dq_t, vslc_t, vwin_t, dv_t, fq_t, fv_t, gates_t = proj
        half = CMP_STRIDE * HEAD_DIM
        kc, vc_t = _compress(kcmp.reshape(b, s // CMP_STRIDE, half), vcmp.reshape(b, s // CMP_STRIDE, half),
                             *_prep_compress(nsa_pos_k[i], nsa_pos_v[i], nsa_phi_k1[i], nsa_phi_k2[i],
                                             nsa_phi_v1[i], nsa_phi_v2[i]))
        ccol, crow = _decay(fls, tri)
        o_nsa = _nsa_attention(nq_t, kc, vc_t, kslc, vslc_t, kwin, vwin_t, gates_t, ovl)
        o_diff = _diff_attention(dq_t, dk, dv_t, diff_lambda[i], jnp.tile(diff_subln_g[i], DIFF_HEADS)[None, :],
                                 lambda_init)
        o_fox = _fox_attention(fq_t, fk, fv_t, ccol, crow)
        wo = w_out[i].astype(BF16)
        x = _out_ln(x, o_nsa.reshape(t, NSA_WIDTH), o_diff.reshape(t, DIFF_WIDTH), o_fox.reshape(t, FOX_WIDTH),
                    wo[:NSA_WIDTH], wo[NSA_WIDTH:NSA_WIDTH + DIFF_WIDTH], wo[NSA_WIDTH + DIFF_WIDTH:],
                    lng[1], lnb[1])
        ple = (p[i].reshape(t, PLE_DIM), ple_w_gate[i].astype(BF16), ple_b_gate[i][None, :],
               ple_w_proj[i].astype(BF16))
        x = _ffn_ln(x, *_prep_ffn(ffn2_w_gate[i], ffn2_w_up[i], ffn2_w_down[i]), lng[2], lnb[2], ple=ple)
    return x.reshape(b, s, D_MODEL)
```

```python
import functools
import math

import numpy as np
import jax
import jax.numpy as jnp
from jax import lax
from jax.experimental import pallas as pl
from jax.experimental.pallas import tpu as pltpu

F32 = jnp.float32
BF16 = jnp.bfloat16

D_MODEL = 1024
HEAD_DIM = 64
NSA_HEADS = 4
DIFF_HEADS = 4
FOX_HEADS = 8
NSA_WIDTH = NSA_HEADS * HEAD_DIM
DIFF_WIDTH = DIFF_HEADS * HEAD_DIM
FOX_WIDTH = FOX_HEADS * HEAD_DIM
CMP_BLOCK = 32
CMP_STRIDE = 16
CMP_HIDDEN = 256
SLC_BLOCK = 64
SLC_TOPK = 16
WINDOW = 512
DIFF_QK_DIM = HEAD_DIM // 2
DIFF_MAPS = 2 * DIFF_HEADS
D_FF = 2752
PLE_DIM = 256
ROPE_THETA = 10000.0
LN_EPS = 1e-5
NEG_BIG = -1e30
DEPTH = 2
DEEPNORM_ALPHA = (2.0 * DEPTH) ** 0.25
LOG2E = math.log2(math.e)

LANES = 128
MXU_COLS = 256
BF16_SUBLANES = 16
VMEM_LIMIT_BYTES = 56 * 1024 * 1024

D_FF_PAD = ((D_FF + MXU_COLS - 1) // MXU_COLS) * MXU_COLS
FF_CHUNK = 2 * MXU_COLS
FF_BOUNDS = tuple((c0, min(c0 + FF_CHUNK, D_FF_PAD)) for c0 in range(0, D_FF_PAD, FF_CHUNK))

ROW_TILE = 512
ATT_Q_TILE = 256
ATT_K_TILE = 256
NSA_Q_TILE = 512
NSA_PRE_TILE = 512
CUM_TILE = 256
QK_AHEAD = 4

TM_ROPE_COLS = 2 * LANES + DIFF_WIDTH
TM_PLAIN_COLS = FOX_WIDTH + 2 * HEAD_DIM
FM_ROPE_ROWS = NSA_WIDTH + DIFF_WIDTH
FM_PLAIN_ROWS = 2 * HEAD_DIM + DIFF_WIDTH + 2 * FOX_WIDTH
N_NSA_GATES = 3 * NSA_HEADS
GATE_ROWS = BF16_SUBLANES
N_CMP = 128
N_SLC = 32


def _cparams(*sem):
    return pltpu.CompilerParams(dimension_semantics=sem, vmem_limit_bytes=VMEM_LIMIT_BYTES)


def _dot(a, b):
    return jnp.dot(a, b, preferred_element_type=F32)


def _dot_nt(a, b):
    return lax.dot_general(a, b, (((1,), (1,)), ((), ())), preferred_element_type=F32)


def _sigmoid(z):
    return 1.0 / (1.0 + jnp.exp(-z))


def _layer_norm(y, g, b):
    mu = jnp.mean(y, axis=-1, keepdims=True)
    yc = y - mu
    var = jnp.mean(yc * yc, axis=-1, keepdims=True)
    return yc * lax.rsqrt(var + LN_EPS) * g + b


def _ffn_kernel(*refs, with_ple):
    if with_ple:
        (x_ref, wg_ref, wu_ref, wd_ref, lng_ref, lnb_ref, p_ref, pwg_ref, pbg_ref, pwp_ref,
         o_ref, xb_ref) = refs
    else:
        x_ref, wg_ref, wu_ref, wd_ref, lng_ref, lnb_ref, o_ref, xb_ref = refs
    xb_ref[...] = x_ref[...].astype(BF16)
    xb = xb_ref[...]

    def gate_up(c):
        c0, c1 = FF_BOUNDS[c]
        return _dot(xb, wg_ref[:, c0:c1]), _dot(xb, wu_ref[:, c0:c1])

    nxt = gate_up(0)
    acc = None
    for c, (c0, c1) in enumerate(FF_BOUNDS):
        g, u = nxt
        if c + 1 < len(FF_BOUNDS):
            nxt = gate_up(c + 1)
        h = (g * _sigmoid(g)) * u
        part = _dot(h.astype(BF16), wd_ref[c0:c1, :])
        acc = part if acc is None else acc + part
    y = DEEPNORM_ALPHA * x_ref[...] + 0.5 * acc
    z = _layer_norm(y, lng_ref[...], lnb_ref[...])
    if with_ple:
        gate = _sigmoid(_dot(z.astype(BF16), pwg_ref[...]) + pbg_ref[...])
        z = z + gate * _dot(p_ref[0].astype(BF16), pwp_ref[...])
    o_ref[...] = z


def _ffn_ln(x, wg, wu, wd, ln_g, ln_b, ple=None):
    t = x.shape[0]
    tm = ROW_TILE
    row = lambda i: (i, 0)
    full2 = lambda i: (0, 0)
    in_specs = [
        pl.BlockSpec((tm, D_MODEL), row),
        pl.BlockSpec((D_MODEL, D_FF_PAD), full2),
        pl.BlockSpec((D_MODEL, D_FF_PAD), full2),
        pl.BlockSpec((D_FF_PAD, D_MODEL), full2),
        pl.BlockSpec((1, D_MODEL), full2),
        pl.BlockSpec((1, D_MODEL), full2),
    ]
    args = [x, wg, wu, wd, ln_g, ln_b]
    if ple is not None:
        p, layer, pwg, pbg, pwp = ple
        in_specs += [
            pl.BlockSpec((1, tm, PLE_DIM), lambda i: (layer, i, 0)),
            pl.BlockSpec((D_MODEL, D_MODEL), full2),
            pl.BlockSpec((1, D_MODEL), full2),
            pl.BlockSpec((PLE_DIM, D_MODEL), full2),
        ]
        args += [p, pwg, pbg, pwp]
    return pl.pallas_call(
        functools.partial(_ffn_kernel, with_ple=ple is not None),
        grid=(t // tm,),
        in_specs=in_specs,
        out_specs=pl.BlockSpec((tm, D_MODEL), row),
        out_shape=jax.ShapeDtypeStruct((t, D_MODEL), F32),
        scratch_shapes=[pltpu.VMEM((tm, D_MODEL), BF16)],
        compiler_params=_cparams("parallel"),
        name="ffn_ln_ple" if ple is not None else "ffn_ln",
    )(*args)


def _swap_halves_lanes(x, d):
    w = x.shape[1]
    lane = lax.broadcasted_iota(jnp.int32, (1, w), 1)
    return jnp.where(lane % d < d // 2, pltpu.roll(x, w - d // 2, 1), pltpu.roll(x, d // 2, 1))


def _swap_halves_rows(x, d):
    parts = []
    for r0 in range(0, x.shape[0], d):
        parts += [x[r0 + d // 2:r0 + d], x[r0:r0 + d // 2]]
    return jnp.concatenate(parts, axis=0)


def _in_proj_kernel(x_ref, wa_ref, wb_ref, wc_ref, wd_ref, we_ref, wf_ref,
                    cosa_ref, sina_ref, cosd_ref, sind_ref, bc_ref,
                    kslc_ref, kwin_ref, dk_ref, fk_ref, kcmp_ref, vcmp_ref, fls_ref,
                    nq_ref, dq_ref, vslc_ref, vwin_ref, dv_ref, fq_ref, fv_ref, gate_ref, *, tm):
    xb = x_ref[0].astype(BF16)
    pa = _dot(xb, wa_ref[...])
    swapped = jnp.concatenate([_swap_halves_lanes(pa[:, 0:2 * LANES], HEAD_DIM),
                               _swap_halves_lanes(pa[:, 2 * LANES:], DIFF_QK_DIM)], axis=1)
    ra = pa * cosa_ref[...] + swapped * sina_ref[...]
    pos = pl.program_id(1) * tm + lax.broadcasted_iota(jnp.int32, (tm, 1), 0)
    lane = lax.broadcasted_iota(jnp.int32, (1, LANES), 1)
    block_id = jnp.where(lane - HEAD_DIM == pos // SLC_BLOCK, 1.0, 0.0)
    kslc_ref[0] = (ra[:, 0:LANES] + block_id).astype(BF16)
    kwin_ref[0] = ra[:, LANES:2 * LANES].astype(BF16)
    dk_ref[0] = ra[:, 2 * LANES:].astype(BF16)
    pb = _dot(xb, wb_ref[...]).astype(BF16)
    fk_ref[0] = pb[:, 0:FOX_WIDTH]
    kcmp_ref[0] = pb[:, FOX_WIDTH:FOX_WIDTH + HEAD_DIM]
    vcmp_ref[0] = pb[:, FOX_WIDTH + HEAD_DIM:]
    z = _dot(xb, wc_ref[...]) + bc_ref[...]
    fls_ref[0] = jnp.minimum(z, 0.0) - jnp.log(1.0 + jnp.exp(-jnp.abs(z)))

    pd = _dot_nt(wd_ref[...], xb)
    swapped = jnp.concatenate([_swap_halves_rows(pd[0:NSA_WIDTH], HEAD_DIM),
                               _swap_halves_rows(pd[NSA_WIDTH:], DIFF_QK_DIM)], axis=0)
    rd = pd * cosd_ref[...] + swapped * sind_ref[...]
    nq_ref[0] = rd[0:NSA_WIDTH].astype(BF16)
    dq_ref[0] = rd[NSA_WIDTH:].astype(BF16)
    pe = _dot_nt(we_ref[...], xb).astype(BF16)
    vslc_ref[0] = pe[0:HEAD_DIM]
    vwin_ref[0] = pe[HEAD_DIM:2 * HEAD_DIM]
    off = 2 * HEAD_DIM
    dv_ref[0] = pe[off:off + DIFF_WIDTH]
    off += DIFF_WIDTH
    fq_ref[0] = pe[off:off + FOX_WIDTH]
    off += FOX_WIDTH
    fv_ref[0] = pe[off:off + FOX_WIDTH]
    gate_ref[0] = _sigmoid(_dot_nt(wf_ref[...], xb))


def _in_proj(x, wa, wb, wc, wd, we, wf, cosa, sina, cosd, sind, bc):
    b, s, _ = x.shape
    tm = ROW_TILE
    grid = (b, s // tm)
    w2 = lambda i, j: (0, 0)
    tok = lambda i, j: (i, j, 0)
    fm = lambda i, j: (i, 0, j)
    in_specs = [
        pl.BlockSpec((1, tm, D_MODEL), tok),
        pl.BlockSpec((D_MODEL, TM_ROPE_COLS), w2),
        pl.BlockSpec((D_MODEL, TM_PLAIN_COLS), w2),
        pl.BlockSpec((D_MODEL, LANES), w2),
        pl.BlockSpec((FM_ROPE_ROWS, D_MODEL), w2),
        pl.BlockSpec((FM_PLAIN_ROWS, D_MODEL), w2),
        pl.BlockSpec((GATE_ROWS, D_MODEL), w2),
        pl.BlockSpec((tm, TM_ROPE_COLS), lambda i, j: (j, 0)),
        pl.BlockSpec((tm, TM_ROPE_COLS), lambda i, j: (j, 0)),
        pl.BlockSpec((FM_ROPE_ROWS, tm), lambda i, j: (0, j)),
        pl.BlockSpec((FM_ROPE_ROWS, tm), lambda i, j: (0, j)),
        pl.BlockSpec((1, LANES), w2),
    ]

    def tok_out(width, dtype=BF16):
        return pl.BlockSpec((1, tm, width), tok), jax.ShapeDtypeStruct((b, s, width), dtype)

    def fm_out(rows, dtype=BF16):
        return pl.BlockSpec((1, rows, tm), fm), jax.ShapeDtypeStruct((b, rows, s), dtype)

    outs = [
        tok_out(LANES), tok_out(LANES), tok_out(DIFF_WIDTH), tok_out(FOX_WIDTH),
        tok_out(HEAD_DIM), tok_out(HEAD_DIM), tok_out(LANES, F32),
        fm_out(NSA_WIDTH), fm_out(DIFF_WIDTH),
        fm_out(HEAD_DIM), fm_out(HEAD_DIM), fm_out(DIFF_WIDTH),
        fm_out(FOX_WIDTH), fm_out(FOX_WIDTH), fm_out(GATE_ROWS, F32),
    ]
    return pl.pallas_call(
        functools.partial(_in_proj_kernel, tm=tm),
        grid=grid,
        in_specs=in_specs,
        out_specs=[o[0] for o in outs],
        out_shape=[o[1] for o in outs],
        compiler_params=_cparams("parallel", "parallel"),
        name="in_proj",
    )(x, wa, wb, wc, wd, we, wf, cosa, sina, cosd, sind, bc)


def _gelu_tanh(x):
    return 0.5 * x * (1.0 + jnp.tanh(math.sqrt(2.0 / math.pi) * (x + 0.044715 * (x * x * x))))


def _compress_kernel(k16_ref, v16_ref, pek_ref, pev_ref, wk1_ref, wk2_ref, wk2r_ref, wv1_ref, wv2t_ref,
                     cos_ref, sin_ref, kc_ref, vct_ref):
    def hidden(x16, pe_ref, w1_ref):
        top = _dot(x16, w1_ref[0])
        bot = _dot(x16, w1_ref[1])
        bias = _dot(pe_ref[0], w1_ref[0]) + _dot(pe_ref[1], w1_ref[1])
        bot = pltpu.roll(bot, N_CMP - 1, 0)
        return _gelu_tanh(top + bot + bias[0:1, :]).astype(BF16)

    hk = hidden(k16_ref[0], pek_ref, wk1_ref)
    kc = _dot(hk, wk2_ref[...]) * cos_ref[...] + _dot(hk, wk2r_ref[...]) * sin_ref[...]
    kc_ref[0] = kc.astype(BF16)
    hv = hidden(v16_ref[0], pev_ref, wv1_ref)
    vct_ref[0] = _dot_nt(wv2t_ref[...], hv).astype(BF16)


def _compress(k16, v16, pek, pev, wk1, wk2, wk2r, wv1, wv2t, cos_c, sin_c):
    b = k16.shape[0]
    half = CMP_STRIDE * HEAD_DIM
    bat = lambda i: (i, 0, 0)
    c2 = lambda i: (0, 0)
    c3 = lambda i: (0, 0, 0)
    in_specs = [
        pl.BlockSpec((1, N_CMP, half), bat),
        pl.BlockSpec((1, N_CMP, half), bat),
        pl.BlockSpec((2, 8, half), c3),
        pl.BlockSpec((2, 8, half), c3),
        pl.BlockSpec((2, half, CMP_HIDDEN), c3),
        pl.BlockSpec((CMP_HIDDEN, HEAD_DIM), c2),
        pl.BlockSpec((CMP_HIDDEN, HEAD_DIM), c2),
        pl.BlockSpec((2, half, CMP_HIDDEN), c3),
        pl.BlockSpec((HEAD_DIM, CMP_HIDDEN), c2),
        pl.BlockSpec((N_CMP, HEAD_DIM), c2),
        pl.BlockSpec((N_CMP, HEAD_DIM), c2),
    ]
    return pl.pallas_call(
        _compress_kernel,
        grid=(b,),
        in_specs=in_specs,
        out_specs=[pl.BlockSpec((1, N_CMP, HEAD_DIM), bat), pl.BlockSpec((1, HEAD_DIM, N_CMP), bat)],
        out_shape=[jax.ShapeDtypeStruct((b, N_CMP, HEAD_DIM), BF16), jax.ShapeDtypeStruct((b, HEAD_DIM, N_CMP), BF16)],
        compiler_params=_cparams("parallel"),
        name="nsa_compress",
    )(k16, v16, pek, pev, wk1, wk2, wk2r, wv1, wv2t, cos_c, sin_c)


def _decay_kernel(fls_ref, tri_ref, ccol_ref, crow_ref, *, n_tiles):
    carry = jnp.zeros((1, LANES), F32)
    for i in range(n_tiles):
        blk = fls_ref[0, i * CUM_TILE:(i + 1) * CUM_TILE, :]
        c = jnp.dot(tri_ref[...], blk, preferred_element_type=F32, precision=lax.Precision.HIGHEST) + carry
        c2 = c * LOG2E
        ccol_ref[0, i * CUM_TILE:(i + 1) * CUM_TILE, :] = c2
        crow_ref[0, :, i * CUM_TILE:(i + 1) * CUM_TILE] = c2.T[0:FOX_HEADS, :]
        carry = c[CUM_TILE - 1:CUM_TILE, :]


def _decay(fls, tri):
    b, s, _ = fls.shape
    return pl.pallas_call(
        functools.partial(_decay_kernel, n_tiles=s // CUM_TILE),
        grid=(b,),
        in_specs=[pl.BlockSpec((1, s, LANES), lambda i: (i, 0, 0)),
                  pl.BlockSpec((CUM_TILE, CUM_TILE), lambda i: (0, 0))],
        out_specs=[pl.BlockSpec((1, s, LANES), lambda i: (i, 0, 0)),
                   pl.BlockSpec((1, FOX_HEADS, s), lambda i: (i, 0, 0))],
        out_shape=[jax.ShapeDtypeStruct((b, s, LANES), F32), jax.ShapeDtypeStruct((b, FOX_HEADS, s), F32)],
        compiler_params=_cparams("parallel"),
        name="fox_decay",
    )(fls, tri)


def _key_minus_query(tk, tq, k0, q0):
    d = lax.broadcasted_iota(jnp.int32, (tk, tq), 0) - lax.broadcasted_iota(jnp.int32, (tk, tq), 1)
    return d, q0 - k0


def _causal_mask(tk, tq, q0):
    def mask(i, k0, diagonal):
        if not diagonal:
            return None
        d, off = _key_minus_query(tk, tq, k0, q0)
        return d <= off
    return mask


def _head_rows(h, width):
    return slice(h * width, (h + 1) * width)


def _attend_t(n, tq, tk, qi, lo, scores, values, acc_ref, sbuf_ref, *, post=None, shift=None, mask=None,
              n_early=None, join=None):
    n_early = n if n_early is None else n_early
    ahead = min(QK_AHEAD, n_early)
    per_tile = tq // tk
    acc_ref[...] = jnp.zeros(acc_ref.shape, F32)
    k_lo = pl.multiple_of(lo * tk, tk)
    for i in range(ahead):
        sbuf_ref[i] = scores(i, k_lo)
    ones = jnp.ones((BF16_SUBLANES, tk), BF16)

    def step(c, heads, diagonal, last, ms, ls):
        k0 = pl.multiple_of(c * tk, tk)
        k1 = pl.multiple_of(c * tk + tk, tk)
        ms, ls = list(ms), list(ls)
        pend = [sbuf_ref[i] for i in range(ahead)]
        for i in range(heads):
            s = pend.pop(0)
            j = i + ahead
            if j < heads:
                pend.append(scores(j, k0))
            elif not last:
                sbuf_ref[j - heads] = scores(j - heads, k1)
            if post is not None:
                s = post(i, s, k0)
            mk = None if mask is None else mask(i, k0, diagonal)
            if mk is not None:
                s = jnp.where(mk, s, NEG_BIG)
            s_max = jnp.max(s, axis=0, keepdims=True)
            sh = None if shift is None else shift(i)
            if sh is not None:
                s_max = s_max + sh
            m_new = jnp.maximum(ms[i], s_max)
            a = jnp.exp2(ms[i] - m_new)
            p = jnp.exp2(s + ((sh - m_new) if sh is not None else -m_new))
            pv = _dot(jnp.concatenate([values(i, k0), ones], axis=0), p.astype(BF16))
            rows = _head_rows(i, HEAD_DIM)
            acc_ref[rows, :] = a * acc_ref[rows, :] + pv[0:HEAD_DIM]
            ls[i] = a * ls[i] + pv[HEAD_DIM:HEAD_DIM + 1]
            ms[i] = m_new
        return tuple(ms), tuple(ls)

    def init(count):
        return (tuple(jnp.full((1, tq), NEG_BIG, F32) for _ in range(count)),
                tuple(jnp.zeros((1, tq), F32) for _ in range(count)))

    first_diag = qi * per_tile
    ms, ls = init(n_early)
    if n_early < n:
        ms, ls = lax.fori_loop(lo, join, lambda c, carry: step(c, n_early, False, False, *carry), (ms, ls))
        late = init(n - n_early)
        ms, ls, lo = ms + late[0], ls + late[1], join
    ms, ls = lax.fori_loop(lo, first_diag, lambda c, carry: step(c, n, False, False, *carry), (ms, ls))
    for d in range(per_tile):
        ms, ls = step(first_diag + d, n, True, d == per_tile - 1, ms, ls)
    return ls


def _pad_queries(q_t, qpad_ref, n, width):
    per = MXU_COLS // width
    row = lax.broadcasted_iota(jnp.int32, (MXU_COLS, q_t.shape[1]), 0)
    for i in range(n):
        g, r = divmod(i, per)
        qg = q_t[g * MXU_COLS:(g + 1) * MXU_COLS]
        qpad_ref[i] = jnp.where((row >= r * width) & (row < (r + 1) * width), qg, jnp.zeros_like(qg))


def _fox_kernel(qt_ref, k_ref, vt_ref, ccol_ref, crow_ref, o_ref, qpad_ref, acc_ref, sbuf_ref, *, tq, tk):
    qi = pl.program_id(1)
    q0 = pl.multiple_of(qi * tq, tq)
    _pad_queries(qt_ref[0], qpad_ref, FOX_HEADS, HEAD_DIM)
    per = MXU_COLS // HEAD_DIM

    def scores(h, k0):
        g = h // per
        return _dot(k_ref[0, pl.ds(k0, tk), g * MXU_COLS:(g + 1) * MXU_COLS], qpad_ref[h])

    ls = _attend_t(
        FOX_HEADS, tq, tk, qi, 0, scores,
        lambda h, k0: vt_ref[0, _head_rows(h, HEAD_DIM), pl.ds(k0, tk)],
        acc_ref, sbuf_ref,
        post=lambda h, s, k0: s - ccol_ref[0, pl.ds(k0, tk), h:h + 1],
        shift=lambda h: crow_ref[0, h:h + 1, pl.ds(q0, tq)],
        mask=_causal_mask(tk, tq, q0))
    for h in range(FOX_HEADS):
        rows = _head_rows(h, HEAD_DIM)
        acc_ref[rows, :] = acc_ref[rows, :] * (1.0 / ls[h])
    o_ref[0] = acc_ref[...].T.astype(BF16)


def _fox_attention(fq_t, fk, fv_t, ccol, crow):
    b, w, s = fq_t.shape
    tq, tk = ATT_Q_TILE, ATT_K_TILE
    return pl.pallas_call(
        functools.partial(_fox_kernel, tq=tq, tk=tk),
        grid=(b, s // tq),
        in_specs=[
            pl.BlockSpec((1, w, tq), lambda i, j: (i, 0, j)),
            pl.BlockSpec((1, s, w), lambda i, j: (i, 0, 0)),
            pl.BlockSpec((1, w, s), lambda i, j: (i, 0, 0)),
            pl.BlockSpec((1, s, LANES), lambda i, j: (i, 0, 0)),
            pl.BlockSpec((1, FOX_HEADS, s), lambda i, j: (i, 0, 0)),
        ],
        out_specs=pl.BlockSpec((1, tq, w), lambda i, j: (i, j, 0)),
        out_shape=jax.ShapeDtypeStruct((b, s, w), BF16),
        scratch_shapes=[pltpu.VMEM((FOX_HEADS, MXU_COLS, tq), BF16), pltpu.VMEM((w, tq), F32),
                        pltpu.VMEM((QK_AHEAD, tk, tq), F32)],
        compiler_params=_cparams("parallel", "arbitrary"),
        name="fox_attention",
    )(fq_t, fk, fv_t, ccol, crow)


def _diff_kernel(qt_ref, k_ref, vt_ref, lam_ref, g_ref, o_ref, qpad_ref, acc_ref, sbuf_ref, *, tq, tk, lambda_init):
    qi = pl.program_id(1)
    _pad_queries(qt_ref[0], qpad_ref, DIFF_MAPS, DIFF_QK_DIM)
    ls = _attend_t(
        DIFF_MAPS, tq, tk, qi, 0,
        lambda i, k0: _dot(k_ref[0, pl.ds(k0, tk), :], qpad_ref[i]),
        lambda i, k0: vt_ref[0, _head_rows(i // 2, HEAD_DIM), pl.ds(k0, tk)],
        acc_ref, sbuf_ref, mask=_causal_mask(tk, tq, qi * tq))
    lp = lam_ref[...]
    lam = (jnp.exp(jnp.sum(lp[0:1] * lp[1:2], axis=1, keepdims=True))
           - jnp.exp(jnp.sum(lp[2:3] * lp[3:4], axis=1, keepdims=True)) + lambda_init)
    heads = []
    for h in range(DIFF_HEADS):
        o1 = acc_ref[_head_rows(2 * h, HEAD_DIM), :] * (1.0 / ls[2 * h])
        o2 = acc_ref[_head_rows(2 * h + 1, HEAD_DIM), :] * (1.0 / ls[2 * h + 1])
        o = o1 - lam * o2
        o = o * lax.rsqrt(jnp.mean(o * o, axis=0, keepdims=True) + LN_EPS)
        heads.append(o * (1.0 - lambda_init))
    o_t = jnp.concatenate(heads, axis=0)
    o_ref[0] = (o_t.T * g_ref[...]).astype(BF16)


def _diff_attention(dq_t, dk, dv_t, lam_params, subln_g, lambda_init):
    b, w, s = dq_t.shape
    tq, tk = ATT_Q_TILE, ATT_K_TILE
    return pl.pallas_call(
        functools.partial(_diff_kernel, tq=tq, tk=tk, lambda_init=lambda_init),
        grid=(b, s // tq),
        in_specs=[
            pl.BlockSpec((1, w, tq), lambda i, j: (i, 0, j)),
            pl.BlockSpec((1, s, w), lambda i, j: (i, 0, 0)),
            pl.BlockSpec((1, w, s), lambda i, j: (i, 0, 0)),
            pl.BlockSpec((4, DIFF_QK_DIM), lambda i, j: (0, 0)),
            pl.BlockSpec((1, w), lambda i, j: (0, 0)),
        ],
        out_specs=pl.BlockSpec((1, tq, w), lambda i, j: (i, j, 0)),
        out_shape=jax.ShapeDtypeStruct((b, s, w), BF16),
        scratch_shapes=[pltpu.VMEM((DIFF_MAPS, MXU_COLS, tq), BF16), pltpu.VMEM((DIFF_MAPS * HEAD_DIM, tq), F32),
                        pltpu.VMEM((QK_AHEAD, tk, tq), F32)],
        compiler_params=_cparams("parallel", "arbitrary"),
        name="diff_attention",
    )(dq_t, dk, dv_t, lam_params, subln_g)


def _nsa_cmp_scores(qt_ref, kc_ref, t0, width):
    cols = slice(t0, t0 + width)
    return [_dot(kc_ref[0], qt_ref[0, _head_rows(h, HEAD_DIM), cols]) for h in range(NSA_HEADS)]


def _nsa_select(scores, vct_ref, g_ref, ovl_ref, sel_ref, ocmp_ref, t0, width):
    cols = slice(t0, t0 + width)
    t_q = t0 + lax.broadcasted_iota(jnp.int32, (1, width), 1)
    block_end = CMP_STRIDE * lax.broadcasted_iota(jnp.int32, (N_CMP, 1), 0) + (CMP_BLOCK - 1)
    vis = block_end <= t_q
    any_vis = t_q >= CMP_BLOCK - 1
    lhs = jnp.concatenate([vct_ref[0], ovl_ref[...], jnp.ones((BF16_SUBLANES, N_CMP), BF16)], axis=0)
    imp = jnp.zeros((N_SLC, width), F32)
    for h in range(NSA_HEADS):
        rows = _head_rows(h, HEAD_DIM)
        s = jnp.where(vis, scores[h], NEG_BIG)
        p = jnp.exp2(s - jnp.max(s, axis=0, keepdims=True))
        r = _dot(lhs, p.astype(BF16))
        l = r[HEAD_DIM + N_SLC:HEAD_DIM + N_SLC + 1]
        scale = jnp.where(any_vis, 1.0 / l, 0.0)
        ocmp_ref[rows, cols] = (g_ref[0, 3 * h:3 * h + 1, cols] * scale) * r[0:HEAD_DIM]
        imp = imp + scale * r[HEAD_DIM:HEAD_DIM + N_SLC]
    j_idx = lax.broadcasted_iota(jnp.int32, (N_SLC, 1), 0)
    blk_t = t_q // SLC_BLOCK
    forced = (j_idx == 0) | (j_idx == blk_t) | (j_idx == blk_t - 1)
    valid = j_idx * SLC_BLOCK <= t_q
    score = jnp.where(forced, 1e9, jnp.where(valid, imp, -1.0))
    groups = [score[g * 8:(g + 1) * 8] for g in range(N_SLC // 8)]
    ranks = [jnp.zeros((8, width), F32) for _ in groups]
    row = lax.broadcasted_iota(jnp.int32, (8, 1), 0)
    for i in range(N_SLC):
        gi, ri = divmod(i, 8)
        si = groups[gi][ri:ri + 1, :]
        for g, sg in enumerate(groups):
            if g < gi:
                ranks[g] = ranks[g] + jnp.where(si > sg, 1.0, 0.0)
            elif g > gi:
                ranks[g] = ranks[g] + jnp.where(si >= sg, 1.0, 0.0)
            else:
                tie = jnp.where(row > ri, 1.0, 0.0)
                ranks[g] = ranks[g] + jnp.where(si > sg, 1.0, 0.0) + jnp.where(si == sg, tie, 0.0)
    rank = jnp.concatenate(ranks, axis=0)
    sel_ref[:, cols] = jnp.where(rank < float(SLC_TOPK), 0.0, NEG_BIG).astype(BF16)


def _nsa_kernel(qt_ref, kc_ref, vct_ref, ks_ref, vst_ref, kw_ref, vwt_ref, g_ref, ovl_ref,
                o_ref, sel_ref, ocmp_ref, qaug_ref, acc_ref, out_ref, sbuf_ref, *, tq, tk, seq):
    qi = pl.program_id(1)
    q0 = pl.multiple_of(qi * tq, tq)

    @pl.when(qi == 0)
    def _():
        nxt = _nsa_cmp_scores(qt_ref, kc_ref, 0, NSA_PRE_TILE)
        for t0 in range(0, seq, NSA_PRE_TILE):
            scores = nxt
            if t0 + NSA_PRE_TILE < seq:
                nxt = _nsa_cmp_scores(qt_ref, kc_ref, t0 + NSA_PRE_TILE, NSA_PRE_TILE)
            _nsa_select(scores, vct_ref, g_ref, ovl_ref, sel_ref, ocmp_ref, t0, NSA_PRE_TILE)

    pad = jnp.zeros((LANES - HEAD_DIM - N_SLC, tq), BF16)
    sel_neg = sel_ref[:, pl.ds(q0, tq)]
    for h in range(NSA_HEADS):
        qaug_ref[h] = jnp.concatenate([qt_ref[0, _head_rows(h, HEAD_DIM), pl.ds(q0, tq)], sel_neg, pad], axis=0)

    def scores(i, k0):
        k_ref = ks_ref if i < NSA_HEADS else kw_ref
        return _dot(k_ref[0, pl.ds(k0, tk), :], qaug_ref[i % NSA_HEADS])

    def values(i, k0):
        vt_ref = vst_ref if i < NSA_HEADS else vwt_ref
        return vt_ref[0, :, pl.ds(k0, tk)]

    def mask(i, k0, diagonal):
        d, off = _key_minus_query(tk, tq, k0, q0)
        if diagonal:
            return d <= off
        return None if i < NSA_HEADS else d > off - WINDOW

    ls = _attend_t(2 * NSA_HEADS, tq, tk, qi, 0, scores, values, acc_ref, sbuf_ref, mask=mask,
                   n_early=NSA_HEADS, join=jnp.maximum(q0 - WINDOW, 0) // tk)
    for h in range(NSA_HEADS):
        rows = _head_rows(h, HEAD_DIM)
        out = ocmp_ref[rows, pl.ds(q0, tq)]
        for branch in (1, 2):
            i = (branch - 1) * NSA_HEADS + h
            gate = g_ref[0, 3 * h + branch:3 * h + branch + 1, pl.ds(q0, tq)]
            out = out + (gate * (1.0 / ls[i])) * acc_ref[_head_rows(i, HEAD_DIM), :]
        out_ref[rows, :] = out
    o_ref[0] = out_ref[...].T.astype(BF16)


def _nsa_attention(nq_t, kc, vc_t, kslc, vslc_t, kwin, vwin_t, gates_t, ovl):
    b, w, s = nq_t.shape
    tq, tk = NSA_Q_TILE, ATT_K_TILE
    seq = lambda i, j: (i, 0, 0)
    return pl.pallas_call(
        functools.partial(_nsa_kernel, tq=tq, tk=tk, seq=s),
        grid=(b, s // tq),
        in_specs=[
            pl.BlockSpec((1, w, s), seq),
            pl.BlockSpec((1, N_CMP, HEAD_DIM), seq),
            pl.BlockSpec((1, HEAD_DIM, N_CMP), seq),
            pl.BlockSpec((1, s, LANES), seq),
            pl.BlockSpec((1, HEAD_DIM, s), seq),
            pl.BlockSpec((1, s, LANES), seq),
            pl.BlockSpec((1, HEAD_DIM, s), seq),
            pl.BlockSpec((1, GATE_ROWS, s), seq),
            pl.BlockSpec((N_SLC, N_CMP), lambda i, j: (0, 0)),
        ],
        out_specs=pl.BlockSpec((1, tq, w), lambda i, j: (i, j, 0)),
        out_shape=jax.ShapeDtypeStruct((b, s, w), BF16),
        scratch_shapes=[pltpu.VMEM((N_SLC, s), BF16), pltpu.VMEM((w, s), F32),
                        pltpu.VMEM((NSA_HEADS, LANES, tq), BF16), pltpu.VMEM((2 * w, tq), F32),
                        pltpu.VMEM((w, tq), F32), pltpu.VMEM((QK_AHEAD, tk, tq), F32)],
        compiler_params=_cparams("parallel", "arbitrary"),
        name="nsa_attention",
    )(nq_t, kc, vc_t, kslc, vslc_t, kwin, vwin_t, gates_t, ovl)


def _out_ln_kernel(x_ref, on_ref, od_ref, of_ref, wn_ref, wd_ref, wf_ref, lng_ref, lnb_ref, o_ref):
    h = _dot(on_ref[...], wn_ref[...]) + _dot(od_ref[...], wd_ref[...]) + _dot(of_ref[...], wf_ref[...])
    o_ref[...] = _layer_norm(DEEPNORM_ALPHA * x_ref[...] + h, lng_ref[...], lnb_ref[...])


def _out_ln(x, o_nsa, o_diff, o_fox, wn, wd, wf, ln_g, ln_b):
    t = x.shape[0]
    tm = ROW_TILE
    row = lambda i: (i, 0)
    c2 = lambda i: (0, 0)
    return pl.pallas_call(
        _out_ln_kernel,
        grid=(t // tm,),
        in_specs=[
            pl.BlockSpec((tm, D_MODEL), row),
            pl.BlockSpec((tm, NSA_WIDTH), row),
            pl.BlockSpec((tm, DIFF_WIDTH), row),
            pl.BlockSpec((tm, FOX_WIDTH), row),
            pl.BlockSpec((NSA_WIDTH, D_MODEL), c2),
            pl.BlockSpec((DIFF_WIDTH, D_MODEL), c2),
            pl.BlockSpec((FOX_WIDTH, D_MODEL), c2),
            pl.BlockSpec((1, D_MODEL), c2),
            pl.BlockSpec((1, D_MODEL), c2),
        ],
        out_specs=pl.BlockSpec((tm, D_MODEL), row),
        out_shape=jax.ShapeDtypeStruct((t, D_MODEL), F32),
        compiler_params=_cparams("parallel"),
        name="out_ln",
    )(x, o_nsa, o_diff, o_fox, wn, wd, wf, ln_g, ln_b)


def _prep_ffn(wg, wu, wd):
    pad = D_FF_PAD - D_FF
    up = lambda w: jnp.pad(w, ((0, 0), (0, pad))).astype(BF16)
    return up(wg), up(wu), jnp.pad(wd, ((0, pad), (0, 0))).astype(BF16)


def _rot_cols(w, d):
    k, n = w.shape
    w = w.reshape(k, n // d, d)
    return jnp.concatenate([-w[..., d // 2:], w[..., :d // 2]], axis=-1).reshape(k, n)


def _rope_table(pos, d, signed=False):
    inv = ROPE_THETA ** (-jnp.arange(0, d, 2, dtype=F32) / d)
    ang = pos.astype(F32)[:, None] * inv[None, :]
    cos = jnp.concatenate([jnp.cos(ang), jnp.cos(ang)], axis=-1)
    sin = jnp.concatenate([-jnp.sin(ang) if signed else jnp.sin(ang), jnp.sin(ang)], axis=-1)
    return cos, sin


def _prep_in_proj(w_in, fox_b_f, s):
    o = 0
    nsa_q = w_in[:, o:o + NSA_WIDTH]; o += NSA_WIDTH
    kv = [w_in[:, o + i * HEAD_DIM:o + (i + 1) * HEAD_DIM] for i in range(6)]; o += 6 * HEAD_DIM
    k_cmp, v_cmp, k_slc, v_slc, k_win, v_win = kv
    nsa_g = w_in[:, o:o + N_NSA_GATES]; o += N_NSA_GATES
    diff_q = w_in[:, o:o + DIFF_WIDTH]; o += DIFF_WIDTH
    diff_k = w_in[:, o:o + DIFF_WIDTH]; o += DIFF_WIDTH
    diff_v = w_in[:, o:o + DIFF_WIDTH]; o += DIFF_WIDTH
    fox_q = w_in[:, o:o + FOX_WIDTH]; o += FOX_WIDTH
    fox_k = w_in[:, o:o + FOX_WIDTH]; o += FOX_WIDTH
    fox_v = w_in[:, o:o + FOX_WIDTH]; o += FOX_WIDTH
    fox_f = w_in[:, o:o + FOX_HEADS]
    zero = jnp.zeros((D_MODEL, LANES - HEAD_DIM), F32)

    wa = jnp.concatenate([k_slc, zero, k_win, zero, diff_k], axis=1)
    wb = jnp.concatenate([fox_k, k_cmp, v_cmp], axis=1)
    wc = jnp.concatenate([fox_f, jnp.zeros((D_MODEL, LANES - FOX_HEADS), F32)], axis=1)
    bc = jnp.concatenate([fox_b_f, jnp.zeros((LANES - FOX_HEADS,), F32)])[None, :]
    wd = jnp.concatenate([nsa_q, diff_q], axis=1).T
    we = jnp.concatenate([v_slc, v_win, diff_v, fox_q * (HEAD_DIM ** -0.5 * LOG2E), fox_v], axis=1).T
    wf = jnp.concatenate([nsa_g, jnp.zeros((D_MODEL, GATE_ROWS - N_NSA_GATES), F32)], axis=1).T

    pos = jnp.arange(s, dtype=jnp.int32)
    c64, s64 = _rope_table(pos, HEAD_DIM, signed=True)
    c32, s32 = _rope_table(pos, DIFF_QK_DIM, signed=True)
    nsa_scale = HEAD_DIM ** -0.5 * LOG2E
    diff_scale = DIFF_QK_DIM ** -0.5 * LOG2E

    def tm_table(t64, t32):
        return jnp.concatenate([t64, t64, t64, t64, jnp.tile(t32, (1, DIFF_MAPS))], axis=1)

    def fm_table(t64, t32):
        return jnp.concatenate([jnp.tile(t64, (1, NSA_HEADS)) * nsa_scale,
                                jnp.tile(t32, (1, DIFF_MAPS)) * diff_scale], axis=1).T

    bf = lambda w: w.astype(BF16)
    return (bf(wa), bf(wb), bf(wc), bf(wd), bf(we), bf(wf),
            tm_table(c64, c32), tm_table(s64, s32), fm_table(c64, c32), fm_table(s64, s32), bc)


def _prep_compress(pos_k, pos_v, phi_k1, phi_k2, phi_v1, phi_v2):
    half = CMP_STRIDE * HEAD_DIM

    def pe(p):
        return jnp.broadcast_to(p.reshape(2, 1, half), (2, 8, half)).astype(BF16)

    block_end = jnp.arange(N_CMP, dtype=jnp.int32) * CMP_STRIDE + (CMP_BLOCK - 1)
    cos_c, sin_c = _rope_table(block_end, HEAD_DIM)
    return (pe(pos_k), pe(pos_v), phi_k1.reshape(2, half, CMP_HIDDEN).astype(BF16), phi_k2.astype(BF16),
            _rot_cols(phi_k2, HEAD_DIM).astype(BF16), phi_v1.reshape(2, half, CMP_HIDDEN).astype(BF16),
            phi_v2.T.astype(BF16), cos_c, sin_c)


def _overlap_matrix(s):
    c0 = np.arange(N_CMP) * CMP_STRIDE
    s0 = np.arange(N_SLC) * SLC_BLOCK
    ovl = (c0[None, :] < s0[:, None] + SLC_BLOCK) & (c0[None, :] + CMP_BLOCK > s0[:, None])
    ovl[:, (s - CMP_BLOCK) // CMP_STRIDE + 1:] = False
    return jnp.asarray(ovl.astype(np.float32), dtype=BF16)


def kernel(x, p, ln_g, ln_b, ffn1_w_gate, ffn1_w_up, ffn1_w_down, ffn2_w_gate, ffn2_w_up, ffn2_w_down, w_in, fox_b_f, nsa_pos_k, nsa_pos_v, nsa_phi_k1, nsa_phi_k2, nsa_phi_v1, nsa_phi_v2, diff_lambda, diff_subln_g, w_out, ple_w_gate, ple_b_gate, ple_w_proj):
    b, s, _ = x.shape
    assert s // SLC_BLOCK == N_SLC and (s - CMP_BLOCK) // CMP_STRIDE + 1 <= N_CMP
    t = b * s
    ovl = _overlap_matrix(s)
    tri = jnp.asarray(np.tril(np.ones((CUM_TILE, CUM_TILE), np.float32)))
    x = x.reshape(t, D_MODEL)
    for i in range(DEPTH):
        lambda_init = 0.8 - 0.6 * math.exp(-0.3 * i)
        lng = ln_g[i][:, None, :]
        lnb = ln_b[i][:, None, :]
        x = _ffn_ln(x, *_prep_ffn(ffn1_w_gate[i], ffn1_w_up[i], ffn1_w_down[i]), lng[0], lnb[0])
        proj = _in_proj(x.reshape(b, s, D_MODEL), *_prep_in_proj(w_in[i], fox_b_f[i], s))
        kslc, kwin, dk, fk, kcmp, vcmp, fls, nq_t, dq_t, vslc_t, vwin_t, dv_t, fq_t, fv_t, gates_t = proj
        half = CMP_STRIDE * HEAD_DIM
        kc, vc_t = _compress(kcmp.reshape(b, s // CMP_STRIDE, half), vcmp.reshape(b, s // CMP_STRIDE, half),
                             *_prep_compress(nsa_pos_k[i], nsa_pos_v[i], nsa_phi_k1[i], nsa_phi_k2[i],
                                             nsa_phi_v1[i], nsa_phi_v2[i]))
        ccol, crow = _decay(fls, tri)
        o_nsa = _nsa_attention(nq_t, kc, vc_t, kslc, vslc_t, kwin, vwin_t, gates_t, ovl)
        o_diff = _diff_attention(dq_t, dk, dv_t, diff_lambda[i], jnp.tile(diff_subln_g[i], DIFF_HEADS)[None, :],
                                 lambda_init)
        o_fox = _fox_attention(fq_t, fk, fv_t, ccol, crow)
        wo = w_out[i].astype(BF16)
        x = _out_ln(x, o_nsa.reshape(t, NSA_WIDTH), o_diff.reshape(t, DIFF_WIDTH), o_fox.reshape(t, FOX_WIDTH),
                    wo[:NSA_WIDTH], wo[NSA_WIDTH:NSA_WIDTH + DIFF_WIDTH], wo[NSA_WIDTH + DIFF_WIDTH:],
                    lng[1], lnb[1])
        ple = (p.reshape(DEPTH, t, PLE_DIM), i, ple_w_gate[i].astype(BF16), ple_b_gate[i][None, :],
               ple_w_proj[i].astype(BF16))
        x = _ffn_ln(x, *_prep_ffn(ffn2_w_gate[i], ffn2_w_up[i], ffn2_w_down[i]), lng[2], lnb[2], ple=ple)
    return x.reshape(b, s, D_MODEL)
```

```python
import functools
import math

import numpy as np
import jax
import jax.numpy as jnp
from jax import lax
from jax.experimental import pallas as pl
from jax.experimental.pallas import tpu as pltpu

F32 = jnp.float32
BF16 = jnp.bfloat16

D_MODEL = 1024
HEAD_DIM = 64
NSA_HEADS = 4
DIFF_HEADS = 4
FOX_HEADS = 8
NSA_WIDTH = NSA_HEADS * HEAD_DIM
DIFF_WIDTH = DIFF_HEADS * HEAD_DIM
FOX_WIDTH = FOX_HEADS * HEAD_DIM
CMP_BLOCK = 32
CMP_STRIDE = 16
CMP_HIDDEN = 256
SLC_BLOCK = 64
SLC_TOPK = 16
WINDOW = 512
DIFF_QK_DIM = HEAD_DIM // 2
DIFF_MAPS = 2 * DIFF_HEADS
D_FF = 2752
PLE_DIM = 256
ROPE_THETA = 10000.0
LN_EPS = 1e-5
NEG_BIG = -1e30
DEPTH = 2
DEEPNORM_ALPHA = (2.0 * DEPTH) ** 0.25
LOG2E = math.log2(math.e)

LANES = 128
MXU_COLS = 256
BF16_SUBLANES = 16
VMEM_LIMIT_BYTES = 56 * 1024 * 1024

D_FF_PAD = ((D_FF + MXU_COLS - 1) // MXU_COLS) * MXU_COLS
FF_CHUNK = 2 * MXU_COLS
FF_BOUNDS = tuple((c0, min(c0 + FF_CHUNK, D_FF_PAD)) for c0 in range(0, D_FF_PAD, FF_CHUNK))

ROW_TILE = 512
ATT_Q_TILE = 512
ATT_K_TILE = 256
NSA_Q_TILE = 512
NSA_PRE_TILE = 512
CUM_TILE = 256
QK_AHEAD = 4

TM_ROPE_COLS = 2 * LANES + DIFF_WIDTH
TM_PLAIN_COLS = FOX_WIDTH + 2 * HEAD_DIM
FM_ROPE_ROWS = NSA_WIDTH + DIFF_WIDTH
FM_PLAIN_ROWS = 2 * HEAD_DIM + DIFF_WIDTH + 2 * FOX_WIDTH
N_NSA_GATES = 3 * NSA_HEADS
GATE_ROWS = BF16_SUBLANES
N_CMP = 128
N_SLC = 32


def _cparams(*sem):
    return pltpu.CompilerParams(dimension_semantics=sem, vmem_limit_bytes=VMEM_LIMIT_BYTES)


def _dot(a, b):
    return jnp.dot(a, b, preferred_element_type=F32)


def _dot_nt(a, b):
    return lax.dot_general(a, b, (((1,), (1,)), ((), ())), preferred_element_type=F32)


def _sigmoid(z):
    return 1.0 / (1.0 + jnp.exp(-z))


def _layer_norm(y, g, b):
    mu = jnp.mean(y, axis=-1, keepdims=True)
    yc = y - mu
    var = jnp.mean(yc * yc, axis=-1, keepdims=True)
    return yc * lax.rsqrt(var + LN_EPS) * g + b


def _ffn_kernel(*refs, with_ple):
    if with_ple:
        (x_ref, wg_ref, wu_ref, wd_ref, lng_ref, lnb_ref, p_ref, pwg_ref, pbg_ref, pwp_ref,
         o_ref, xb_ref) = refs
    else:
        x_ref, wg_ref, wu_ref, wd_ref, lng_ref, lnb_ref, o_ref, xb_ref = refs
    xb_ref[...] = x_ref[...].astype(BF16)
    xb = xb_ref[...]

    def gate_up(c):
        c0, c1 = FF_BOUNDS[c]
        return _dot(xb, wg_ref[:, c0:c1]), _dot(xb, wu_ref[:, c0:c1])

    nxt = gate_up(0)
    acc = None
    for c, (c0, c1) in enumerate(FF_BOUNDS):
        g, u = nxt
        if c + 1 < len(FF_BOUNDS):
            nxt = gate_up(c + 1)
        h = (g * _sigmoid(g)) * u
        part = _dot(h.astype(BF16), wd_ref[c0:c1, :])
        acc = part if acc is None else acc + part
    y = DEEPNORM_ALPHA * x_ref[...] + 0.5 * acc
    z = _layer_norm(y, lng_ref[...], lnb_ref[...])
    if with_ple:
        gate = _sigmoid(_dot(z.astype(BF16), pwg_ref[...]) + pbg_ref[...])
        z = z + gate * _dot(p_ref[0].astype(BF16), pwp_ref[...])
    o_ref[...] = z


def _ffn_ln(x, wg, wu, wd, ln_g, ln_b, ple=None):
    t = x.shape[0]
    tm = ROW_TILE
    row = lambda i: (i, 0)
    full2 = lambda i: (0, 0)
    in_specs = [
        pl.BlockSpec((tm, D_MODEL), row),
        pl.BlockSpec((D_MODEL, D_FF_PAD), full2),
        pl.BlockSpec((D_MODEL, D_FF_PAD), full2),
        pl.BlockSpec((D_FF_PAD, D_MODEL), full2),
        pl.BlockSpec((1, D_MODEL), full2),
        pl.BlockSpec((1, D_MODEL), full2),
    ]
    args = [x, wg, wu, wd, ln_g, ln_b]
    if ple is not None:
        p, layer, pwg, pbg, pwp = ple
        in_specs += [
            pl.BlockSpec((1, tm, PLE_DIM), lambda i: (layer, i, 0)),
            pl.BlockSpec((D_MODEL, D_MODEL), full2),
            pl.BlockSpec((1, D_MODEL), full2),
            pl.BlockSpec((PLE_DIM, D_MODEL), full2),
        ]
        args += [p, pwg, pbg, pwp]
    return pl.pallas_call(
        functools.partial(_ffn_kernel, with_ple=ple is not None),
        grid=(t // tm,),
        in_specs=in_specs,
        out_specs=pl.BlockSpec((tm, D_MODEL), row),
        out_shape=jax.ShapeDtypeStruct((t, D_MODEL), F32),
        scratch_shapes=[pltpu.VMEM((tm, D_MODEL), BF16)],
        compiler_params=_cparams("parallel"),
        name="ffn_ln_ple" if ple is not None else "ffn_ln",
    )(*args)


def _swap_halves_lanes(x, d):
    w = x.shape[1]
    lane = lax.broadcasted_iota(jnp.int32, (1, w), 1)
    return jnp.where(lane % d < d // 2, pltpu.roll(x, w - d // 2, 1), pltpu.roll(x, d // 2, 1))


def _swap_halves_rows(x, d):
    parts = []
    for r0 in range(0, x.shape[0], d):
        parts += [x[r0 + d // 2:r0 + d], x[r0:r0 + d // 2]]
    return jnp.concatenate(parts, axis=0)


def _in_proj_kernel(x_ref, wa_ref, wb_ref, wc_ref, wd_ref, we_ref, wf_ref,
                    cosa_ref, sina_ref, cosd_ref, sind_ref, bc_ref,
                    kslc_ref, kwin_ref, dk_ref, fk_ref, kcmp_ref, vcmp_ref, fls_ref,
                    nq_ref, dq_ref, vslc_ref, vwin_ref, dv_ref, fq_ref, fv_ref, gate_ref, *, tm):
    xb = x_ref[0].astype(BF16)
    pa = _dot(xb, wa_ref[...])
    swapped = jnp.concatenate([_swap_halves_lanes(pa[:, 0:2 * LANES], HEAD_DIM),
                               _swap_halves_lanes(pa[:, 2 * LANES:], DIFF_QK_DIM)], axis=1)
    ra = pa * cosa_ref[...] + swapped * sina_ref[...]
    pos = pl.program_id(1) * tm + lax.broadcasted_iota(jnp.int32, (tm, 1), 0)
    lane = lax.broadcasted_iota(jnp.int32, (1, LANES), 1)
    block_id = jnp.where(lane - HEAD_DIM == pos // SLC_BLOCK, 1.0, 0.0)
    kslc_ref[0] = (ra[:, 0:LANES] + block_id).astype(BF16)
    kwin_ref[0] = ra[:, LANES:2 * LANES].astype(BF16)
    dk_ref[0] = ra[:, 2 * LANES:].astype(BF16)
    pb = _dot(xb, wb_ref[...]).astype(BF16)
    fk_ref[0] = pb[:, 0:FOX_WIDTH]
    kcmp_ref[0] = pb[:, FOX_WIDTH:FOX_WIDTH + HEAD_DIM]
    vcmp_ref[0] = pb[:, FOX_WIDTH + HEAD_DIM:]
    z = _dot(xb, wc_ref[...]) + bc_ref[...]
    fls_ref[0] = jnp.minimum(z, 0.0) - jnp.log(1.0 + jnp.exp(-jnp.abs(z)))

    pd = _dot_nt(wd_ref[...], xb)
    swapped = jnp.concatenate([_swap_halves_rows(pd[0:NSA_WIDTH], HEAD_DIM),
                               _swap_halves_rows(pd[NSA_WIDTH:], DIFF_QK_DIM)], axis=0)
    rd = pd * cosd_ref[...] + swapped * sind_ref[...]
    nq_ref[0] = rd[0:NSA_WIDTH].astype(BF16)
    dq_ref[0] = rd[NSA_WIDTH:].astype(BF16)
    pe = _dot_nt(we_ref[...], xb).astype(BF16)
    vslc_ref[0] = pe[0:HEAD_DIM]
    vwin_ref[0] = pe[HEAD_DIM:2 * HEAD_DIM]
    off = 2 * HEAD_DIM
    dv_ref[0] = pe[off:off + DIFF_WIDTH]
    off += DIFF_WIDTH
    fq_ref[0] = pe[off:off + FOX_WIDTH]
    off += FOX_WIDTH
    fv_ref[0] = pe[off:off + FOX_WIDTH]
    gate_ref[0] = _sigmoid(_dot_nt(wf_ref[...], xb))


def _in_proj(x, wa, wb, wc, wd, we, wf, cosa, sina, cosd, sind, bc):
    b, s, _ = x.shape
    tm = ROW_TILE
    grid = (b, s // tm)
    w2 = lambda i, j: (0, 0)
    tok = lambda i, j: (i, j, 0)
    fm = lambda i, j: (i, 0, j)
    in_specs = [
        pl.BlockSpec((1, tm, D_MODEL), tok),
        pl.BlockSpec((D_MODEL, TM_ROPE_COLS), w2),
        pl.BlockSpec((D_MODEL, TM_PLAIN_COLS), w2),
        pl.BlockSpec((D_MODEL, LANES), w2),
        pl.BlockSpec((FM_ROPE_ROWS, D_MODEL), w2),
        pl.BlockSpec((FM_PLAIN_ROWS, D_MODEL), w2),
        pl.BlockSpec((GATE_ROWS, D_MODEL), w2),
        pl.BlockSpec((tm, TM_ROPE_COLS), lambda i, j: (j, 0)),
        pl.BlockSpec((tm, TM_ROPE_COLS), lambda i, j: (j, 0)),
        pl.BlockSpec((FM_ROPE_ROWS, tm), lambda i, j: (0, j)),
        pl.BlockSpec((FM_ROPE_ROWS, tm), lambda i, j: (0, j)),
        pl.BlockSpec((1, LANES), w2),
    ]

    def tok_out(width, dtype=BF16):
        return pl.BlockSpec((1, tm, width), tok), jax.ShapeDtypeStruct((b, s, width), dtype)

    def fm_out(rows, dtype=BF16):
        return pl.BlockSpec((1, rows, tm), fm), jax.ShapeDtypeStruct((b, rows, s), dtype)

    outs = [
        tok_out(LANES), tok_out(LANES), tok_out(DIFF_WIDTH), tok_out(FOX_WIDTH),
        tok_out(HEAD_DIM), tok_out(HEAD_DIM), tok_out(LANES, F32),
        fm_out(NSA_WIDTH), fm_out(DIFF_WIDTH),
        fm_out(HEAD_DIM), fm_out(HEAD_DIM), fm_out(DIFF_WIDTH),
        fm_out(FOX_WIDTH), fm_out(FOX_WIDTH), fm_out(GATE_ROWS, F32),
    ]
    return pl.pallas_call(
        functools.partial(_in_proj_kernel, tm=tm),
        grid=grid,
        in_specs=in_specs,
        out_specs=[o[0] for o in outs],
        out_shape=[o[1] for o in outs],
        compiler_params=_cparams("parallel", "parallel"),
        name="in_proj",
    )(x, wa, wb, wc, wd, we, wf, cosa, sina, cosd, sind, bc)


def _gelu_tanh(x):
    return 0.5 * x * (1.0 + jnp.tanh(math.sqrt(2.0 / math.pi) * (x + 0.044715 * (x * x * x))))


def _compress_kernel(k16_ref, v16_ref, pek_ref, pev_ref, wk1_ref, wk2_ref, wk2r_ref, wv1_ref, wv2t_ref,
                     cos_ref, sin_ref, kc_ref, vct_ref):
    def hidden(x16, pe_ref, w1_ref):
        top = _dot(x16, w1_ref[0])
        bot = _dot(x16, w1_ref[1])
        bias = _dot(pe_ref[0], w1_ref[0]) + _dot(pe_ref[1], w1_ref[1])
        bot = pltpu.roll(bot, N_CMP - 1, 0)
        return _gelu_tanh(top + bot + bias[0:1, :]).astype(BF16)

    hk = hidden(k16_ref[0], pek_ref, wk1_ref)
    kc = _dot(hk, wk2_ref[...]) * cos_ref[...] + _dot(hk, wk2r_ref[...]) * sin_ref[...]
    kc_ref[0] = kc.astype(BF16)
    hv = hidden(v16_ref[0], pev_ref, wv1_ref)
    vct_ref[0] = _dot_nt(wv2t_ref[...], hv).astype(BF16)


def _compress(k16, v16, pek, pev, wk1, wk2, wk2r, wv1, wv2t, cos_c, sin_c):
    b = k16.shape[0]
    half = CMP_STRIDE * HEAD_DIM
    bat = lambda i: (i, 0, 0)
    c2 = lambda i: (0, 0)
    c3 = lambda i: (0, 0, 0)
    in_specs = [
        pl.BlockSpec((1, N_CMP, half), bat),
        pl.BlockSpec((1, N_CMP, half), bat),
        pl.BlockSpec((2, 8, half), c3),
        pl.BlockSpec((2, 8, half), c3),
        pl.BlockSpec((2, half, CMP_HIDDEN), c3),
        pl.BlockSpec((CMP_HIDDEN, HEAD_DIM), c2),
        pl.BlockSpec((CMP_HIDDEN, HEAD_DIM), c2),
        pl.BlockSpec((2, half, CMP_HIDDEN), c3),
        pl.BlockSpec((HEAD_DIM, CMP_HIDDEN), c2),
        pl.BlockSpec((N_CMP, HEAD_DIM), c2),
        pl.BlockSpec((N_CMP, HEAD_DIM), c2),
    ]
    return pl.pallas_call(
        _compress_kernel,
        grid=(b,),
        in_specs=in_specs,
        out_specs=[pl.BlockSpec((1, N_CMP, HEAD_DIM), bat), pl.BlockSpec((1, HEAD_DIM, N_CMP), bat)],
        out_shape=[jax.ShapeDtypeStruct((b, N_CMP, HEAD_DIM), BF16), jax.ShapeDtypeStruct((b, HEAD_DIM, N_CMP), BF16)],
        compiler_params=_cparams("parallel"),
        name="nsa_compress",
    )(k16, v16, pek, pev, wk1, wk2, wk2r, wv1, wv2t, cos_c, sin_c)


def _decay_kernel(fls_ref, tri_ref, ccol_ref, crow_ref, *, n_tiles):
    carry = jnp.zeros((1, LANES), F32)
    for i in range(n_tiles):
        blk = fls_ref[0, i * CUM_TILE:(i + 1) * CUM_TILE, :]
        c = jnp.dot(tri_ref[...], blk, preferred_element_type=F32, precision=lax.Precision.HIGHEST) + carry
        c2 = c * LOG2E
        ccol_ref[0, i * CUM_TILE:(i + 1) * CUM_TILE, :] = c2
        crow_ref[0, :, i * CUM_TILE:(i + 1) * CUM_TILE] = c2.T[0:FOX_HEADS, :]
        carry = c[CUM_TILE - 1:CUM_TILE, :]


def _decay(fls, tri):
    b, s, _ = fls.shape
    return pl.pallas_call(
        functools.partial(_decay_kernel, n_tiles=s // CUM_TILE),
        grid=(b,),
        in_specs=[pl.BlockSpec((1, s, LANES), lambda i: (i, 0, 0)),
                  pl.BlockSpec((CUM_TILE, CUM_TILE), lambda i: (0, 0))],
        out_specs=[pl.BlockSpec((1, s, LANES), lambda i: (i, 0, 0)),
                   pl.BlockSpec((1, FOX_HEADS, s), lambda i: (i, 0, 0))],
        out_shape=[jax.ShapeDtypeStruct((b, s, LANES), F32), jax.ShapeDtypeStruct((b, FOX_HEADS, s), F32)],
        compiler_params=_cparams("parallel"),
        name="fox_decay",
    )(fls, tri)


def _key_minus_query(tk, tq, k0, q0):
    d = lax.broadcasted_iota(jnp.int32, (tk, tq), 0) - lax.broadcasted_iota(jnp.int32, (tk, tq), 1)
    return d, q0 - k0


def _causal_mask(tk, q0):
    def mask(i, k0, diagonal, q_lo, width):
        if not diagonal:
            return None
        d, off = _key_minus_query(tk, width, k0, q0 + q_lo)
        return d <= off
    return mask


def _head_rows(h, width):
    return slice(h * width, (h + 1) * width)


def _attend_t(n, tq, tk, qi, lo, scores, values, acc_ref, sbuf_ref, *, post=None, shift=None, mask=None,
              n_early=None, join=None):
    n_early = n if n_early is None else n_early
    ahead = min(QK_AHEAD, n_early)
    per_tile = tq // tk
    full = slice(0, tq)
    acc_ref[...] = jnp.zeros(acc_ref.shape, F32)
    k_lo = pl.multiple_of(lo * tk, tk)
    for i in range(ahead):
        sbuf_ref[i] = scores(i, k_lo, full)
    ones = jnp.ones((BF16_SUBLANES, tk), BF16)

    def step(c, heads, diag, ms, ls):
        diagonal = diag is not None
        q_lo = diag * tk if diagonal else 0
        cols = slice(q_lo, tq)
        nxt = slice(q_lo + tk, tq) if diagonal else full
        k0 = pl.multiple_of(c * tk, tk)
        k1 = pl.multiple_of(c * tk + tk, tk)
        ms, ls = list(ms), list(ls)
        pend = [sbuf_ref[i, :, cols] for i in range(ahead)]
        for i in range(heads):
            s = pend.pop(0)
            j = i + ahead
            if j < heads:
                pend.append(scores(j, k0, cols))
            elif nxt.start < tq:
                sbuf_ref[j - heads, :, nxt] = scores(j - heads, k1, nxt)
            if post is not None:
                s = post(i, s, k0)
            mk = None if mask is None else mask(i, k0, diagonal, q_lo, tq - q_lo)
            if mk is not None:
                s = jnp.where(mk, s, NEG_BIG)
            s_max = jnp.max(s, axis=0, keepdims=True)
            sh = None if shift is None else shift(i, cols)
            if sh is not None:
                s_max = s_max + sh
            m_old, l_old = ms[i][:, cols], ls[i][:, cols]
            m_new = jnp.maximum(m_old, s_max)
            a = jnp.exp2(m_old - m_new)
            p = jnp.exp2(s + ((sh - m_new) if sh is not None else -m_new))
            pv = _dot(jnp.concatenate([values(i, k0), ones], axis=0), p.astype(BF16))
            rows = _head_rows(i, HEAD_DIM)
            acc_ref[rows, cols] = a * acc_ref[rows, cols] + pv[0:HEAD_DIM]
            l_new = a * l_old + pv[HEAD_DIM:HEAD_DIM + 1]
            if q_lo:
                m_new = jnp.concatenate([ms[i][:, 0:q_lo], m_new], axis=1)
                l_new = jnp.concatenate([ls[i][:, 0:q_lo], l_new], axis=1)
            ms[i], ls[i] = m_new, l_new
        return tuple(ms), tuple(ls)

    def init(count):
        return (tuple(jnp.full((1, tq), NEG_BIG, F32) for _ in range(count)),
                tuple(jnp.zeros((1, tq), F32) for _ in range(count)))

    first_diag = qi * per_tile
    ms, ls = init(n_early)
    if n_early < n:
        ms, ls = lax.fori_loop(lo, join, lambda c, carry: step(c, n_early, None, *carry), (ms, ls))
        late = init(n - n_early)
        ms, ls, lo = ms + late[0], ls + late[1], join
    ms, ls = lax.fori_loop(lo, first_diag, lambda c, carry: step(c, n, None, *carry), (ms, ls))
    for d in range(per_tile):
        ms, ls = step(first_diag + d, n, d, ms, ls)
    return ls


def _pad_queries(q_t, qpad_ref, n, width):
    per = MXU_COLS // width
    row = lax.broadcasted_iota(jnp.int32, (MXU_COLS, q_t.shape[1]), 0)
    for i in range(n):
        g, r = divmod(i, per)
        qg = q_t[g * MXU_COLS:(g + 1) * MXU_COLS]
        qpad_ref[i] = jnp.where((row >= r * width) & (row < (r + 1) * width), qg, jnp.zeros_like(qg))


def _fox_kernel(qt_ref, k_ref, vt_ref, ccol_ref, crow_ref, o_ref, qpad_ref, acc_ref, sbuf_ref, *, tq, tk):
    qi = pl.program_id(1)
    q0 = pl.multiple_of(qi * tq, tq)
    _pad_queries(qt_ref[0], qpad_ref, FOX_HEADS, HEAD_DIM)
    per = MXU_COLS // HEAD_DIM

    def scores(h, k0, cols):
        g = h // per
        return _dot(k_ref[0, pl.ds(k0, tk), g * MXU_COLS:(g + 1) * MXU_COLS], qpad_ref[h, :, cols])

    ls = _attend_t(
        FOX_HEADS, tq, tk, qi, 0, scores,
        lambda h, k0: vt_ref[0, _head_rows(h, HEAD_DIM), pl.ds(k0, tk)],
        acc_ref, sbuf_ref,
        post=lambda h, s, k0: s - ccol_ref[0, pl.ds(k0, tk), h:h + 1],
        shift=lambda h, cols: crow_ref[0, h:h + 1, pl.ds(q0 + cols.start, cols.stop - cols.start)],
        mask=_causal_mask(tk, q0))
    for h in range(FOX_HEADS):
        rows = _head_rows(h, HEAD_DIM)
        acc_ref[rows, :] = acc_ref[rows, :] * (1.0 / ls[h])
    o_ref[0] = acc_ref[...].T.astype(BF16)


def _fox_attention(fq_t, fk, fv_t, ccol, crow):
    b, w, s = fq_t.shape
    tq, tk = ATT_Q_TILE, ATT_K_TILE
    return pl.pallas_call(
        functools.partial(_fox_kernel, tq=tq, tk=tk),
        grid=(b, s // tq),
        in_specs=[
            pl.BlockSpec((1, w, tq), lambda i, j: (i, 0, j)),
            pl.BlockSpec((1, s, w), lambda i, j: (i, 0, 0)),
            pl.BlockSpec((1, w, s), lambda i, j: (i, 0, 0)),
            pl.BlockSpec((1, s, LANES), lambda i, j: (i, 0, 0)),
            pl.BlockSpec((1, FOX_HEADS, s), lambda i, j: (i, 0, 0)),
        ],
        out_specs=pl.BlockSpec((1, tq, w), lambda i, j: (i, j, 0)),
        out_shape=jax.ShapeDtypeStruct((b, s, w), BF16),
        scratch_shapes=[pltpu.VMEM((FOX_HEADS, MXU_COLS, tq), BF16), pltpu.VMEM((w, tq), F32),
                        pltpu.VMEM((QK_AHEAD, tk, tq), F32)],
        compiler_params=_cparams("parallel", "arbitrary"),
        name="fox_attention",
    )(fq_t, fk, fv_t, ccol, crow)


def _diff_kernel(qt_ref, k_ref, vt_ref, lam_ref, g_ref, o_ref, qpad_ref, acc_ref, sbuf_ref, *, tq, tk, lambda_init):
    qi = pl.program_id(1)
    _pad_queries(qt_ref[0], qpad_ref, DIFF_MAPS, DIFF_QK_DIM)
    ls = _attend_t(
        DIFF_MAPS, tq, tk, qi, 0,
        lambda i, k0, cols: _dot(k_ref[0, pl.ds(k0, tk), :], qpad_ref[i, :, cols]),
        lambda i, k0: vt_ref[0, _head_rows(i // 2, HEAD_DIM), pl.ds(k0, tk)],
        acc_ref, sbuf_ref, mask=_causal_mask(tk, qi * tq))
    lp = lam_ref[...]
    lam = (jnp.exp(jnp.sum(lp[0:1] * lp[1:2], axis=1, keepdims=True))
           - jnp.exp(jnp.sum(lp[2:3] * lp[3:4], axis=1, keepdims=True)) + lambda_init)
    heads = []
    for h in range(DIFF_HEADS):
        o1 = acc_ref[_head_rows(2 * h, HEAD_DIM), :] * (1.0 / ls[2 * h])
        o2 = acc_ref[_head_rows(2 * h + 1, HEAD_DIM), :] * (1.0 / ls[2 * h + 1])
        o = o1 - lam * o2
        o = o * lax.rsqrt(jnp.mean(o * o, axis=0, keepdims=True) + LN_EPS)
        heads.append(o * (1.0 - lambda_init))
    o_t = jnp.concatenate(heads, axis=0)
    o_ref[0] = (o_t.T * g_ref[...]).astype(BF16)


def _diff_attention(dq_t, dk, dv_t, lam_params, subln_g, lambda_init):
    b, w, s = dq_t.shape
    tq, tk = ATT_Q_TILE, ATT_K_TILE
    return pl.pallas_call(
        functools.partial(_diff_kernel, tq=tq, tk=tk, lambda_init=lambda_init),
        grid=(b, s // tq),
        in_specs=[
            pl.BlockSpec((1, w, tq), lambda i, j: (i, 0, j)),
            pl.BlockSpec((1, s, w), lambda i, j: (i, 0, 0)),
            pl.BlockSpec((1, w, s), lambda i, j: (i, 0, 0)),
            pl.BlockSpec((4, DIFF_QK_DIM), lambda i, j: (0, 0)),
            pl.BlockSpec((1, w), lambda i, j: (0, 0)),
        ],
        out_specs=pl.BlockSpec((1, tq, w), lambda i, j: (i, j, 0)),
        out_shape=jax.ShapeDtypeStruct((b, s, w), BF16),
        scratch_shapes=[pltpu.VMEM((DIFF_MAPS, MXU_COLS, tq), BF16), pltpu.VMEM((DIFF_MAPS * HEAD_DIM, tq), F32),
                        pltpu.VMEM((QK_AHEAD, tk, tq), F32)],
        compiler_params=_cparams("parallel", "arbitrary"),
        name="diff_attention",
    )(dq_t, dk, dv_t, lam_params, subln_g)


def _nsa_cmp_scores(qt_ref, kc_ref, t0, width):
    cols = slice(t0, t0 + width)
    return [_dot(kc_ref[0], qt_ref[0, _head_rows(h, HEAD_DIM), cols]) for h in range(NSA_HEADS)]


def _nsa_select(scores, vct_ref, g_ref, ovl_ref, sel_ref, ocmp_ref, t0, width):
    cols = slice(t0, t0 + width)
    t_q = t0 + lax.broadcasted_iota(jnp.int32, (1, width), 1)
    block_end = CMP_STRIDE * lax.broadcasted_iota(jnp.int32, (N_CMP, 1), 0) + (CMP_BLOCK - 1)
    vis = block_end <= t_q
    any_vis = t_q >= CMP_BLOCK - 1
    lhs = jnp.concatenate([vct_ref[0], ovl_ref[...], jnp.ones((BF16_SUBLANES, N_CMP), BF16)], axis=0)
    imp = jnp.zeros((N_SLC, width), F32)
    for h in range(NSA_HEADS):
        rows = _head_rows(h, HEAD_DIM)
        s = jnp.where(vis, scores[h], NEG_BIG)
        p = jnp.exp2(s - jnp.max(s, axis=0, keepdims=True))
        r = _dot(lhs, p.astype(BF16))
        l = r[HEAD_DIM + N_SLC:HEAD_DIM + N_SLC + 1]
        scale = jnp.where(any_vis, 1.0 / l, 0.0)
        ocmp_ref[rows, cols] = (g_ref[0, 3 * h:3 * h + 1, cols] * scale) * r[0:HEAD_DIM]
        imp = imp + scale * r[HEAD_DIM:HEAD_DIM + N_SLC]
    j_idx = lax.broadcasted_iota(jnp.int32, (N_SLC, 1), 0)
    blk_t = t_q // SLC_BLOCK
    forced = (j_idx == 0) | (j_idx == blk_t) | (j_idx == blk_t - 1)
    valid = j_idx * SLC_BLOCK <= t_q
    score = jnp.where(forced, 1e9, jnp.where(valid, imp, -1.0))
    groups = [score[g * 8:(g + 1) * 8] for g in range(N_SLC // 8)]
    ranks = [jnp.zeros((8, width), F32) for _ in groups]
    row = lax.broadcasted_iota(jnp.int32, (8, 1), 0)
    for i in range(N_SLC):
        gi, ri = divmod(i, 8)
        si = groups[gi][ri:ri + 1, :]
        for g, sg in enumerate(groups):
            if g < gi:
                ranks[g] = ranks[g] + jnp.where(si > sg, 1.0, 0.0)
            elif g > gi:
                ranks[g] = ranks[g] + jnp.where(si >= sg, 1.0, 0.0)
            else:
                tie = jnp.where(row > ri, 1.0, 0.0)
                ranks[g] = ranks[g] + jnp.where(si > sg, 1.0, 0.0) + jnp.where(si == sg, tie, 0.0)
    rank = jnp.concatenate(ranks, axis=0)
    sel_ref[:, cols] = jnp.where(rank < float(SLC_TOPK), 0.0, NEG_BIG).astype(BF16)


def _nsa_kernel(qt_ref, kc_ref, vct_ref, ks_ref, vst_ref, kw_ref, vwt_ref, g_ref, ovl_ref,
                o_ref, sel_ref, ocmp_ref, qaug_ref, acc_ref, out_ref, sbuf_ref, *, tq, tk, seq):
    qi = pl.program_id(1)
    q0 = pl.multiple_of(qi * tq, tq)

    @pl.when(qi == 0)
    def _():
        nxt = _nsa_cmp_scores(qt_ref, kc_ref, 0, NSA_PRE_TILE)
        for t0 in range(0, seq, NSA_PRE_TILE):
            scores = nxt
            if t0 + NSA_PRE_TILE < seq:
                nxt = _nsa_cmp_scores(qt_ref, kc_ref, t0 + NSA_PRE_TILE, NSA_PRE_TILE)
            _nsa_select(scores, vct_ref, g_ref, ovl_ref, sel_ref, ocmp_ref, t0, NSA_PRE_TILE)

    pad = jnp.zeros((LANES - HEAD_DIM - N_SLC, tq), BF16)
    sel_neg = sel_ref[:, pl.ds(q0, tq)]
    for h in range(NSA_HEADS):
        qaug_ref[h] = jnp.concatenate([qt_ref[0, _head_rows(h, HEAD_DIM), pl.ds(q0, tq)], sel_neg, pad], axis=0)

    def scores(i, k0, cols):
        k_ref = ks_ref if i < NSA_HEADS else kw_ref
        return _dot(k_ref[0, pl.ds(k0, tk), :], qaug_ref[i % NSA_HEADS, :, cols])

    def values(i, k0):
        vt_ref = vst_ref if i < NSA_HEADS else vwt_ref
        return vt_ref[0, :, pl.ds(k0, tk)]

    def mask(i, k0, diagonal, q_lo, width):
        d, off = _key_minus_query(tk, width, k0, q0 + q_lo)
        if diagonal:
            return d <= off
        return None if i < NSA_HEADS else d > off - WINDOW

    ls = _attend_t(2 * NSA_HEADS, tq, tk, qi, 0, scores, values, acc_ref, sbuf_ref, mask=mask,
                   n_early=NSA_HEADS, join=jnp.maximum(q0 - WINDOW, 0) // tk)
    for h in range(NSA_HEADS):
        rows = _head_rows(h, HEAD_DIM)
        out = ocmp_ref[rows, pl.ds(q0, tq)]
        for branch in (1, 2):
            i = (branch - 1) * NSA_HEADS + h
            gate = g_ref[0, 3 * h + branch:3 * h + branch + 1, pl.ds(q0, tq)]
            out = out + (gate * (1.0 / ls[i])) * acc_ref[_head_rows(i, HEAD_DIM), :]
        out_ref[rows, :] = out
    o_ref[0] = out_ref[...].T.astype(BF16)


def _nsa_attention(nq_t, kc, vc_t, kslc, vslc_t, kwin, vwin_t, gates_t, ovl):
    b, w, s = nq_t.shape
    tq, tk = NSA_Q_TILE, ATT_K_TILE
    seq = lambda i, j: (i, 0, 0)
    return pl.pallas_call(
        functools.partial(_nsa_kernel, tq=tq, tk=tk, seq=s),
        grid=(b, s // tq),
        in_specs=[
            pl.BlockSpec((1, w, s), seq),
            pl.BlockSpec((1, N_CMP, HEAD_DIM), seq),
            pl.BlockSpec((1, HEAD_DIM, N_CMP), seq),
            pl.BlockSpec((1, s, LANES), seq),
            pl.BlockSpec((1, HEAD_DIM, s), seq),
            pl.BlockSpec((1, s, LANES), seq),
            pl.BlockSpec((1, HEAD_DIM, s), seq),
            pl.BlockSpec((1, GATE_ROWS, s), seq),
            pl.BlockSpec((N_SLC, N_CMP), lambda i, j: (0, 0)),
        ],
        out_specs=pl.BlockSpec((1, tq, w), lambda i, j: (i, j, 0)),
        out_shape=jax.ShapeDtypeStruct((b, s, w), BF16),
        scratch_shapes=[pltpu.VMEM((N_SLC, s), BF16), pltpu.VMEM((w, s), F32),
                        pltpu.VMEM((NSA_HEADS, LANES, tq), BF16), pltpu.VMEM((2 * w, tq), F32),
                        pltpu.VMEM((w, tq), F32), pltpu.VMEM((QK_AHEAD, tk, tq), F32)],
        compiler_params=_cparams("parallel", "arbitrary"),
        name="nsa_attention",
    )(nq_t, kc, vc_t, kslc, vslc_t, kwin, vwin_t, gates_t, ovl)


def _out_ln_kernel(x_ref, on_ref, od_ref, of_ref, wn_ref, wd_ref, wf_ref, lng_ref, lnb_ref, o_ref):
    h = _dot(on_ref[...], wn_ref[...]) + _dot(od_ref[...], wd_ref[...]) + _dot(of_ref[...], wf_ref[...])
    o_ref[...] = _layer_norm(DEEPNORM_ALPHA * x_ref[...] + h, lng_ref[...], lnb_ref[...])


def _out_ln(x, o_nsa, o_diff, o_fox, wn, wd, wf, ln_g, ln_b):
    t = x.shape[0]
    tm = ROW_TILE
    row = lambda i: (i, 0)
    c2 = lambda i: (0, 0)
    return pl.pallas_call(
        _out_ln_kernel,
        grid=(t // tm,),
        in_specs=[
            pl.BlockSpec((tm, D_MODEL), row),
            pl.BlockSpec((tm, NSA_WIDTH), row),
            pl.BlockSpec((tm, DIFF_WIDTH), row),
            pl.BlockSpec((tm, FOX_WIDTH), row),
            pl.BlockSpec((NSA_WIDTH, D_MODEL), c2),
            pl.BlockSpec((DIFF_WIDTH, D_MODEL), c2),
            pl.BlockSpec((FOX_WIDTH, D_MODEL), c2),
            pl.BlockSpec((1, D_MODEL), c2),
            pl.BlockSpec((1, D_MODEL), c2),
        ],
        out_specs=pl.BlockSpec((tm, D_MODEL), row),
        out_shape=jax.ShapeDtypeStruct((t, D_MODEL), F32),
        compiler_params=_cparams("parallel"),
        name="out_ln",
    )(x, o_nsa, o_diff, o_fox, wn, wd, wf, ln_g, ln_b)


def _prep_ffn(wg, wu, wd):
    pad = D_FF_PAD - D_FF
    up = lambda w: jnp.pad(w, ((0, 0), (0, pad))).astype(BF16)
    return up(wg), up(wu), jnp.pad(wd, ((0, pad), (0, 0))).astype(BF16)


def _rot_cols(w, d):
    k, n = w.shape
    w = w.reshape(k, n // d, d)
    return jnp.concatenate([-w[..., d // 2:], w[..., :d // 2]], axis=-1).reshape(k, n)


def _rope_table(pos, d, signed=False):
    inv = ROPE_THETA ** (-jnp.arange(0, d, 2, dtype=F32) / d)
    ang = pos.astype(F32)[:, None] * inv[None, :]
    cos = jnp.concatenate([jnp.cos(ang), jnp.cos(ang)], axis=-1)
    sin = jnp.concatenate([-jnp.sin(ang) if signed else jnp.sin(ang), jnp.sin(ang)], axis=-1)
    return cos, sin


def _prep_in_proj(w_in, fox_b_f, s):
    o = 0
    nsa_q = w_in[:, o:o + NSA_WIDTH]; o += NSA_WIDTH
    kv = [w_in[:, o + i * HEAD_DIM:o + (i + 1) * HEAD_DIM] for i in range(6)]; o += 6 * HEAD_DIM
    k_cmp, v_cmp, k_slc, v_slc, k_win, v_win = kv
    nsa_g = w_in[:, o:o + N_NSA_GATES]; o += N_NSA_GATES
    diff_q = w_in[:, o:o + DIFF_WIDTH]; o += DIFF_WIDTH
    diff_k = w_in[:, o:o + DIFF_WIDTH]; o += DIFF_WIDTH
    diff_v = w_in[:, o:o + DIFF_WIDTH]; o += DIFF_WIDTH
    fox_q = w_in[:, o:o + FOX_WIDTH]; o += FOX_WIDTH
    fox_k = w_in[:, o:o + FOX_WIDTH]; o += FOX_WIDTH
    fox_v = w_in[:, o:o + FOX_WIDTH]; o += FOX_WIDTH
    fox_f = w_in[:, o:o + FOX_HEADS]
    zero = jnp.zeros((D_MODEL, LANES - HEAD_DIM), F32)

    wa = jnp.concatenate([k_slc, zero, k_win, zero, diff_k], axis=1)
    wb = jnp.concatenate([fox_k, k_cmp, v_cmp], axis=1)
    wc = jnp.concatenate([fox_f, jnp.zeros((D_MODEL, LANES - FOX_HEADS), F32)], axis=1)
    bc = jnp.concatenate([fox_b_f, jnp.zeros((LANES - FOX_HEADS,), F32)])[None, :]
    wd = jnp.concatenate([nsa_q, diff_q], axis=1).T
    we = jnp.concatenate([v_slc, v_win, diff_v, fox_q * (HEAD_DIM ** -0.5 * LOG2E), fox_v], axis=1).T
    wf = jnp.concatenate([nsa_g, jnp.zeros((D_MODEL, GATE_ROWS - N_NSA_GATES), F32)], axis=1).T

    pos = jnp.arange(s, dtype=jnp.int32)
    c64, s64 = _rope_table(pos, HEAD_DIM, signed=True)
    c32, s32 = _rope_table(pos, DIFF_QK_DIM, signed=True)
    nsa_scale = HEAD_DIM ** -0.5 * LOG2E
    diff_scale = DIFF_QK_DIM ** -0.5 * LOG2E

    def tm_table(t64, t32):
        return jnp.concatenate([t64, t64, t64, t64, jnp.tile(t32, (1, DIFF_MAPS))], axis=1)

    def fm_table(t64, t32):
        return jnp.concatenate([jnp.tile(t64, (1, NSA_HEADS)) * nsa_scale,
                                jnp.tile(t32, (1, DIFF_MAPS)) * diff_scale], axis=1).T

    bf = lambda w: w.astype(BF16)
    return (bf(wa), bf(wb), bf(wc), bf(wd), bf(we), bf(wf),
            tm_table(c64, c32), tm_table(s64, s32), fm_table(c64, c32), fm_table(s64, s32), bc)


def _prep_compress(pos_k, pos_v, phi_k1, phi_k2, phi_v1, phi_v2):
    half = CMP_STRIDE * HEAD_DIM

    def pe(p):
        return jnp.broadcast_to(p.reshape(2, 1, half), (2, 8, half)).astype(BF16)

    block_end = jnp.arange(N_CMP, dtype=jnp.int32) * CMP_STRIDE + (CMP_BLOCK - 1)
    cos_c, sin_c = _rope_table(block_end, HEAD_DIM)
    return (pe(pos_k), pe(pos_v), phi_k1.reshape(2, half, CMP_HIDDEN).astype(BF16), phi_k2.astype(BF16),
            _rot_cols(phi_k2, HEAD_DIM).astype(BF16), phi_v1.reshape(2, half, CMP_HIDDEN).astype(BF16),
            phi_v2.T.astype(BF16), cos_c, sin_c)


def _overlap_matrix(s):
    c0 = np.arange(N_CMP) * CMP_STRIDE
    s0 = np.arange(N_SLC) * SLC_BLOCK
    ovl = (c0[None, :] < s0[:, None] + SLC_BLOCK) & (c0[None, :] + CMP_BLOCK > s0[:, None])
    ovl[:, (s - CMP_BLOCK) // CMP_STRIDE + 1:] = False
    return jnp.asarray(ovl.astype(np.float32), dtype=BF16)


def kernel(x, p, ln_g, ln_b, ffn1_w_gate, ffn1_w_up, ffn1_w_down, ffn2_w_gate, ffn2_w_up, ffn2_w_down, w_in, fox_b_f, nsa_pos_k, nsa_pos_v, nsa_phi_k1, nsa_phi_k2, nsa_phi_v1, nsa_phi_v2, diff_lambda, diff_subln_g, w_out, ple_w_gate, ple_b_gate, ple_w_proj):
    b, s, _ = x.shape
    assert s // SLC_BLOCK == N_SLC and (s - CMP_BLOCK) // CMP_STRIDE + 1 <= N_CMP
    t = b * s
    ovl = _overlap_matrix(s)
    tri = jnp.asarray(np.tril(np.ones((CUM_TILE, CUM_TILE), np.float32)))
    x = x.reshape(t, D_MODEL)
    for i in range(DEPTH):
        lambda_init = 0.8 - 0.6 * math.exp(-0.3 * i)
        lng = ln_g[i][:, None, :]
        lnb = ln_b[i][:, None, :]
        x = _ffn_ln(x, *_prep_ffn(ffn1_w_gate[i], ffn1_w_up[i], ffn1_w_down[i]), lng[0], lnb[0])
        proj = _in_proj(x.reshape(b, s, D_MODEL), *_prep_in_proj(w_in[i], fox_b_f[i], s))
        kslc, kwin, dk, fk, kcmp, vcmp, fls, nq_t, dq_t, vslc_t, vwin_t, dv_t, fq_t, fv_t, gates_t = proj
        half = CMP_STRIDE * HEAD_DIM
        kc, vc_t = _compress(kcmp.reshape(b, s // CMP_STRIDE, half), vcmp.reshape(b, s // CMP_STRIDE, half),
                             *_prep_compress(nsa_pos_k[i], nsa_pos_v[i], nsa_phi_k1[i], nsa_phi_k2[i],
                                             nsa_phi_v1[i], nsa_phi_v2[i]))
        ccol, crow = _decay(fls, tri)
        o_nsa = _nsa_attention(nq_t, kc, vc_t, kslc, vslc_t, kwin, vwin_t, gates_t, ovl)
        o_diff = _diff_attention(dq_t, dk, dv_t, diff_lambda[i], jnp.tile(diff_subln_g[i], DIFF_HEADS)[None, :],
                                 lambda_init)
        o_fox = _fox_attention(fq_t, fk, fv_t, ccol, crow)
        wo = w_out[i].astype(BF16)
        x = _out_ln(x, o_nsa.reshape(t, NSA_WIDTH), o_diff.reshape(t, DIFF_WIDTH), o_fox.reshape(t, FOX_WIDTH),
                    wo[:NSA_WIDTH], wo[NSA_WIDTH:NSA_WIDTH + DIFF_WIDTH], wo[NSA_WIDTH + DIFF_WIDTH:],
                    lng[1], lnb[1])
        ple = (p.reshape(DEPTH, t, PLE_DIM), i, ple_w_gate[i].astype(BF16), ple_b_gate[i][None, :],
               ple_w_proj[i].astype(BF16))
        x = _ffn_ln(x, *_prep_ffn(ffn2_w_gate[i], ffn2_w_up[i], ffn2_w_down[i]), lng[2], lnb[2], ple=ple)
    return x.reshape(b, s, D_MODEL)
```

```python
import functools
import math

import numpy as np
import jax
import jax.numpy as jnp
from jax import lax
from jax.experimental import pallas as pl
from jax.experimental.pallas import tpu as pltpu

F32 = jnp.float32
BF16 = jnp.bfloat16

D_MODEL = 1024
HEAD_DIM = 64
NSA_HEADS = 4
DIFF_HEADS = 4
FOX_HEADS = 8
NSA_WIDTH = NSA_HEADS * HEAD_DIM
DIFF_WIDTH = DIFF_HEADS * HEAD_DIM
FOX_WIDTH = FOX_HEADS * HEAD_DIM
CMP_BLOCK = 32
CMP_STRIDE = 16
CMP_HIDDEN = 256
SLC_BLOCK = 64
SLC_TOPK = 16
WINDOW = 512
DIFF_QK_DIM = HEAD_DIM // 2
DIFF_MAPS = 2 * DIFF_HEADS
D_FF = 2752
PLE_DIM = 256
ROPE_THETA = 10000.0
LN_EPS = 1e-5
NEG_BIG = -1e30
DEPTH = 2
DEEPNORM_ALPHA = (2.0 * DEPTH) ** 0.25
LOG2E = math.log2(math.e)

LANES = 128
MXU_COLS = 256
BF16_SUBLANES = 16
VMEM_LIMIT_BYTES = 56 * 1024 * 1024

D_FF_PAD = ((D_FF + MXU_COLS - 1) // MXU_COLS) * MXU_COLS
FF_CHUNK = 2 * MXU_COLS
FF_BOUNDS = tuple((c0, min(c0 + FF_CHUNK, D_FF_PAD)) for c0 in range(0, D_FF_PAD, FF_CHUNK))

ROW_TILE = 512
ATT_Q_TILE = 512
ATT_K_TILE = 256
NSA_Q_TILE = 512
NSA_PRE_TILE = 512
CUM_TILE = 256
QK_AHEAD = 4

TM_ROPE_COLS = 2 * LANES + DIFF_WIDTH
FOX_GROUP_HEADS = 3
FOX_QK_WIDTH = -(-FOX_HEADS // FOX_GROUP_HEADS) * MXU_COLS
TM_PLAIN_COLS = FOX_QK_WIDTH + 2 * HEAD_DIM
FM_ROPE_ROWS = NSA_WIDTH + DIFF_WIDTH
FM_PLAIN_ROWS = 2 * HEAD_DIM + DIFF_WIDTH + FOX_QK_WIDTH + FOX_WIDTH
N_NSA_GATES = 3 * NSA_HEADS
GATE_ROWS = BF16_SUBLANES
N_CMP = 128
N_SLC = 32


def _cparams(*sem):
    return pltpu.CompilerParams(dimension_semantics=sem, vmem_limit_bytes=VMEM_LIMIT_BYTES)


def _dot(a, b):
    return jnp.dot(a, b, preferred_element_type=F32)


def _dot_nt(a, b):
    return lax.dot_general(a, b, (((1,), (1,)), ((), ())), preferred_element_type=F32)


def _sigmoid(z):
    return 1.0 / (1.0 + jnp.exp(-z))


def _layer_norm(y, g, b):
    mu = jnp.mean(y, axis=-1, keepdims=True)
    yc = y - mu
    var = jnp.mean(yc * yc, axis=-1, keepdims=True)
    return yc * lax.rsqrt(var + LN_EPS) * g + b


def _ffn_kernel(*refs, with_ple):
    if with_ple:
        (x_ref, wg_ref, wu_ref, wd_ref, lng_ref, lnb_ref, p_ref, pwg_ref, pbg_ref, pwp_ref,
         o_ref, xb_ref) = refs
    else:
        x_ref, wg_ref, wu_ref, wd_ref, lng_ref, lnb_ref, o_ref, xb_ref = refs
    xb_ref[...] = x_ref[...].astype(BF16)
    xb = xb_ref[...]

    def gate_up(c):
        c0, c1 = FF_BOUNDS[c]
        return _dot(xb, wg_ref[:, c0:c1]), _dot(xb, wu_ref[:, c0:c1])

    nxt = gate_up(0)
    acc = None
    for c, (c0, c1) in enumerate(FF_BOUNDS):
        g, u = nxt
        if c + 1 < len(FF_BOUNDS):
            nxt = gate_up(c + 1)
        h = (g * _sigmoid(g)) * u
        part = _dot(h.astype(BF16), wd_ref[c0:c1, :])
        acc = part if acc is None else acc + part
    y = DEEPNORM_ALPHA * x_ref[...] + 0.5 * acc
    z = _layer_norm(y, lng_ref[...], lnb_ref[...])
    if with_ple:
        gate = _sigmoid(_dot(z.astype(BF16), pwg_ref[...]) + pbg_ref[...])
        z = z + gate * _dot(p_ref[0].astype(BF16), pwp_ref[...])
    o_ref[...] = z


def _ffn_ln(x, wg, wu, wd, ln_g, ln_b, ple=None):
    t = x.shape[0]
    tm = ROW_TILE
    row = lambda i: (i, 0)
    full2 = lambda i: (0, 0)
    in_specs = [
        pl.BlockSpec((tm, D_MODEL), row),
        pl.BlockSpec((D_MODEL, D_FF_PAD), full2),
        pl.BlockSpec((D_MODEL, D_FF_PAD), full2),
        pl.BlockSpec((D_FF_PAD, D_MODEL), full2),
        pl.BlockSpec((1, D_MODEL), full2),
        pl.BlockSpec((1, D_MODEL), full2),
    ]
    args = [x, wg, wu, wd, ln_g, ln_b]
    if ple is not None:
        p, layer, pwg, pbg, pwp = ple
        in_specs += [
            pl.BlockSpec((1, tm, PLE_DIM), lambda i: (layer, i, 0)),
            pl.BlockSpec((D_MODEL, D_MODEL), full2),
            pl.BlockSpec((1, D_MODEL), full2),
            pl.BlockSpec((PLE_DIM, D_MODEL), full2),
        ]
        args += [p, pwg, pbg, pwp]
    return pl.pallas_call(
        functools.partial(_ffn_kernel, with_ple=ple is not None),
        grid=(t // tm,),
        in_specs=in_specs,
        out_specs=pl.BlockSpec((tm, D_MODEL), row),
        out_shape=jax.ShapeDtypeStruct((t, D_MODEL), F32),
        scratch_shapes=[pltpu.VMEM((tm, D_MODEL), BF16)],
        compiler_params=_cparams("parallel"),
        name="ffn_ln_ple" if ple is not None else "ffn_ln",
    )(*args)


def _swap_halves_lanes(x, d):
    w = x.shape[1]
    lane = lax.broadcasted_iota(jnp.int32, (1, w), 1)
    return jnp.where(lane % d < d // 2, pltpu.roll(x, w - d // 2, 1), pltpu.roll(x, d // 2, 1))


def _swap_halves_rows(x, d):
    parts = []
    for r0 in range(0, x.shape[0], d):
        parts += [x[r0 + d // 2:r0 + d], x[r0:r0 + d // 2]]
    return jnp.concatenate(parts, axis=0)


def _in_proj_kernel(x_ref, wa_ref, wb_ref, wc_ref, wd_ref, we_ref, wf_ref,
                    cosa_ref, sina_ref, cosd_ref, sind_ref, bc_ref,
                    kslc_ref, kwin_ref, dk_ref, fk_ref, kcmp_ref, vcmp_ref, fls_ref,
                    nq_ref, dq_ref, vslc_ref, vwin_ref, dv_ref, fq_ref, fv_ref, gate_ref, *, tm):
    xb = x_ref[0].astype(BF16)
    pa = _dot(xb, wa_ref[...])
    swapped = jnp.concatenate([_swap_halves_lanes(pa[:, 0:2 * LANES], HEAD_DIM),
                               _swap_halves_lanes(pa[:, 2 * LANES:], DIFF_QK_DIM)], axis=1)
    ra = pa * cosa_ref[...] + swapped * sina_ref[...]
    pos = pl.program_id(1) * tm + lax.broadcasted_iota(jnp.int32, (tm, 1), 0)
    lane = lax.broadcasted_iota(jnp.int32, (1, LANES), 1)
    block_id = jnp.where(lane - HEAD_DIM == pos // SLC_BLOCK, 1.0, 0.0)
    kslc_ref[0] = (ra[:, 0:LANES] + block_id).astype(BF16)
    kwin_ref[0] = ra[:, LANES:2 * LANES].astype(BF16)
    dk_ref[0] = ra[:, 2 * LANES:].astype(BF16)
    pb = _dot(xb, wb_ref[...]).astype(BF16)
    fk_ref[0] = pb[:, 0:FOX_QK_WIDTH]
    kcmp_ref[0] = pb[:, FOX_QK_WIDTH:FOX_QK_WIDTH + HEAD_DIM]
    vcmp_ref[0] = pb[:, FOX_QK_WIDTH + HEAD_DIM:]
    z = _dot(xb, wc_ref[...]) + bc_ref[...]
    fls_ref[0] = jnp.minimum(z, 0.0) - jnp.log(1.0 + jnp.exp(-jnp.abs(z)))

    pd = _dot_nt(wd_ref[...], xb)
    swapped = jnp.concatenate([_swap_halves_rows(pd[0:NSA_WIDTH], HEAD_DIM),
                               _swap_halves_rows(pd[NSA_WIDTH:], DIFF_QK_DIM)], axis=0)
    rd = pd * cosd_ref[...] + swapped * sind_ref[...]
    nq_ref[0] = rd[0:NSA_WIDTH].astype(BF16)
    dq_ref[0] = rd[NSA_WIDTH:].astype(BF16)
    pe = _dot_nt(we_ref[...], xb).astype(BF16)
    vslc_ref[0] = pe[0:HEAD_DIM]
    vwin_ref[0] = pe[HEAD_DIM:2 * HEAD_DIM]
    off = 2 * HEAD_DIM
    dv_ref[0] = pe[off:off + DIFF_WIDTH]
    off += DIFF_WIDTH
    fq_ref[0] = pe[off:off + FOX_QK_WIDTH]
    off += FOX_QK_WIDTH
    fv_ref[0] = pe[off:off + FOX_WIDTH]
    gate_ref[0] = _sigmoid(_dot_nt(wf_ref[...], xb))


def _in_proj(x, wa, wb, wc, wd, we, wf, cosa, sina, cosd, sind, bc):
    b, s, _ = x.shape
    tm = ROW_TILE
    grid = (b, s // tm)
    w2 = lambda i, j: (0, 0)
    tok = lambda i, j: (i, j, 0)
    fm = lambda i, j: (i, 0, j)
    in_specs = [
        pl.BlockSpec((1, tm, D_MODEL), tok),
        pl.BlockSpec((D_MODEL, TM_ROPE_COLS), w2),
        pl.BlockSpec((D_MODEL, TM_PLAIN_COLS), w2),
        pl.BlockSpec((D_MODEL, LANES), w2),
        pl.BlockSpec((FM_ROPE_ROWS, D_MODEL), w2),
        pl.BlockSpec((FM_PLAIN_ROWS, D_MODEL), w2),
        pl.BlockSpec((GATE_ROWS, D_MODEL), w2),
        pl.BlockSpec((tm, TM_ROPE_COLS), lambda i, j: (j, 0)),
        pl.BlockSpec((tm, TM_ROPE_COLS), lambda i, j: (j, 0)),
        pl.BlockSpec((FM_ROPE_ROWS, tm), lambda i, j: (0, j)),
        pl.BlockSpec((FM_ROPE_ROWS, tm), lambda i, j: (0, j)),
        pl.BlockSpec((1, LANES), w2),
    ]

    def tok_out(width, dtype=BF16):
        return pl.BlockSpec((1, tm, width), tok), jax.ShapeDtypeStruct((b, s, width), dtype)

    def fm_out(rows, dtype=BF16):
        return pl.BlockSpec((1, rows, tm), fm), jax.ShapeDtypeStruct((b, rows, s), dtype)

    outs = [
        tok_out(LANES), tok_out(LANES), tok_out(DIFF_WIDTH), tok_out(FOX_QK_WIDTH),
        tok_out(HEAD_DIM), tok_out(HEAD_DIM), tok_out(LANES, F32),
        fm_out(NSA_WIDTH), fm_out(DIFF_WIDTH),
        fm_out(HEAD_DIM), fm_out(HEAD_DIM), fm_out(DIFF_WIDTH),
        fm_out(FOX_QK_WIDTH), fm_out(FOX_WIDTH), fm_out(GATE_ROWS, F32),
    ]
    return pl.pallas_call(
        functools.partial(_in_proj_kernel, tm=tm),
        grid=grid,
        in_specs=in_specs,
        out_specs=[o[0] for o in outs],
        out_shape=[o[1] for o in outs],
        compiler_params=_cparams("parallel", "parallel"),
        name="in_proj",
    )(x, wa, wb, wc, wd, we, wf, cosa, sina, cosd, sind, bc)


def _gelu_tanh(x):
    return 0.5 * x * (1.0 + jnp.tanh(math.sqrt(2.0 / math.pi) * (x + 0.044715 * (x * x * x))))


def _compress_kernel(k16_ref, v16_ref, pek_ref, pev_ref, wk1_ref, wk2_ref, wk2r_ref, wv1_ref, wv2t_ref,
                     cos_ref, sin_ref, kc_ref, vct_ref):
    def hidden(x16, pe_ref, w1_ref):
        top = _dot(x16, w1_ref[0])
        bot = _dot(x16, w1_ref[1])
        bias = _dot(pe_ref[0], w1_ref[0]) + _dot(pe_ref[1], w1_ref[1])
        bot = pltpu.roll(bot, N_CMP - 1, 0)
        return _gelu_tanh(top + bot + bias[0:1, :]).astype(BF16)

    hk = hidden(k16_ref[0], pek_ref, wk1_ref)
    kc = _dot(hk, wk2_ref[...]) * cos_ref[...] + _dot(hk, wk2r_ref[...]) * sin_ref[...]
    kc_ref[0] = kc.astype(BF16)
    hv = hidden(v16_ref[0], pev_ref, wv1_ref)
    vct_ref[0] = _dot_nt(wv2t_ref[...], hv).astype(BF16)


def _compress(k16, v16, pek, pev, wk1, wk2, wk2r, wv1, wv2t, cos_c, sin_c):
    b = k16.shape[0]
    half = CMP_STRIDE * HEAD_DIM
    bat = lambda i: (i, 0, 0)
    c2 = lambda i: (0, 0)
    c3 = lambda i: (0, 0, 0)
    in_specs = [
        pl.BlockSpec((1, N_CMP, half), bat),
        pl.BlockSpec((1, N_CMP, half), bat),
        pl.BlockSpec((2, 8, half), c3),
        pl.BlockSpec((2, 8, half), c3),
        pl.BlockSpec((2, half, CMP_HIDDEN), c3),
        pl.BlockSpec((CMP_HIDDEN, HEAD_DIM), c2),
        pl.BlockSpec((CMP_HIDDEN, HEAD_DIM), c2),
        pl.BlockSpec((2, half, CMP_HIDDEN), c3),
        pl.BlockSpec((HEAD_DIM, CMP_HIDDEN), c2),
        pl.BlockSpec((N_CMP, HEAD_DIM), c2),
        pl.BlockSpec((N_CMP, HEAD_DIM), c2),
    ]
    return pl.pallas_call(
        _compress_kernel,
        grid=(b,),
        in_specs=in_specs,
        out_specs=[pl.BlockSpec((1, N_CMP, HEAD_DIM), bat), pl.BlockSpec((1, HEAD_DIM, N_CMP), bat)],
        out_shape=[jax.ShapeDtypeStruct((b, N_CMP, HEAD_DIM), BF16), jax.ShapeDtypeStruct((b, HEAD_DIM, N_CMP), BF16)],
        compiler_params=_cparams("parallel"),
        name="nsa_compress",
    )(k16, v16, pek, pev, wk1, wk2, wk2r, wv1, wv2t, cos_c, sin_c)


def _key_minus_query(tk, tq, k0, q0):
    d = lax.broadcasted_iota(jnp.int32, (tk, tq), 0) - lax.broadcasted_iota(jnp.int32, (tk, tq), 1)
    return d, q0 - k0


def _causal_mask(tk, q0):
    def mask(i, k0, diagonal, q_lo, width):
        if not diagonal:
            return None
        d, off = _key_minus_query(tk, width, k0, q0 + q_lo)
        return d <= off
    return mask


def _head_rows(h, width):
    return slice(h * width, (h + 1) * width)


def _attend_t(n, tq, tk, qi, lo, scores, values, acc_ref, sbuf_ref, *, mask, n_early=None, join=None):
    n_early = n if n_early is None else n_early
    ahead = min(QK_AHEAD, n_early)
    per_tile = tq // tk
    full = slice(0, tq)
    acc_ref[...] = jnp.zeros(acc_ref.shape, F32)
    k_lo = pl.multiple_of(lo * tk, tk)
    for i in range(ahead):
        sbuf_ref[i] = scores(i, k_lo, full)
    ones = jnp.ones((BF16_SUBLANES, tk), BF16)

    def step(c, heads, diag, ms, ls):
        diagonal = diag is not None
        q_lo = diag * tk if diagonal else 0
        cols = slice(q_lo, tq)
        nxt = slice(q_lo + tk, tq) if diagonal else full
        k0 = pl.multiple_of(c * tk, tk)
        k1 = pl.multiple_of(c * tk + tk, tk)
        ms, ls = list(ms), list(ls)
        pend = [sbuf_ref[i, :, cols] for i in range(ahead)]
        for i in range(heads):
            s = pend.pop(0)
            j = i + ahead
            if j < heads:
                pend.append(scores(j, k0, cols))
            elif nxt.start < tq:
                sbuf_ref[j - heads, :, nxt] = scores(j - heads, k1, nxt)
            mk = mask(i, k0, diagonal, q_lo, tq - q_lo)
            if mk is not None:
                s = jnp.where(mk, s, NEG_BIG)
            m_old, l_old = ms[i][:, cols], ls[i][:, cols]
            m_new = jnp.maximum(m_old, jnp.max(s, axis=0, keepdims=True))
            a = jnp.exp2(m_old - m_new)
            p = jnp.exp2(s - m_new)
            pv = _dot(jnp.concatenate([values(i, k0), ones], axis=0), p.astype(BF16))
            rows = _head_rows(i, HEAD_DIM)
            acc_ref[rows, cols] = a * acc_ref[rows, cols] + pv[0:HEAD_DIM]
            l_new = a * l_old + pv[HEAD_DIM:HEAD_DIM + 1]
            if q_lo:
                m_new = jnp.concatenate([ms[i][:, 0:q_lo], m_new], axis=1)
                l_new = jnp.concatenate([ls[i][:, 0:q_lo], l_new], axis=1)
            ms[i], ls[i] = m_new, l_new
        return tuple(ms), tuple(ls)

    def init(count):
        return (tuple(jnp.full((1, tq), NEG_BIG, F32) for _ in range(count)),
                tuple(jnp.zeros((1, tq), F32) for _ in range(count)))

    first_diag = qi * per_tile
    ms, ls = init(n_early)
    if n_early < n:
        ms, ls = lax.fori_loop(lo, join, lambda c, carry: step(c, n_early, None, *carry), (ms, ls))
        late = init(n - n_early)
        ms, ls, lo = ms + late[0], ls + late[1], join
    ms, ls = lax.fori_loop(lo, first_diag, lambda c, carry: step(c, n, None, *carry), (ms, ls))
    for d in range(per_tile):
        ms, ls = step(first_diag + d, n, d, ms, ls)
    return ls


def _pad_queries(q_t, qpad_ref, n, width):
    per = MXU_COLS // width
    row = lax.broadcasted_iota(jnp.int32, (MXU_COLS, q_t.shape[1]), 0)
    for i in range(n):
        g, r = divmod(i, per)
        qg = q_t[g * MXU_COLS:(g + 1) * MXU_COLS]
        qpad_ref[i] = jnp.where((row >= r * width) & (row < (r + 1) * width), qg, jnp.zeros_like(qg))


def _split3(x):
    hi = x.astype(BF16)
    r1 = x - hi.astype(F32)
    mid = r1.astype(BF16)
    lo = (r1 - mid.astype(F32)).astype(BF16)
    return hi, mid, lo


def _fox_bias_lane(h):
    g, r = divmod(h, FOX_GROUP_HEADS)
    return g, FOX_GROUP_HEADS * HEAD_DIM + 8 * r


def _fox_kernel(qt_ref, k_ref, vt_ref, fls_ref, tri_ref, place_ref, o_ref,
                kaug_ref, crow_ref, qpad_ref, acc_ref, sbuf_ref, *, tq, tk, seq):
    qi = pl.program_id(1)
    q0 = pl.multiple_of(qi * tq, tq)

    @pl.when(qi == 0)
    def _():
        carry = jnp.zeros((1, LANES), F32)
        for i in range(seq // CUM_TILE):
            rows = slice(i * CUM_TILE, (i + 1) * CUM_TILE)
            c = carry + sum(_dot(tri_ref[...], part) for part in _split3(fls_ref[0, rows, :]))
            carry = c[CUM_TILE - 1:CUM_TILE, :]
            c = c * LOG2E
            crow_ref[:, rows] = c.T[0:FOX_HEADS, :]
            hi, mid, lo = _split3(c)
            bias = (_dot(hi, place_ref[0]) + _dot(mid, place_ref[1]) + _dot(lo, place_ref[2])
                    + place_ref[3, 0:1, :].astype(F32))
            kaug_ref[rows, :] = (k_ref[0, rows, :].astype(F32) + bias).astype(BF16)

    n_q = FOX_GROUP_HEADS * HEAD_DIM
    row = lax.broadcasted_iota(jnp.int32, (n_q, tq), 0)
    brow = n_q + lax.broadcasted_iota(jnp.int32, (MXU_COLS - n_q, 1), 0)
    for h in range(FOX_HEADS):
        g, b0 = _fox_bias_lane(h)
        r = h % FOX_GROUP_HEADS
        qg = qt_ref[0, g * MXU_COLS:g * MXU_COLS + n_q, :]
        q = jnp.where((row >= r * HEAD_DIM) & (row < (r + 1) * HEAD_DIM), qg, jnp.zeros_like(qg))
        bias = jnp.where((brow >= b0) & (brow < b0 + 3), 1.0, 0.0)
        for j, part in enumerate(_split3(crow_ref[h:h + 1, pl.ds(q0, tq)])):
            bias = jnp.where(brow == b0 + 3 + j, part.astype(F32), bias)
        qpad_ref[h] = jnp.concatenate([q, bias.astype(BF16)], axis=0)

    ls = _attend_t(
        FOX_HEADS, tq, tk, qi, 0,
        lambda h, k0, cols: _dot(kaug_ref[pl.ds(k0, tk), _head_rows(h // FOX_GROUP_HEADS, MXU_COLS)],
                                 qpad_ref[h, :, cols]),
        lambda h, k0: vt_ref[0, _head_rows(h, HEAD_DIM), pl.ds(k0, tk)],
        acc_ref, sbuf_ref, mask=_causal_mask(tk, q0))
    for h in range(FOX_HEADS):
        rows = _head_rows(h, HEAD_DIM)
        acc_ref[rows, :] = acc_ref[rows, :] * (1.0 / ls[h])
    o_ref[0] = acc_ref[...].T.astype(BF16)


def _fox_attention(fq_t, fk, fv_t, fls, tri, place):
    b, w, s = fv_t.shape
    wk = fk.shape[2]
    tq, tk = ATT_Q_TILE, ATT_K_TILE
    return pl.pallas_call(
        functools.partial(_fox_kernel, tq=tq, tk=tk, seq=s),
        grid=(b, s // tq),
        in_specs=[
            pl.BlockSpec((1, wk, tq), lambda i, j: (i, 0, j)),
            pl.BlockSpec((1, s, wk), lambda i, j: (i, 0, 0)),
            pl.BlockSpec((1, w, s), lambda i, j: (i, 0, 0)),
            pl.BlockSpec((1, s, LANES), lambda i, j: (i, 0, 0)),
            pl.BlockSpec((CUM_TILE, CUM_TILE), lambda i, j: (0, 0)),
            pl.BlockSpec((4, LANES, wk), lambda i, j: (0, 0, 0)),
        ],
        out_specs=pl.BlockSpec((1, tq, w), lambda i, j: (i, j, 0)),
        out_shape=jax.ShapeDtypeStruct((b, s, w), BF16),
        scratch_shapes=[pltpu.VMEM((s, wk), BF16), pltpu.VMEM((FOX_HEADS, s), F32),
                        pltpu.VMEM((FOX_HEADS, MXU_COLS, tq), BF16), pltpu.VMEM((w, tq), F32),
                        pltpu.VMEM((QK_AHEAD, tk, tq), F32)],
        compiler_params=_cparams("parallel", "arbitrary"),
        name="fox_attention",
    )(fq_t, fk, fv_t, fls, tri, place)


def _diff_kernel(qt_ref, k_ref, vt_ref, lam_ref, g_ref, o_ref, qpad_ref, acc_ref, sbuf_ref, *, tq, tk, lambda_init):
    qi = pl.program_id(1)
    _pad_queries(qt_ref[0], qpad_ref, DIFF_MAPS, DIFF_QK_DIM)
    ls = _attend_t(
        DIFF_MAPS, tq, tk, qi, 0,
        lambda i, k0, cols: _dot(k_ref[0, pl.ds(k0, tk), :], qpad_ref[i, :, cols]),
        lambda i, k0: vt_ref[0, _head_rows(i // 2, HEAD_DIM), pl.ds(k0, tk)],
        acc_ref, sbuf_ref, mask=_causal_mask(tk, qi * tq))
    lp = lam_ref[...]
    lam = (jnp.exp(jnp.sum(lp[0:1] * lp[1:2], axis=1, keepdims=True))
           - jnp.exp(jnp.sum(lp[2:3] * lp[3:4], axis=1, keepdims=True)) + lambda_init)
    heads = []
    for h in range(DIFF_HEADS):
        o1 = acc_ref[_head_rows(2 * h, HEAD_DIM), :] * (1.0 / ls[2 * h])
        o2 = acc_ref[_head_rows(2 * h + 1, HEAD_DIM), :] * (1.0 / ls[2 * h + 1])
        o = o1 - lam * o2
        o = o * lax.rsqrt(jnp.mean(o * o, axis=0, keepdims=True) + LN_EPS)
        heads.append(o * (1.0 - lambda_init))
    o_t = jnp.concatenate(heads, axis=0)
    o_ref[0] = (o_t.T * g_ref[...]).astype(BF16)


def _diff_attention(dq_t, dk, dv_t, lam_params, subln_g, lambda_init):
    b, w, s = dq_t.shape
    tq, tk = ATT_Q_TILE, ATT_K_TILE
    return pl.pallas_call(
        functools.partial(_diff_kernel, tq=tq, tk=tk, lambda_init=lambda_init),
        grid=(b, s // tq),
        in_specs=[
            pl.BlockSpec((1, w, tq), lambda i, j: (i, 0, j)),
            pl.BlockSpec((1, s, w), lambda i, j: (i, 0, 0)),
            pl.BlockSpec((1, w, s), lambda i, j: (i, 0, 0)),
            pl.BlockSpec((4, DIFF_QK_DIM), lambda i, j: (0, 0)),
            pl.BlockSpec((1, w), lambda i, j: (0, 0)),
        ],
        out_specs=pl.BlockSpec((1, tq, w), lambda i, j: (i, j, 0)),
        out_shape=jax.ShapeDtypeStruct((b, s, w), BF16),
        scratch_shapes=[pltpu.VMEM((DIFF_MAPS, MXU_COLS, tq), BF16), pltpu.VMEM((DIFF_MAPS * HEAD_DIM, tq), F32),
                        pltpu.VMEM((QK_AHEAD, tk, tq), F32)],
        compiler_params=_cparams("parallel", "arbitrary"),
        name="diff_attention",
    )(dq_t, dk, dv_t, lam_params, subln_g)


def _nsa_cmp_scores(qt_ref, kc_ref, t0, width):
    cols = slice(t0, t0 + width)
    return [_dot(kc_ref[0], qt_ref[0, _head_rows(h, HEAD_DIM), cols]) for h in range(NSA_HEADS)]


def _nsa_select(scores, vct_ref, g_ref, ovl_ref, sel_ref, ocmp_ref, t0, width):
    cols = slice(t0, t0 + width)
    t_q = t0 + lax.broadcasted_iota(jnp.int32, (1, width), 1)
    block_end = CMP_STRIDE * lax.broadcasted_iota(jnp.int32, (N_CMP, 1), 0) + (CMP_BLOCK - 1)
    vis = block_end <= t_q
    any_vis = t_q >= CMP_BLOCK - 1
    lhs = jnp.concatenate([vct_ref[0], ovl_ref[...], jnp.ones((BF16_SUBLANES, N_CMP), BF16)], axis=0)
    imp = jnp.zeros((N_SLC, width), F32)
    for h in range(NSA_HEADS):
        rows = _head_rows(h, HEAD_DIM)
        s = jnp.where(vis, scores[h], NEG_BIG)
        p = jnp.exp2(s - jnp.max(s, axis=0, keepdims=True))
        r = _dot(lhs, p.astype(BF16))
        l = r[HEAD_DIM + N_SLC:HEAD_DIM + N_SLC + 1]
        scale = jnp.where(any_vis, 1.0 / l, 0.0)
        ocmp_ref[rows, cols] = (g_ref[0, 3 * h:3 * h + 1, cols] * scale) * r[0:HEAD_DIM]
        imp = imp + scale * r[HEAD_DIM:HEAD_DIM + N_SLC]
    j_idx = lax.broadcasted_iota(jnp.int32, (N_SLC, 1), 0)
    blk_t = t_q // SLC_BLOCK
    forced = (j_idx == 0) | (j_idx == blk_t) | (j_idx == blk_t - 1)
    valid = j_idx * SLC_BLOCK <= t_q
    score = jnp.where(forced, 1e9, jnp.where(valid, imp, -1.0))
    groups = [score[g * 8:(g + 1) * 8] for g in range(N_SLC // 8)]
    ranks = [jnp.zeros((8, width), F32) for _ in groups]
    row = lax.broadcasted_iota(jnp.int32, (8, 1), 0)
    for i in range(N_SLC):
        gi, ri = divmod(i, 8)
        si = groups[gi][ri:ri + 1, :]
        for g, sg in enumerate(groups):
            if g < gi:
                ranks[g] = ranks[g] + jnp.where(si > sg, 1.0, 0.0)
            elif g > gi:
                ranks[g] = ranks[g] + jnp.where(si >= sg, 1.0, 0.0)
            else:
                tie = jnp.where(row > ri, 1.0, 0.0)
                ranks[g] = ranks[g] + jnp.where(si > sg, 1.0, 0.0) + jnp.where(si == sg, tie, 0.0)
    rank = jnp.concatenate(ranks, axis=0)
    sel_ref[:, cols] = jnp.where(rank < float(SLC_TOPK), 0.0, NEG_BIG).astype(BF16)


def _nsa_kernel(qt_ref, kc_ref, vct_ref, ks_ref, vst_ref, kw_ref, vwt_ref, g_ref, ovl_ref,
                o_ref, sel_ref, ocmp_ref, qaug_ref, acc_ref, out_ref, sbuf_ref, *, tq, tk, seq):
    qi = pl.program_id(1)
    q0 = pl.multiple_of(qi * tq, tq)

    @pl.when(qi == 0)
    def _():
        nxt = _nsa_cmp_scores(qt_ref, kc_ref, 0, NSA_PRE_TILE)
        for t0 in range(0, seq, NSA_PRE_TILE):
            scores = nxt
            if t0 + NSA_PRE_TILE < seq:
                nxt = _nsa_cmp_scores(qt_ref, kc_ref, t0 + NSA_PRE_TILE, NSA_PRE_TILE)
            _nsa_select(scores, vct_ref, g_ref, ovl_ref, sel_ref, ocmp_ref, t0, NSA_PRE_TILE)

    pad = jnp.zeros((LANES - HEAD_DIM - N_SLC, tq), BF16)
    sel_neg = sel_ref[:, pl.ds(q0, tq)]
    for h in range(NSA_HEADS):
        qaug_ref[h] = jnp.concatenate([qt_ref[0, _head_rows(h, HEAD_DIM), pl.ds(q0, tq)], sel_neg, pad], axis=0)

    def scores(i, k0, cols):
        k_ref = ks_ref if i < NSA_HEADS else kw_ref
        return _dot(k_ref[0, pl.ds(k0, tk), :], qaug_ref[i % NSA_HEADS, :, cols])

    def values(i, k0):
        vt_ref = vst_ref if i < NSA_HEADS else vwt_ref
        return vt_ref[0, :, pl.ds(k0, tk)]

    def mask(i, k0, diagonal, q_lo, width):
        d, off = _key_minus_query(tk, width, k0, q0 + q_lo)
        if diagonal:
            return d <= off
        return None if i < NSA_HEADS else d > off - WINDOW

    ls = _attend_t(2 * NSA_HEADS, tq, tk, qi, 0, scores, values, acc_ref, sbuf_ref, mask=mask,
                   n_early=NSA_HEADS, join=jnp.maximum(q0 - WINDOW, 0) // tk)
    for h in range(NSA_HEADS):
        rows = _head_rows(h, HEAD_DIM)
        out = ocmp_ref[rows, pl.ds(q0, tq)]
        for branch in (1, 2):
            i = (branch - 1) * NSA_HEADS + h
            gate = g_ref[0, 3 * h + branch:3 * h + branch + 1, pl.ds(q0, tq)]
            out = out + (gate * (1.0 / ls[i])) * acc_ref[_head_rows(i, HEAD_DIM), :]
        out_ref[rows, :] = out
    o_ref[0] = out_ref[...].T.astype(BF16)


def _nsa_attention(nq_t, kc, vc_t, kslc, vslc_t, kwin, vwin_t, gates_t, ovl):
    b, w, s = nq_t.shape
    tq, tk = NSA_Q_TILE, ATT_K_TILE
    seq = lambda i, j: (i, 0, 0)
    return pl.pallas_call(
        functools.partial(_nsa_kernel, tq=tq, tk=tk, seq=s),
        grid=(b, s // tq),
        in_specs=[
            pl.BlockSpec((1, w, s), seq),
            pl.BlockSpec((1, N_CMP, HEAD_DIM), seq),
            pl.BlockSpec((1, HEAD_DIM, N_CMP), seq),
            pl.BlockSpec((1, s, LANES), seq),
            pl.BlockSpec((1, HEAD_DIM, s), seq),
            pl.BlockSpec((1, s, LANES), seq),
            pl.BlockSpec((1, HEAD_DIM, s), seq),
            pl.BlockSpec((1, GATE_ROWS, s), seq),
            pl.BlockSpec((N_SLC, N_CMP), lambda i, j: (0, 0)),
        ],
        out_specs=pl.BlockSpec((1, tq, w), lambda i, j: (i, j, 0)),
        out_shape=jax.ShapeDtypeStruct((b, s, w), BF16),
        scratch_shapes=[pltpu.VMEM((N_SLC, s), BF16), pltpu.VMEM((w, s), F32),
                        pltpu.VMEM((NSA_HEADS, LANES, tq), BF16), pltpu.VMEM((2 * w, tq), F32),
                        pltpu.VMEM((w, tq), F32), pltpu.VMEM((QK_AHEAD, tk, tq), F32)],
        compiler_params=_cparams("parallel", "arbitrary"),
        name="nsa_attention",
    )(nq_t, kc, vc_t, kslc, vslc_t, kwin, vwin_t, gates_t, ovl)


def _out_ln_kernel(x_ref, on_ref, od_ref, of_ref, wn_ref, wd_ref, wf_ref, lng_ref, lnb_ref, o_ref):
    h = _dot(on_ref[...], wn_ref[...]) + _dot(od_ref[...], wd_ref[...]) + _dot(of_ref[...], wf_ref[...])
    o_ref[...] = _layer_norm(DEEPNORM_ALPHA * x_ref[...] + h, lng_ref[...], lnb_ref[...])


def _out_ln(x, o_nsa, o_diff, o_fox, wn, wd, wf, ln_g, ln_b):
    t = x.shape[0]
    tm = ROW_TILE
    row = lambda i: (i, 0)
    c2 = lambda i: (0, 0)
    return pl.pallas_call(
        _out_ln_kernel,
        grid=(t // tm,),
        in_specs=[
            pl.BlockSpec((tm, D_MODEL), row),
            pl.BlockSpec((tm, NSA_WIDTH), row),
            pl.BlockSpec((tm, DIFF_WIDTH), row),
            pl.BlockSpec((tm, FOX_WIDTH), row),
            pl.BlockSpec((NSA_WIDTH, D_MODEL), c2),
            pl.BlockSpec((DIFF_WIDTH, D_MODEL), c2),
            pl.BlockSpec((FOX_WIDTH, D_MODEL), c2),
            pl.BlockSpec((1, D_MODEL), c2),
            pl.BlockSpec((1, D_MODEL), c2),
        ],
        out_specs=pl.BlockSpec((tm, D_MODEL), row),
        out_shape=jax.ShapeDtypeStruct((t, D_MODEL), F32),
        compiler_params=_cparams("parallel"),
        name="out_ln",
    )(x, o_nsa, o_diff, o_fox, wn, wd, wf, ln_g, ln_b)


def _prep_ffn(wg, wu, wd):
    pad = D_FF_PAD - D_FF
    up = lambda w: jnp.pad(w, ((0, 0), (0, pad))).astype(BF16)
    return up(wg), up(wu), jnp.pad(wd, ((0, pad), (0, 0))).astype(BF16)


def _rot_cols(w, d):
    k, n = w.shape
    w = w.reshape(k, n // d, d)
    return jnp.concatenate([-w[..., d // 2:], w[..., :d // 2]], axis=-1).reshape(k, n)


def _rope_table(pos, d, signed=False):
    inv = ROPE_THETA ** (-jnp.arange(0, d, 2, dtype=F32) / d)
    ang = pos.astype(F32)[:, None] * inv[None, :]
    cos = jnp.concatenate([jnp.cos(ang), jnp.cos(ang)], axis=-1)
    sin = jnp.concatenate([-jnp.sin(ang) if signed else jnp.sin(ang), jnp.sin(ang)], axis=-1)
    return cos, sin


def _fox_grouped(w):
    groups = []
    for h0 in range(0, FOX_HEADS, FOX_GROUP_HEADS):
        cols = w[:, h0 * HEAD_DIM:min(h0 + FOX_GROUP_HEADS, FOX_HEADS) * HEAD_DIM]
        groups.append(jnp.pad(cols, ((0, 0), (0, MXU_COLS - cols.shape[1]))))
    return jnp.concatenate(groups, axis=1)


def _fox_placement():
    place = np.zeros((4, LANES, FOX_QK_WIDTH), np.float32)
    for h in range(FOX_HEADS):
        g, b0 = _fox_bias_lane(h)
        for j in range(3):
            place[j, h, g * MXU_COLS + b0 + j] = -1.0
            place[3, 0, g * MXU_COLS + b0 + 3 + j] = 1.0
    return jnp.asarray(place, dtype=BF16)


def _prep_in_proj(w_in, fox_b_f, s):
    o = 0
    nsa_q = w_in[:, o:o + NSA_WIDTH]; o += NSA_WIDTH
    kv = [w_in[:, o + i * HEAD_DIM:o + (i + 1) * HEAD_DIM] for i in range(6)]; o += 6 * HEAD_DIM
    k_cmp, v_cmp, k_slc, v_slc, k_win, v_win = kv
    nsa_g = w_in[:, o:o + N_NSA_GATES]; o += N_NSA_GATES
    diff_q = w_in[:, o:o + DIFF_WIDTH]; o += DIFF_WIDTH
    diff_k = w_in[:, o:o + DIFF_WIDTH]; o += DIFF_WIDTH
    diff_v = w_in[:, o:o + DIFF_WIDTH]; o += DIFF_WIDTH
    fox_q = w_in[:, o:o + FOX_WIDTH]; o += FOX_WIDTH
    fox_k = w_in[:, o:o + FOX_WIDTH]; o += FOX_WIDTH
    fox_v = w_in[:, o:o + FOX_WIDTH]; o += FOX_WIDTH
    fox_f = w_in[:, o:o + FOX_HEADS]
    zero = jnp.zeros((D_MODEL, LANES - HEAD_DIM), F32)

    wa = jnp.concatenate([k_slc, zero, k_win, zero, diff_k], axis=1)
    wb = jnp.concatenate([_fox_grouped(fox_k), k_cmp, v_cmp], axis=1)
    wc = jnp.concatenate([fox_f, jnp.zeros((D_MODEL, LANES - FOX_HEADS), F32)], axis=1)
    bc = jnp.concatenate([fox_b_f, jnp.zeros((LANES - FOX_HEADS,), F32)])[None, :]
    wd = jnp.concatenate([nsa_q, diff_q], axis=1).T
    we = jnp.concatenate([v_slc, v_win, diff_v, _fox_grouped(fox_q * (HEAD_DIM ** -0.5 * LOG2E)), fox_v], axis=1).T
    wf = jnp.concatenate([nsa_g, jnp.zeros((D_MODEL, GATE_ROWS - N_NSA_GATES), F32)], axis=1).T

    pos = jnp.arange(s, dtype=jnp.int32)
    c64, s64 = _rope_table(pos, HEAD_DIM, signed=True)
    c32, s32 = _rope_table(pos, DIFF_QK_DIM, signed=True)
    nsa_scale = HEAD_DIM ** -0.5 * LOG2E
    diff_scale = DIFF_QK_DIM ** -0.5 * LOG2E

    def tm_table(t64, t32):
        return jnp.concatenate([t64, t64, t64, t64, jnp.tile(t32, (1, DIFF_MAPS))], axis=1)

    def fm_table(t64, t32):
        return jnp.concatenate([jnp.tile(t64, (1, NSA_HEADS)) * nsa_scale,
                                jnp.tile(t32, (1, DIFF_MAPS)) * diff_scale], axis=1).T

    bf = lambda w: w.astype(BF16)
    return (bf(wa), bf(wb), bf(wc), bf(wd), bf(we), bf(wf),
            tm_table(c64, c32), tm_table(s64, s32), fm_table(c64, c32), fm_table(s64, s32), bc)


def _prep_compress(pos_k, pos_v, phi_k1, phi_k2, phi_v1, phi_v2):
    half = CMP_STRIDE * HEAD_DIM

    def pe(p):
        return jnp.broadcast_to(p.reshape(2, 1, half), (2, 8, half)).astype(BF16)

    block_end = jnp.arange(N_CMP, dtype=jnp.int32) * CMP_STRIDE + (CMP_BLOCK - 1)
    cos_c, sin_c = _rope_table(block_end, HEAD_DIM)
    return (pe(pos_k), pe(pos_v), phi_k1.reshape(2, half, CMP_HIDDEN).astype(BF16), phi_k2.astype(BF16),
            _rot_cols(phi_k2, HEAD_DIM).astype(BF16), phi_v1.reshape(2, half, CMP_HIDDEN).astype(BF16),
            phi_v2.T.astype(BF16), cos_c, sin_c)


def _overlap_matrix(s):
    c0 = np.arange(N_CMP) * CMP_STRIDE
    s0 = np.arange(N_SLC) * SLC_BLOCK
    ovl = (c0[None, :] < s0[:, None] + SLC_BLOCK) & (c0[None, :] + CMP_BLOCK > s0[:, None])
    ovl[:, (s - CMP_BLOCK) // CMP_STRIDE + 1:] = False
    return jnp.asarray(ovl.astype(np.float32), dtype=BF16)


def kernel(x, p, ln_g, ln_b, ffn1_w_gate, ffn1_w_up, ffn1_w_down, ffn2_w_gate, ffn2_w_up, ffn2_w_down, w_in, fox_b_f, nsa_pos_k, nsa_pos_v, nsa_phi_k1, nsa_phi_k2, nsa_phi_v1, nsa_phi_v2, diff_lambda, diff_subln_g, w_out, ple_w_gate, ple_b_gate, ple_w_proj):
    b, s, _ = x.shape
    assert s // SLC_BLOCK == N_SLC and (s - CMP_BLOCK) // CMP_STRIDE + 1 <= N_CMP
    t = b * s
    ovl = _overlap_matrix(s)
    tri = jnp.asarray(np.tril(np.ones((CUM_TILE, CUM_TILE), np.float32)), dtype=BF16)
    place = _fox_placement()
    x = x.reshape(t, D_MODEL)
    for i in range(DEPTH):
        lambda_init = 0.8 - 0.6 * math.exp(-0.3 * i)
        lng = ln_g[i][:, None, :]
        lnb = ln_b[i][:, None, :]
        x = _ffn_ln(x, *_prep_ffn(ffn1_w_gate[i], ffn1_w_up[i], ffn1_w_down[i]), lng[0], lnb[0])
        proj = _in_proj(x.reshape(b, s, D_MODEL), *_prep_in_proj(w_in[i], fox_b_f[i], s))
        kslc, kwin, dk, fk, kcmp, vcmp, fls, nq_t, dq_t, vslc_t, vwin_t, dv_t, fq_t, fv_t, gates_t = proj
        half = CMP_STRIDE * HEAD_DIM
        kc, vc_t = _compress(kcmp.reshape(b, s // CMP_STRIDE, half), vcmp.reshape(b, s // CMP_STRIDE, half),
                             *_prep_compress(nsa_pos_k[i], nsa_pos_v[i], nsa_phi_k1[i], nsa_phi_k2[i],
                                             nsa_phi_v1[i], nsa_phi_v2[i]))
        o_nsa = _nsa_attention(nq_t, kc, vc_t, kslc, vslc_t, kwin, vwin_t, gates_t, ovl)
        o_diff = _diff_attention(dq_t, dk, dv_t, diff_lambda[i], jnp.tile(diff_subln_g[i], DIFF_HEADS)[None, :],
                                 lambda_init)
        o_fox = _fox_attention(fq_t, fk, fv_t, fls, tri, place)
        wo = w_out[i].astype(BF16)
        x = _out_ln(x, o_nsa.reshape(t, NSA_WIDTH), o_diff.reshape(t, DIFF_WIDTH), o_fox.reshape(t, FOX_WIDTH),
                    wo[:NSA_WIDTH], wo[NSA_WIDTH:NSA_WIDTH + DIFF_WIDTH], wo[NSA_WIDTH + DIFF_WIDTH:],
                    lng[1], lnb[1])
        ple = (p.reshape(DEPTH, t, PLE_DIM), i, ple_w_gate[i].astype(BF16), ple_b_gate[i][None, :],
               ple_w_proj[i].astype(BF16))
        x = _ffn_ln(x, *_prep_ffn(ffn2_w_gate[i], ffn2_w_up[i], ffn2_w_down[i]), lng[2], lnb[2], ple=ple)
    return x.reshape(b, s, D_MODEL)
```

```python
import functools
import math

import numpy as np
import jax
import jax.numpy as jnp
from jax import lax
from jax.experimental import pallas as pl
from jax.experimental.pallas import tpu as pltpu

F32 = jnp.float32
BF16 = jnp.bfloat16

D_MODEL = 1024
HEAD_DIM = 64
NSA_HEADS = 4
DIFF_HEADS = 4
FOX_HEADS = 8
NSA_WIDTH = NSA_HEADS * HEAD_DIM
DIFF_WIDTH = DIFF_HEADS * HEAD_DIM
FOX_WIDTH = FOX_HEADS * HEAD_DIM
CMP_BLOCK = 32
CMP_STRIDE = 16
CMP_HIDDEN = 256
SLC_BLOCK = 64
SLC_TOPK = 16
WINDOW = 512
DIFF_QK_DIM = HEAD_DIM // 2
DIFF_MAPS = 2 * DIFF_HEADS
D_FF = 2752
PLE_DIM = 256
ROPE_THETA = 10000.0
LN_EPS = 1e-5
NEG_BIG = -1e30
DEPTH = 2
DEEPNORM_ALPHA = (2.0 * DEPTH) ** 0.25
LOG2E = math.log2(math.e)

LANES = 128
MXU_COLS = 256
BF16_SUBLANES = 16
VMEM_LIMIT_BYTES = 56 * 1024 * 1024

D_FF_PAD = ((D_FF + MXU_COLS - 1) // MXU_COLS) * MXU_COLS
FF_CHUNK = 2 * MXU_COLS
FF_BOUNDS = tuple((c0, min(c0 + FF_CHUNK, D_FF_PAD)) for c0 in range(0, D_FF_PAD, FF_CHUNK))

ROW_TILE = 512
ATT_Q_TILE = 512
ATT_K_TILE = 256
NSA_Q_TILE = 512
NSA_PRE_TILE = 512
CUM_TILE = 256
QK_AHEAD = 4

TM_ROPE_COLS = 2 * LANES + DIFF_WIDTH
TM_PLAIN_COLS = FOX_WIDTH + 2 * HEAD_DIM
FM_ROPE_ROWS = NSA_WIDTH + DIFF_WIDTH
FM_PLAIN_ROWS = 2 * HEAD_DIM + DIFF_WIDTH + 2 * FOX_WIDTH
FOX_GROUP_HEADS = 3
FOX_QK_WIDTH = -(-FOX_HEADS // FOX_GROUP_HEADS) * MXU_COLS
N_NSA_GATES = 3 * NSA_HEADS
GATE_ROWS = BF16_SUBLANES
N_CMP = 128
N_SLC = 32


def _cparams(*sem):
    return pltpu.CompilerParams(dimension_semantics=sem, vmem_limit_bytes=VMEM_LIMIT_BYTES)


def _dot(a, b):
    return jnp.dot(a, b, preferred_element_type=F32)


def _dot_nt(a, b):
    return lax.dot_general(a, b, (((1,), (1,)), ((), ())), preferred_element_type=F32)


def _sigmoid(z):
    return 1.0 / (1.0 + jnp.exp(-z))


def _layer_norm(y, g, b):
    mu = jnp.mean(y, axis=-1, keepdims=True)
    yc = y - mu
    var = jnp.mean(yc * yc, axis=-1, keepdims=True)
    return yc * lax.rsqrt(var + LN_EPS) * g + b


def _ffn_kernel(*refs, with_ple):
    if with_ple:
        (x_ref, wg_ref, wu_ref, wd_ref, lng_ref, lnb_ref, p_ref, pwg_ref, pbg_ref, pwp_ref,
         o_ref, xb_ref) = refs
    else:
        x_ref, wg_ref, wu_ref, wd_ref, lng_ref, lnb_ref, o_ref, xb_ref = refs
    xb_ref[...] = x_ref[...].astype(BF16)
    xb = xb_ref[...]

    def gate_up(c):
        c0, c1 = FF_BOUNDS[c]
        return _dot(xb, wg_ref[:, c0:c1]), _dot(xb, wu_ref[:, c0:c1])

    nxt = gate_up(0)
    acc = None
    for c, (c0, c1) in enumerate(FF_BOUNDS):
        g, u = nxt
        if c + 1 < len(FF_BOUNDS):
            nxt = gate_up(c + 1)
        h = (g * _sigmoid(g)) * u
        part = _dot(h.astype(BF16), wd_ref[c0:c1, :])
        acc = part if acc is None else acc + part
    y = DEEPNORM_ALPHA * x_ref[...] + 0.5 * acc
    z = _layer_norm(y, lng_ref[...], lnb_ref[...])
    if with_ple:
        gate = _sigmoid(_dot(z.astype(BF16), pwg_ref[...]) + pbg_ref[...])
        z = z + gate * _dot(p_ref[0].astype(BF16), pwp_ref[...])
    o_ref[...] = z


def _ffn_ln(x, wg, wu, wd, ln_g, ln_b, ple=None):
    t = x.shape[0]
    tm = ROW_TILE
    row = lambda i: (i, 0)
    full2 = lambda i: (0, 0)
    in_specs = [
        pl.BlockSpec((tm, D_MODEL), row),
        pl.BlockSpec((D_MODEL, D_FF_PAD), full2),
        pl.BlockSpec((D_MODEL, D_FF_PAD), full2),
        pl.BlockSpec((D_FF_PAD, D_MODEL), full2),
        pl.BlockSpec((1, D_MODEL), full2),
        pl.BlockSpec((1, D_MODEL), full2),
    ]
    args = [x, wg, wu, wd, ln_g, ln_b]
    if ple is not None:
        p, layer, pwg, pbg, pwp = ple
        in_specs += [
            pl.BlockSpec((1, tm, PLE_DIM), lambda i: (layer, i, 0)),
            pl.BlockSpec((D_MODEL, D_MODEL), full2),
            pl.BlockSpec((1, D_MODEL), full2),
            pl.BlockSpec((PLE_DIM, D_MODEL), full2),
        ]
        args += [p, pwg, pbg, pwp]
    return pl.pallas_call(
        functools.partial(_ffn_kernel, with_ple=ple is not None),
        grid=(t // tm,),
        in_specs=in_specs,
        out_specs=pl.BlockSpec((tm, D_MODEL), row),
        out_shape=jax.ShapeDtypeStruct((t, D_MODEL), F32),
        scratch_shapes=[pltpu.VMEM((tm, D_MODEL), BF16)],
        compiler_params=_cparams("parallel"),
        name="ffn_ln_ple" if ple is not None else "ffn_ln",
    )(*args)


def _swap_halves_lanes(x, d):
    w = x.shape[1]
    lane = lax.broadcasted_iota(jnp.int32, (1, w), 1)
    return jnp.where(lane % d < d // 2, pltpu.roll(x, w - d // 2, 1), pltpu.roll(x, d // 2, 1))


def _swap_halves_rows(x, d):
    parts = []
    for r0 in range(0, x.shape[0], d):
        parts += [x[r0 + d // 2:r0 + d], x[r0:r0 + d // 2]]
    return jnp.concatenate(parts, axis=0)


def _in_proj_kernel(x_ref, wa_ref, wb_ref, wc_ref, wd_ref, we_ref, wf_ref,
                    cosa_ref, sina_ref, cosd_ref, sind_ref, bc_ref,
                    kslc_ref, kwin_ref, dk_ref, fk_ref, kcmp_ref, vcmp_ref, fls_ref,
                    nq_ref, dq_ref, vslc_ref, vwin_ref, dv_ref, fq_ref, fv_ref, gate_ref, *, tm):
    xb = x_ref[0].astype(BF16)
    pa = _dot(xb, wa_ref[...])
    swapped = jnp.concatenate([_swap_halves_lanes(pa[:, 0:2 * LANES], HEAD_DIM),
                               _swap_halves_lanes(pa[:, 2 * LANES:], DIFF_QK_DIM)], axis=1)
    ra = pa * cosa_ref[...] + swapped * sina_ref[...]
    pos = pl.program_id(1) * tm + lax.broadcasted_iota(jnp.int32, (tm, 1), 0)
    lane = lax.broadcasted_iota(jnp.int32, (1, LANES), 1)
    block_id = jnp.where(lane - HEAD_DIM == pos // SLC_BLOCK, 1.0, 0.0)
    kslc_ref[0] = (ra[:, 0:LANES] + block_id).astype(BF16)
    kwin_ref[0] = ra[:, LANES:2 * LANES].astype(BF16)
    dk_ref[0] = ra[:, 2 * LANES:].astype(BF16)
    pb = _dot(xb, wb_ref[...]).astype(BF16)
    fk_ref[0] = pb[:, 0:FOX_WIDTH]
    kcmp_ref[0] = pb[:, FOX_WIDTH:FOX_WIDTH + HEAD_DIM]
    vcmp_ref[0] = pb[:, FOX_WIDTH + HEAD_DIM:]
    z = _dot(xb, wc_ref[...]) + bc_ref[...]
    fls_ref[0] = jnp.minimum(z, 0.0) - jnp.log(1.0 + jnp.exp(-jnp.abs(z)))

    pd = _dot_nt(wd_ref[...], xb)
    swapped = jnp.concatenate([_swap_halves_rows(pd[0:NSA_WIDTH], HEAD_DIM),
                               _swap_halves_rows(pd[NSA_WIDTH:], DIFF_QK_DIM)], axis=0)
    rd = pd * cosd_ref[...] + swapped * sind_ref[...]
    nq_ref[0] = rd[0:NSA_WIDTH].astype(BF16)
    dq_ref[0] = rd[NSA_WIDTH:].astype(BF16)
    pe = _dot_nt(we_ref[...], xb).astype(BF16)
    vslc_ref[0] = pe[0:HEAD_DIM]
    vwin_ref[0] = pe[HEAD_DIM:2 * HEAD_DIM]
    off = 2 * HEAD_DIM
    dv_ref[0] = pe[off:off + DIFF_WIDTH]
    off += DIFF_WIDTH
    fq_ref[0] = pe[off:off + FOX_WIDTH]
    off += FOX_WIDTH
    fv_ref[0] = pe[off:off + FOX_WIDTH]
    gate_ref[0] = _sigmoid(_dot_nt(wf_ref[...], xb))


def _in_proj(x, wa, wb, wc, wd, we, wf, cosa, sina, cosd, sind, bc):
    b, s, _ = x.shape
    tm = ROW_TILE
    grid = (b, s // tm)
    w2 = lambda i, j: (0, 0)
    tok = lambda i, j: (i, j, 0)
    fm = lambda i, j: (i, 0, j)
    in_specs = [
        pl.BlockSpec((1, tm, D_MODEL), tok),
        pl.BlockSpec((D_MODEL, TM_ROPE_COLS), w2),
        pl.BlockSpec((D_MODEL, TM_PLAIN_COLS), w2),
        pl.BlockSpec((D_MODEL, LANES), w2),
        pl.BlockSpec((FM_ROPE_ROWS, D_MODEL), w2),
        pl.BlockSpec((FM_PLAIN_ROWS, D_MODEL), w2),
        pl.BlockSpec((GATE_ROWS, D_MODEL), w2),
        pl.BlockSpec((tm, TM_ROPE_COLS), lambda i, j: (j, 0)),
        pl.BlockSpec((tm, TM_ROPE_COLS), lambda i, j: (j, 0)),
        pl.BlockSpec((FM_ROPE_ROWS, tm), lambda i, j: (0, j)),
        pl.BlockSpec((FM_ROPE_ROWS, tm), lambda i, j: (0, j)),
        pl.BlockSpec((1, LANES), w2),
    ]

    def tok_out(width, dtype=BF16):
        return pl.BlockSpec((1, tm, width), tok), jax.ShapeDtypeStruct((b, s, width), dtype)

    def fm_out(rows, dtype=BF16):
        return pl.BlockSpec((1, rows, tm), fm), jax.ShapeDtypeStruct((b, rows, s), dtype)

    outs = [
        tok_out(LANES), tok_out(LANES), tok_out(DIFF_WIDTH), tok_out(FOX_WIDTH),
        tok_out(HEAD_DIM), tok_out(HEAD_DIM), tok_out(LANES, F32),
        fm_out(NSA_WIDTH), fm_out(DIFF_WIDTH),
        fm_out(HEAD_DIM), fm_out(HEAD_DIM), fm_out(DIFF_WIDTH),
        fm_out(FOX_WIDTH), fm_out(FOX_WIDTH), fm_out(GATE_ROWS, F32),
    ]
    return pl.pallas_call(
        functools.partial(_in_proj_kernel, tm=tm),
        grid=grid,
        in_specs=in_specs,
        out_specs=[o[0] for o in outs],
        out_shape=[o[1] for o in outs],
        compiler_params=_cparams("parallel", "parallel"),
        name="in_proj",
    )(x, wa, wb, wc, wd, we, wf, cosa, sina, cosd, sind, bc)


def _gelu_tanh(x):
    return 0.5 * x * (1.0 + jnp.tanh(math.sqrt(2.0 / math.pi) * (x + 0.044715 * (x * x * x))))


def _compress_kernel(k16_ref, v16_ref, pek_ref, pev_ref, wk1_ref, wk2_ref, wk2r_ref, wv1_ref, wv2t_ref,
                     cos_ref, sin_ref, kc_ref, vct_ref):
    def hidden(x16, pe_ref, w1_ref):
        top = _dot(x16, w1_ref[0])
        bot = _dot(x16, w1_ref[1])
        bias = _dot(pe_ref[0], w1_ref[0]) + _dot(pe_ref[1], w1_ref[1])
        bot = pltpu.roll(bot, N_CMP - 1, 0)
        return _gelu_tanh(top + bot + bias[0:1, :]).astype(BF16)

    hk = hidden(k16_ref[0], pek_ref, wk1_ref)
    kc = _dot(hk, wk2_ref[...]) * cos_ref[...] + _dot(hk, wk2r_ref[...]) * sin_ref[...]
    kc_ref[0] = kc.astype(BF16)
    hv = hidden(v16_ref[0], pev_ref, wv1_ref)
    vct_ref[0] = _dot_nt(wv2t_ref[...], hv).astype(BF16)


def _compress(k16, v16, pek, pev, wk1, wk2, wk2r, wv1, wv2t, cos_c, sin_c):
    b = k16.shape[0]
    half = CMP_STRIDE * HEAD_DIM
    bat = lambda i: (i, 0, 0)
    c2 = lambda i: (0, 0)
    c3 = lambda i: (0, 0, 0)
    in_specs = [
        pl.BlockSpec((1, N_CMP, half), bat),
        pl.BlockSpec((1, N_CMP, half), bat),
        pl.BlockSpec((2, 8, half), c3),
        pl.BlockSpec((2, 8, half), c3),
        pl.BlockSpec((2, half, CMP_HIDDEN), c3),
        pl.BlockSpec((CMP_HIDDEN, HEAD_DIM), c2),
        pl.BlockSpec((CMP_HIDDEN, HEAD_DIM), c2),
        pl.BlockSpec((2, half, CMP_HIDDEN), c3),
        pl.BlockSpec((HEAD_DIM, CMP_HIDDEN), c2),
        pl.BlockSpec((N_CMP, HEAD_DIM), c2),
        pl.BlockSpec((N_CMP, HEAD_DIM), c2),
    ]
    return pl.pallas_call(
        _compress_kernel,
        grid=(b,),
        in_specs=in_specs,
        out_specs=[pl.BlockSpec((1, N_CMP, HEAD_DIM), bat), pl.BlockSpec((1, HEAD_DIM, N_CMP), bat)],
        out_shape=[jax.ShapeDtypeStruct((b, N_CMP, HEAD_DIM), BF16), jax.ShapeDtypeStruct((b, HEAD_DIM, N_CMP), BF16)],
        compiler_params=_cparams("parallel"),
        name="nsa_compress",
    )(k16, v16, pek, pev, wk1, wk2, wk2r, wv1, wv2t, cos_c, sin_c)


def _key_minus_query(tk, tq, k0, q0):
    d = lax.broadcasted_iota(jnp.int32, (tk, tq), 0) - lax.broadcasted_iota(jnp.int32, (tk, tq), 1)
    return d, q0 - k0


def _causal_mask(tk, q0):
    def mask(i, k0, diagonal, q_lo, width):
        if not diagonal:
            return None
        d, off = _key_minus_query(tk, width, k0, q0 + q_lo)
        return d <= off
    return mask


def _head_rows(h, width):
    return slice(h * width, (h + 1) * width)


def _attend_t(n, tq, tk, qi, lo, scores, values, acc_ref, sbuf_ref, *, mask, n_early=None, join=None):
    n_early = n if n_early is None else n_early
    ahead = min(QK_AHEAD, n_early)
    per_tile = tq // tk
    full = slice(0, tq)
    acc_ref[...] = jnp.zeros(acc_ref.shape, F32)
    k_lo = pl.multiple_of(lo * tk, tk)
    for i in range(ahead):
        sbuf_ref[i] = scores(i, k_lo, full)
    ones = jnp.ones((BF16_SUBLANES, tk), BF16)

    def step(c, heads, diag, ms, ls):
        diagonal = diag is not None
        q_lo = diag * tk if diagonal else 0
        cols = slice(q_lo, tq)
        nxt = slice(q_lo + tk, tq) if diagonal else full
        k0 = pl.multiple_of(c * tk, tk)
        k1 = pl.multiple_of(c * tk + tk, tk)
        ms, ls = list(ms), list(ls)
        pend = [sbuf_ref[i, :, cols] for i in range(ahead)]
        for i in range(heads):
            s = pend.pop(0)
            j = i + ahead
            if j < heads:
                pend.append(scores(j, k0, cols))
            elif nxt.start < tq:
                sbuf_ref[j - heads, :, nxt] = scores(j - heads, k1, nxt)
            mk = mask(i, k0, diagonal, q_lo, tq - q_lo)
            if mk is not None:
                s = jnp.where(mk, s, NEG_BIG)
            m_old, l_old = ms[i][:, cols], ls[i][:, cols]
            m_new = jnp.maximum(m_old, jnp.max(s, axis=0, keepdims=True))
            a = jnp.exp2(m_old - m_new)
            p = jnp.exp2(s - m_new)
            pv = _dot(jnp.concatenate([values(i, k0), ones], axis=0), p.astype(BF16))
            rows = _head_rows(i, HEAD_DIM)
            acc_ref[rows, cols] = a * acc_ref[rows, cols] + pv[0:HEAD_DIM]
            l_new = a * l_old + pv[HEAD_DIM:HEAD_DIM + 1]
            if q_lo:
                m_new = jnp.concatenate([ms[i][:, 0:q_lo], m_new], axis=1)
                l_new = jnp.concatenate([ls[i][:, 0:q_lo], l_new], axis=1)
            ms[i], ls[i] = m_new, l_new
        return tuple(ms), tuple(ls)

    def init(count):
        return (tuple(jnp.full((1, tq), NEG_BIG, F32) for _ in range(count)),
                tuple(jnp.zeros((1, tq), F32) for _ in range(count)))

    first_diag = qi * per_tile
    ms, ls = init(n_early)
    if n_early < n:
        ms, ls = lax.fori_loop(lo, join, lambda c, carry: step(c, n_early, None, *carry), (ms, ls))
        late = init(n - n_early)
        ms, ls, lo = ms + late[0], ls + late[1], join
    ms, ls = lax.fori_loop(lo, first_diag, lambda c, carry: step(c, n, None, *carry), (ms, ls))
    for d in range(per_tile):
        ms, ls = step(first_diag + d, n, d, ms, ls)
    return ls


def _pad_queries(q_t, qpad_ref, n, width):
    per = MXU_COLS // width
    row = lax.broadcasted_iota(jnp.int32, (MXU_COLS, q_t.shape[1]), 0)
    for i in range(n):
        g, r = divmod(i, per)
        qg = q_t[g * MXU_COLS:(g + 1) * MXU_COLS]
        qpad_ref[i] = jnp.where((row >= r * width) & (row < (r + 1) * width), qg, jnp.zeros_like(qg))


def _split3(x):
    hi = x.astype(BF16)
    r1 = x - hi.astype(F32)
    mid = r1.astype(BF16)
    lo = (r1 - mid.astype(F32)).astype(BF16)
    return hi, mid, lo


def _fox_kernel(qt_ref, k_ref, vt_ref, fls_ref, tri_ref, place_ref, o_ref,
                kaug_ref, crow_ref, qpad_ref, acc_ref, sbuf_ref, *, tq, tk, seq):
    qi = pl.program_id(1)
    q0 = pl.multiple_of(qi * tq, tq)
    n_q = FOX_GROUP_HEADS * HEAD_DIM
    n_b = MXU_COLS - n_q
    n_groups = FOX_QK_WIDTH // MXU_COLS

    @pl.when(qi == 0)
    def _():
        carry = jnp.zeros((1, LANES), F32)
        for i in range(seq // CUM_TILE):
            rows = slice(i * CUM_TILE, (i + 1) * CUM_TILE)
            c = carry + sum(_dot(tri_ref[...], part) for part in _split3(fls_ref[0, rows, :]))
            carry = c[CUM_TILE - 1:CUM_TILE, :]
            c = c * LOG2E
            crow_ref[:, rows] = c.T[0:FOX_HEADS, :]
            hi, mid, lo = _split3(c)
            bias = (_dot(hi, place_ref[0]) + _dot(mid, place_ref[1]) + _dot(lo, place_ref[2])
                    + place_ref[3, 0:1, :].astype(F32))
            k = k_ref[0, rows, :].astype(F32)
            pieces = []
            for g in range(n_groups):
                kg = k[:, g * n_q:min((g + 1) * n_q, FOX_WIDTH)]
                pieces.append(kg)
                if kg.shape[1] < n_q:
                    pieces.append(jnp.zeros((CUM_TILE, n_q - kg.shape[1]), F32))
                pieces.append(bias[:, g * n_b:(g + 1) * n_b])
            kaug_ref[rows, :] = jnp.concatenate(pieces, axis=1).astype(BF16)

    brow = lax.broadcasted_iota(jnp.int32, (n_b, 1), 0)
    for h in range(FOX_HEADS):
        r = h % FOX_GROUP_HEADS
        bias = jnp.where((brow >= 8 * r) & (brow < 8 * r + 3), 1.0, 0.0)
        for j, part in enumerate(_split3(crow_ref[h:h + 1, pl.ds(q0, tq)])):
            bias = jnp.where(brow == 8 * r + 3 + j, part.astype(F32), bias)
        zero = lambda heads: [jnp.zeros((heads * HEAD_DIM, tq), BF16)] if heads else []
        qpad_ref[h] = jnp.concatenate(zero(r) + [qt_ref[0, _head_rows(h, HEAD_DIM), :]]
                                      + zero(FOX_GROUP_HEADS - 1 - r) + [bias.astype(BF16)], axis=0)

    ls = _attend_t(
        FOX_HEADS, tq, tk, qi, 0,
        lambda h, k0, cols: _dot(kaug_ref[pl.ds(k0, tk), _head_rows(h // FOX_GROUP_HEADS, MXU_COLS)],
                                 qpad_ref[h, :, cols]),
        lambda h, k0: vt_ref[0, _head_rows(h, HEAD_DIM), pl.ds(k0, tk)],
        acc_ref, sbuf_ref, mask=_causal_mask(tk, q0))
    for h in range(FOX_HEADS):
        rows = _head_rows(h, HEAD_DIM)
        acc_ref[rows, :] = acc_ref[rows, :] * (1.0 / ls[h])
    o_ref[0] = acc_ref[...].T.astype(BF16)


def _fox_attention(fq_t, fk, fv_t, fls, tri, place):
    b, w, s = fv_t.shape
    wk = FOX_QK_WIDTH
    tq, tk = ATT_Q_TILE, ATT_K_TILE
    return pl.pallas_call(
        functools.partial(_fox_kernel, tq=tq, tk=tk, seq=s),
        grid=(b, s // tq),
        in_specs=[
            pl.BlockSpec((1, w, tq), lambda i, j: (i, 0, j)),
            pl.BlockSpec((1, s, w), lambda i, j: (i, 0, 0)),
            pl.BlockSpec((1, w, s), lambda i, j: (i, 0, 0)),
            pl.BlockSpec((1, s, LANES), lambda i, j: (i, 0, 0)),
            pl.BlockSpec((CUM_TILE, CUM_TILE), lambda i, j: (0, 0)),
            pl.BlockSpec(place.shape, lambda i, j: (0, 0, 0)),
        ],
        out_specs=pl.BlockSpec((1, tq, w), lambda i, j: (i, j, 0)),
        out_shape=jax.ShapeDtypeStruct((b, s, w), BF16),
        scratch_shapes=[pltpu.VMEM((s, wk), BF16), pltpu.VMEM((FOX_HEADS, s), F32),
                        pltpu.VMEM((FOX_HEADS, MXU_COLS, tq), BF16), pltpu.VMEM((w, tq), F32),
                        pltpu.VMEM((QK_AHEAD, tk, tq), F32)],
        compiler_params=_cparams("parallel", "arbitrary"),
        name="fox_attention",
    )(fq_t, fk, fv_t, fls, tri, place)


def _diff_kernel(qt_ref, k_ref, vt_ref, lam_ref, g_ref, o_ref, qpad_ref, acc_ref, sbuf_ref, *, tq, tk, lambda_init):
    qi = pl.program_id(1)
    _pad_queries(qt_ref[0], qpad_ref, DIFF_MAPS, DIFF_QK_DIM)
    ls = _attend_t(
        DIFF_MAPS, tq, tk, qi, 0,
        lambda i, k0, cols: _dot(k_ref[0, pl.ds(k0, tk), :], qpad_ref[i, :, cols]),
        lambda i, k0: vt_ref[0, _head_rows(i // 2, HEAD_DIM), pl.ds(k0, tk)],
        acc_ref, sbuf_ref, mask=_causal_mask(tk, qi * tq))
    lp = lam_ref[...]
    lam = (jnp.exp(jnp.sum(lp[0:1] * lp[1:2], axis=1, keepdims=True))
           - jnp.exp(jnp.sum(lp[2:3] * lp[3:4], axis=1, keepdims=True)) + lambda_init)
    heads = []
    for h in range(DIFF_HEADS):
        o1 = acc_ref[_head_rows(2 * h, HEAD_DIM), :] * (1.0 / ls[2 * h])
        o2 = acc_ref[_head_rows(2 * h + 1, HEAD_DIM), :] * (1.0 / ls[2 * h + 1])
        o = o1 - lam * o2
        o = o * lax.rsqrt(jnp.mean(o * o, axis=0, keepdims=True) + LN_EPS)
        heads.append(o * (1.0 - lambda_init))
    o_t = jnp.concatenate(heads, axis=0)
    o_ref[0] = (o_t.T * g_ref[...]).astype(BF16)


def _diff_attention(dq_t, dk, dv_t, lam_params, subln_g, lambda_init):
    b, w, s = dq_t.shape
    tq, tk = ATT_Q_TILE, ATT_K_TILE
    return pl.pallas_call(
        functools.partial(_diff_kernel, tq=tq, tk=tk, lambda_init=lambda_init),
        grid=(b, s // tq),
        in_specs=[
            pl.BlockSpec((1, w, tq), lambda i, j: (i, 0, j)),
            pl.BlockSpec((1, s, w), lambda i, j: (i, 0, 0)),
            pl.BlockSpec((1, w, s), lambda i, j: (i, 0, 0)),
            pl.BlockSpec((4, DIFF_QK_DIM), lambda i, j: (0, 0)),
            pl.BlockSpec((1, w), lambda i, j: (0, 0)),
        ],
        out_specs=pl.BlockSpec((1, tq, w), lambda i, j: (i, j, 0)),
        out_shape=jax.ShapeDtypeStruct((b, s, w), BF16),
        scratch_shapes=[pltpu.VMEM((DIFF_MAPS, MXU_COLS, tq), BF16), pltpu.VMEM((DIFF_MAPS * HEAD_DIM, tq), F32),
                        pltpu.VMEM((QK_AHEAD, tk, tq), F32)],
        compiler_params=_cparams("parallel", "arbitrary"),
        name="diff_attention",
    )(dq_t, dk, dv_t, lam_params, subln_g)


def _nsa_cmp_scores(qt_ref, kc_ref, t0, width):
    cols = slice(t0, t0 + width)
    return [_dot(kc_ref[0], qt_ref[0, _head_rows(h, HEAD_DIM), cols]) for h in range(NSA_HEADS)]


def _nsa_select(scores, vct_ref, g_ref, ovl_ref, sel_ref, ocmp_ref, t0, width):
    cols = slice(t0, t0 + width)
    t_q = t0 + lax.broadcasted_iota(jnp.int32, (1, width), 1)
    block_end = CMP_STRIDE * lax.broadcasted_iota(jnp.int32, (N_CMP, 1), 0) + (CMP_BLOCK - 1)
    vis = block_end <= t_q
    any_vis = t_q >= CMP_BLOCK - 1
    lhs = jnp.concatenate([vct_ref[0], ovl_ref[...], jnp.ones((BF16_SUBLANES, N_CMP), BF16)], axis=0)
    imp = jnp.zeros((N_SLC, width), F32)
    for h in range(NSA_HEADS):
        rows = _head_rows(h, HEAD_DIM)
        s = jnp.where(vis, scores[h], NEG_BIG)
        p = jnp.exp2(s - jnp.max(s, axis=0, keepdims=True))
        r = _dot(lhs, p.astype(BF16))
        l = r[HEAD_DIM + N_SLC:HEAD_DIM + N_SLC + 1]
        scale = jnp.where(any_vis, 1.0 / l, 0.0)
        ocmp_ref[rows, cols] = (g_ref[0, 3 * h:3 * h + 1, cols] * scale) * r[0:HEAD_DIM]
        imp = imp + scale * r[HEAD_DIM:HEAD_DIM + N_SLC]
    j_idx = lax.broadcasted_iota(jnp.int32, (N_SLC, 1), 0)
    blk_t = t_q // SLC_BLOCK
    forced = (j_idx == 0) | (j_idx == blk_t) | (j_idx == blk_t - 1)
    valid = j_idx * SLC_BLOCK <= t_q
    score = jnp.where(forced, 1e9, jnp.where(valid, imp, -1.0))
    groups = [score[g * 8:(g + 1) * 8] for g in range(N_SLC // 8)]
    ranks = [jnp.zeros((8, width), F32) for _ in groups]
    row = lax.broadcasted_iota(jnp.int32, (8, 1), 0)
    for i in range(N_SLC):
        gi, ri = divmod(i, 8)
        si = groups[gi][ri:ri + 1, :]
        for g, sg in enumerate(groups):
            if g < gi:
                ranks[g] = ranks[g] + jnp.where(si > sg, 1.0, 0.0)
            elif g > gi:
                ranks[g] = ranks[g] + jnp.where(si >= sg, 1.0, 0.0)
            else:
                tie = jnp.where(row > ri, 1.0, 0.0)
                ranks[g] = ranks[g] + jnp.where(si > sg, 1.0, 0.0) + jnp.where(si == sg, tie, 0.0)
    rank = jnp.concatenate(ranks, axis=0)
    sel_ref[:, cols] = jnp.where(rank < float(SLC_TOPK), 0.0, NEG_BIG).astype(BF16)


def _nsa_kernel(qt_ref, kc_ref, vct_ref, ks_ref, vst_ref, kw_ref, vwt_ref, g_ref, ovl_ref,
                o_ref, sel_ref, ocmp_ref, qaug_ref, acc_ref, out_ref, sbuf_ref, *, tq, tk, seq):
    qi = pl.program_id(1)
    q0 = pl.multiple_of(qi * tq, tq)

    @pl.when(qi == 0)
    def _():
        nxt = _nsa_cmp_scores(qt_ref, kc_ref, 0, NSA_PRE_TILE)
        for t0 in range(0, seq, NSA_PRE_TILE):
            scores = nxt
            if t0 + NSA_PRE_TILE < seq:
                nxt = _nsa_cmp_scores(qt_ref, kc_ref, t0 + NSA_PRE_TILE, NSA_PRE_TILE)
            _nsa_select(scores, vct_ref, g_ref, ovl_ref, sel_ref, ocmp_ref, t0, NSA_PRE_TILE)

    pad = jnp.zeros((LANES - HEAD_DIM - N_SLC, tq), BF16)
    sel_neg = sel_ref[:, pl.ds(q0, tq)]
    for h in range(NSA_HEADS):
        qaug_ref[h] = jnp.concatenate([qt_ref[0, _head_rows(h, HEAD_DIM), pl.ds(q0, tq)], sel_neg, pad], axis=0)

    def scores(i, k0, cols):
        k_ref = ks_ref if i < NSA_HEADS else kw_ref
        return _dot(k_ref[0, pl.ds(k0, tk), :], qaug_ref[i % NSA_HEADS, :, cols])

    def values(i, k0):
        vt_ref = vst_ref if i < NSA_HEADS else vwt_ref
        return vt_ref[0, :, pl.ds(k0, tk)]

    def mask(i, k0, diagonal, q_lo, width):
        d, off = _key_minus_query(tk, width, k0, q0 + q_lo)
        if diagonal:
            return d <= off
        return None if i < NSA_HEADS else d > off - WINDOW

    ls = _attend_t(2 * NSA_HEADS, tq, tk, qi, 0, scores, values, acc_ref, sbuf_ref, mask=mask,
                   n_early=NSA_HEADS, join=jnp.maximum(q0 - WINDOW, 0) // tk)
    for h in range(NSA_HEADS):
        rows = _head_rows(h, HEAD_DIM)
        out = ocmp_ref[rows, pl.ds(q0, tq)]
        for branch in (1, 2):
            i = (branch - 1) * NSA_HEADS + h
            gate = g_ref[0, 3 * h + branch:3 * h + branch + 1, pl.ds(q0, tq)]
            out = out + (gate * (1.0 / ls[i])) * acc_ref[_head_rows(i, HEAD_DIM), :]
        out_ref[rows, :] = out
    o_ref[0] = out_ref[...].T.astype(BF16)


def _nsa_attention(nq_t, kc, vc_t, kslc, vslc_t, kwin, vwin_t, gates_t, ovl):
    b, w, s = nq_t.shape
    tq, tk = NSA_Q_TILE, ATT_K_TILE
    seq = lambda i, j: (i, 0, 0)
    return pl.pallas_call(
        functools.partial(_nsa_kernel, tq=tq, tk=tk, seq=s),
        grid=(b, s // tq),
        in_specs=[
            pl.BlockSpec((1, w, s), seq),
            pl.BlockSpec((1, N_CMP, HEAD_DIM), seq),
            pl.BlockSpec((1, HEAD_DIM, N_CMP), seq),
            pl.BlockSpec((1, s, LANES), seq),
            pl.BlockSpec((1, HEAD_DIM, s), seq),
            pl.BlockSpec((1, s, LANES), seq),
            pl.BlockSpec((1, HEAD_DIM, s), seq),
            pl.BlockSpec((1, GATE_ROWS, s), seq),
            pl.BlockSpec((N_SLC, N_CMP), lambda i, j: (0, 0)),
        ],
        out_specs=pl.BlockSpec((1, tq, w), lambda i, j: (i, j, 0)),
        out_shape=jax.ShapeDtypeStruct((b, s, w), BF16),
        scratch_shapes=[pltpu.VMEM((N_SLC, s), BF16), pltpu.VMEM((w, s), F32),
                        pltpu.VMEM((NSA_HEADS, LANES, tq), BF16), pltpu.VMEM((2 * w, tq), F32),
                        pltpu.VMEM((w, tq), F32), pltpu.VMEM((QK_AHEAD, tk, tq), F32)],
        compiler_params=_cparams("parallel", "arbitrary"),
        name="nsa_attention",
    )(nq_t, kc, vc_t, kslc, vslc_t, kwin, vwin_t, gates_t, ovl)


def _out_ln_kernel(x_ref, on_ref, od_ref, of_ref, wn_ref, wd_ref, wf_ref, lng_ref, lnb_ref, o_ref):
    h = _dot(on_ref[...], wn_ref[...]) + _dot(od_ref[...], wd_ref[...]) + _dot(of_ref[...], wf_ref[...])
    o_ref[...] = _layer_norm(DEEPNORM_ALPHA * x_ref[...] + h, lng_ref[...], lnb_ref[...])


def _out_ln(x, o_nsa, o_diff, o_fox, wn, wd, wf, ln_g, ln_b):
    t = x.shape[0]
    tm = ROW_TILE
    row = lambda i: (i, 0)
    c2 = lambda i: (0, 0)
    return pl.pallas_call(
        _out_ln_kernel,
        grid=(t // tm,),
        in_specs=[
            pl.BlockSpec((tm, D_MODEL), row),
            pl.BlockSpec((tm, NSA_WIDTH), row),
            pl.BlockSpec((tm, DIFF_WIDTH), row),
            pl.BlockSpec((tm, FOX_WIDTH), row),
            pl.BlockSpec((NSA_WIDTH, D_MODEL), c2),
            pl.BlockSpec((DIFF_WIDTH, D_MODEL), c2),
            pl.BlockSpec((FOX_WIDTH, D_MODEL), c2),
            pl.BlockSpec((1, D_MODEL), c2),
            pl.BlockSpec((1, D_MODEL), c2),
        ],
        out_specs=pl.BlockSpec((tm, D_MODEL), row),
        out_shape=jax.ShapeDtypeStruct((t, D_MODEL), F32),
        compiler_params=_cparams("parallel"),
        name="out_ln",
    )(x, o_nsa, o_diff, o_fox, wn, wd, wf, ln_g, ln_b)


def _prep_ffn(wg, wu, wd):
    pad = D_FF_PAD - D_FF
    up = lambda w: jnp.pad(w, ((0, 0), (0, pad))).astype(BF16)
    return up(wg), up(wu), jnp.pad(wd, ((0, pad), (0, 0))).astype(BF16)


def _rot_cols(w, d):
    k, n = w.shape
    w = w.reshape(k, n // d, d)
    return jnp.concatenate([-w[..., d // 2:], w[..., :d // 2]], axis=-1).reshape(k, n)


def _rope_table(pos, d, signed=False):
    inv = ROPE_THETA ** (-jnp.arange(0, d, 2, dtype=F32) / d)
    ang = pos.astype(F32)[:, None] * inv[None, :]
    cos = jnp.concatenate([jnp.cos(ang), jnp.cos(ang)], axis=-1)
    sin = jnp.concatenate([-jnp.sin(ang) if signed else jnp.sin(ang), jnp.sin(ang)], axis=-1)
    return cos, sin


def _fox_placement():
    n_b = MXU_COLS - FOX_GROUP_HEADS * HEAD_DIM
    place = np.zeros((4, LANES, (FOX_QK_WIDTH // MXU_COLS) * n_b), np.float32)
    for h in range(FOX_HEADS):
        g, r = divmod(h, FOX_GROUP_HEADS)
        for j in range(3):
            place[j, h, g * n_b + 8 * r + j] = -1.0
            place[3, 0, g * n_b + 8 * r + 3 + j] = 1.0
    return jnp.asarray(place, dtype=BF16)


def _prep_in_proj(w_in, fox_b_f, s):
    o = 0
    nsa_q = w_in[:, o:o + NSA_WIDTH]; o += NSA_WIDTH
    kv = [w_in[:, o + i * HEAD_DIM:o + (i + 1) * HEAD_DIM] for i in range(6)]; o += 6 * HEAD_DIM
    k_cmp, v_cmp, k_slc, v_slc, k_win, v_win = kv
    nsa_g = w_in[:, o:o + N_NSA_GATES]; o += N_NSA_GATES
    diff_q = w_in[:, o:o + DIFF_WIDTH]; o += DIFF_WIDTH
    diff_k = w_in[:, o:o + DIFF_WIDTH]; o += DIFF_WIDTH
    diff_v = w_in[:, o:o + DIFF_WIDTH]; o += DIFF_WIDTH
    fox_q = w_in[:, o:o + FOX_WIDTH]; o += FOX_WIDTH
    fox_k = w_in[:, o:o + FOX_WIDTH]; o += FOX_WIDTH
    fox_v = w_in[:, o:o + FOX_WIDTH]; o += FOX_WIDTH
    fox_f = w_in[:, o:o + FOX_HEADS]
    zero = jnp.zeros((D_MODEL, LANES - HEAD_DIM), F32)

    wa = jnp.concatenate([k_slc, zero, k_win, zero, diff_k], axis=1)
    wb = jnp.concatenate([fox_k, k_cmp, v_cmp], axis=1)
    wc = jnp.concatenate([fox_f, jnp.zeros((D_MODEL, LANES - FOX_HEADS), F32)], axis=1)
    bc = jnp.concatenate([fox_b_f, jnp.zeros((LANES - FOX_HEADS,), F32)])[None, :]
    wd = jnp.concatenate([nsa_q, diff_q], axis=1).T
    we = jnp.concatenate([v_slc, v_win, diff_v, fox_q * (HEAD_DIM ** -0.5 * LOG2E), fox_v], axis=1).T
    wf = jnp.concatenate([nsa_g, jnp.zeros((D_MODEL, GATE_ROWS - N_NSA_GATES), F32)], axis=1).T

    pos = jnp.arange(s, dtype=jnp.int32)
    c64, s64 = _rope_table(pos, HEAD_DIM, signed=True)
    c32, s32 = _rope_table(pos, DIFF_QK_DIM, signed=True)
    nsa_scale = HEAD_DIM ** -0.5 * LOG2E
    diff_scale = DIFF_QK_DIM ** -0.5 * LOG2E

    def tm_table(t64, t32):
        return jnp.concatenate([t64, t64, t64, t64, jnp.tile(t32, (1, DIFF_MAPS))], axis=1)

    def fm_table(t64, t32):
        return jnp.concatenate([jnp.tile(t64, (1, NSA_HEADS)) * nsa_scale,
                                jnp.tile(t32, (1, DIFF_MAPS)) * diff_scale], axis=1).T

    bf = lambda w: w.astype(BF16)
    return (bf(wa), bf(wb), bf(wc), bf(wd), bf(we), bf(wf),
            tm_table(c64, c32), tm_table(s64, s32), fm_table(c64, c32), fm_table(s64, s32), bc)


def _prep_compress(pos_k, pos_v, phi_k1, phi_k2, phi_v1, phi_v2):
    half = CMP_STRIDE * HEAD_DIM

    def pe(p):
        return jnp.broadcast_to(p.reshape(2, 1, half), (2, 8, half)).astype(BF16)

    block_end = jnp.arange(N_CMP, dtype=jnp.int32) * CMP_STRIDE + (CMP_BLOCK - 1)
    cos_c, sin_c = _rope_table(block_end, HEAD_DIM)
    return (pe(pos_k), pe(pos_v), phi_k1.reshape(2, half, CMP_HIDDEN).astype(BF16), phi_k2.astype(BF16),
            _rot_cols(phi_k2, HEAD_DIM).astype(BF16), phi_v1.reshape(2, half, CMP_HIDDEN).astype(BF16),
            phi_v2.T.astype(BF16), cos_c, sin_c)


def _overlap_matrix(s):
    c0 = np.arange(N_CMP) * CMP_STRIDE
    s0 = np.arange(N_SLC) * SLC_BLOCK
    ovl = (c0[None, :] < s0[:, None] + SLC_BLOCK) & (c0[None, :] + CMP_BLOCK > s0[:, None])
    ovl[:, (s - CMP_BLOCK) // CMP_STRIDE + 1:] = False
    return jnp.asarray(ovl.astype(np.float32), dtype=BF16)


def kernel(x, p, ln_g, ln_b, ffn1_w_gate, ffn1_w_up, ffn1_w_down, ffn2_w_gate, ffn2_w_up, ffn2_w_down, w_in, fox_b_f, nsa_pos_k, nsa_pos_v, nsa_phi_k1, nsa_phi_k2, nsa_phi_v1, nsa_phi_v2, diff_lambda, diff_subln_g, w_out, ple_w_gate, ple_b_gate, ple_w_proj):
    b, s, _ = x.shape
    assert s // SLC_BLOCK == N_SLC and (s - CMP_BLOCK) // CMP_STRIDE + 1 <= N_CMP
    t = b * s
    ovl = _overlap_matrix(s)
    tri = jnp.asarray(np.tril(np.ones((CUM_TILE, CUM_TILE), np.float32)), dtype=BF16)
    place = _fox_placement()
    x = x.reshape(t, D_MODEL)
    for i in range(DEPTH):
        lambda_init = 0.8 - 0.6 * math.exp(-0.3 * i)
        lng = ln_g[i][:, None, :]
        lnb = ln_b[i][:, None, :]
        x = _ffn_ln(x, *_prep_ffn(ffn1_w_gate[i], ffn1_w_up[i], ffn1_w_down[i]), lng[0], lnb[0])
        proj = _in_proj(x.reshape(b, s, D_MODEL), *_prep_in_proj(w_in[i], fox_b_f[i], s))
        kslc, kwin, dk, fk, kcmp, vcmp, fls, nq_t, dq_t, vslc_t, vwin_t, dv_t, fq_t, fv_t, gates_t = proj
        half = CMP_STRIDE * HEAD_DIM
        kc, vc_t = _compress(kcmp.reshape(b, s // CMP_STRIDE, half), vcmp.reshape(b, s // CMP_STRIDE, half),
                             *_prep_compress(nsa_pos_k[i], nsa_pos_v[i], nsa_phi_k1[i], nsa_phi_k2[i],
                                             nsa_phi_v1[i], nsa_phi_v2[i]))
        o_nsa = _nsa_attention(nq_t, kc, vc_t, kslc, vslc_t, kwin, vwin_t, gates_t, ovl)
        o_diff = _diff_attention(dq_t, dk, dv_t, diff_lambda[i], jnp.tile(diff_subln_g[i], DIFF_HEADS)[None, :],
                                 lambda_init)
        o_fox = _fox_attention(fq_t, fk, fv_t, fls, tri, place)
        wo = w_out[i].astype(BF16)
        x = _out_ln(x, o_nsa.reshape(t, NSA_WIDTH), o_diff.reshape(t, DIFF_WIDTH), o_fox.reshape(t, FOX_WIDTH),
                    wo[:NSA_WIDTH], wo[NSA_WIDTH:NSA_WIDTH + DIFF_WIDTH], wo[NSA_WIDTH + DIFF_WIDTH:],
                    lng[1], lnb[1])
        ple = (p.reshape(DEPTH, t, PLE_DIM), i, ple_w_gate[i].astype(BF16), ple_b_gate[i][None, :],
               ple_w_proj[i].astype(BF16))
        x = _ffn_ln(x, *_prep_ffn(ffn2_w_gate[i], ffn2_w_up[i], ffn2_w_down[i]), lng[2], lnb[2], ple=ple)
    return x.reshape(b, s, D_MODEL)
```

```python
import functools
import math

import numpy as np
import jax
import jax.numpy as jnp
from jax import lax
from jax.experimental import pallas as pl
from jax.experimental.pallas import tpu as pltpu

F32 = jnp.float32
BF16 = jnp.bfloat16

D_MODEL = 1024
HEAD_DIM = 64
NSA_HEADS = 4
DIFF_HEADS = 4
FOX_HEADS = 8
NSA_WIDTH = NSA_HEADS * HEAD_DIM
DIFF_WIDTH = DIFF_HEADS * HEAD_DIM
FOX_WIDTH = FOX_HEADS * HEAD_DIM
CMP_BLOCK = 32
CMP_STRIDE = 16
CMP_HIDDEN = 256
SLC_BLOCK = 64
SLC_TOPK = 16
WINDOW = 512
DIFF_QK_DIM = HEAD_DIM // 2
DIFF_MAPS = 2 * DIFF_HEADS
D_FF = 2752
PLE_DIM = 256
ROPE_THETA = 10000.0
LN_EPS = 1e-5
NEG_BIG = -1e30
DEPTH = 2
DEEPNORM_ALPHA = (2.0 * DEPTH) ** 0.25
LOG2E = math.log2(math.e)

LANES = 128
MXU_COLS = 256
BF16_SUBLANES = 16
VMEM_LIMIT_BYTES = 56 * 1024 * 1024

D_FF_PAD = ((D_FF + MXU_COLS - 1) // MXU_COLS) * MXU_COLS
FF_CHUNK = 2 * MXU_COLS
FF_BOUNDS = tuple((c0, min(c0 + FF_CHUNK, D_FF_PAD)) for c0 in range(0, D_FF_PAD, FF_CHUNK))

ROW_TILE = 512
ATT_Q_TILE = 512
ATT_K_TILE = 256
NSA_Q_TILE = 512
NSA_PRE_TILE = 512
CUM_TILE = 256
QK_AHEAD = 2

TM_ROPE_COLS = 2 * LANES + DIFF_WIDTH
TM_PLAIN_COLS = FOX_WIDTH + 2 * HEAD_DIM
FM_ROPE_ROWS = NSA_WIDTH + DIFF_WIDTH
FM_PLAIN_ROWS = 2 * HEAD_DIM + DIFF_WIDTH + 2 * FOX_WIDTH
FOX_GROUP_HEADS = 3
FOX_QK_WIDTH = -(-FOX_HEADS // FOX_GROUP_HEADS) * MXU_COLS
N_NSA_GATES = 3 * NSA_HEADS
GATE_ROWS = BF16_SUBLANES
N_CMP = 128
N_SLC = 32


def _cparams(*sem):
    return pltpu.CompilerParams(dimension_semantics=sem, vmem_limit_bytes=VMEM_LIMIT_BYTES)


def _dot(a, b):
    return jnp.dot(a, b, preferred_element_type=F32)


def _dot_nt(a, b):
    return lax.dot_general(a, b, (((1,), (1,)), ((), ())), preferred_element_type=F32)


def _sigmoid(z):
    return 1.0 / (1.0 + jnp.exp(-z))


def _layer_norm(y, g, b):
    mu = jnp.mean(y, axis=-1, keepdims=True)
    yc = y - mu
    var = jnp.mean(yc * yc, axis=-1, keepdims=True)
    return yc * lax.rsqrt(var + LN_EPS) * g + b


def _ffn_kernel(*refs, with_ple):
    if with_ple:
        (x_ref, wg_ref, wu_ref, wd_ref, lng_ref, lnb_ref, p_ref, pwg_ref, pbg_ref, pwp_ref,
         o_ref, xb_ref) = refs
    else:
        x_ref, wg_ref, wu_ref, wd_ref, lng_ref, lnb_ref, o_ref, xb_ref = refs
    xb_ref[...] = x_ref[...].astype(BF16)
    xb = xb_ref[...]

    def gate_up(c):
        c0, c1 = FF_BOUNDS[c]
        return _dot(xb, wg_ref[:, c0:c1]), _dot(xb, wu_ref[:, c0:c1])

    nxt = gate_up(0)
    acc = None
    for c, (c0, c1) in enumerate(FF_BOUNDS):
        g, u = nxt
        if c + 1 < len(FF_BOUNDS):
            nxt = gate_up(c + 1)
        h = (g * _sigmoid(g)) * u
        part = _dot(h.astype(BF16), wd_ref[c0:c1, :])
        acc = part if acc is None else acc + part
    y = DEEPNORM_ALPHA * x_ref[...] + 0.5 * acc
    z = _layer_norm(y, lng_ref[...], lnb_ref[...])
    if with_ple:
        gate = _sigmoid(_dot(z.astype(BF16), pwg_ref[...]) + pbg_ref[...])
        z = z + gate * _dot(p_ref[0].astype(BF16), pwp_ref[...])
    o_ref[...] = z


def _ffn_ln(x, wg, wu, wd, ln_g, ln_b, ple=None):
    t = x.shape[0]
    tm = ROW_TILE
    row = lambda i: (i, 0)
    full2 = lambda i: (0, 0)
    in_specs = [
        pl.BlockSpec((tm, D_MODEL), row),
        pl.BlockSpec((D_MODEL, D_FF_PAD), full2),
        pl.BlockSpec((D_MODEL, D_FF_PAD), full2),
        pl.BlockSpec((D_FF_PAD, D_MODEL), full2),
        pl.BlockSpec((1, D_MODEL), full2),
        pl.BlockSpec((1, D_MODEL), full2),
    ]
    args = [x, wg, wu, wd, ln_g, ln_b]
    if ple is not None:
        p, layer, pwg, pbg, pwp = ple
        in_specs += [
            pl.BlockSpec((1, tm, PLE_DIM), lambda i: (layer, i, 0)),
            pl.BlockSpec((D_MODEL, D_MODEL), full2),
            pl.BlockSpec((1, D_MODEL), full2),
            pl.BlockSpec((PLE_DIM, D_MODEL), full2),
        ]
        args += [p, pwg, pbg, pwp]
    return pl.pallas_call(
        functools.partial(_ffn_kernel, with_ple=ple is not None),
        grid=(t // tm,),
        in_specs=in_specs,
        out_specs=pl.BlockSpec((tm, D_MODEL), row),
        out_shape=jax.ShapeDtypeStruct((t, D_MODEL), F32),
        scratch_shapes=[pltpu.VMEM((tm, D_MODEL), BF16)],
        compiler_params=_cparams("parallel"),
        name="ffn_ln_ple" if ple is not None else "ffn_ln",
    )(*args)


def _swap_halves_lanes(x, d):
    w = x.shape[1]
    lane = lax.broadcasted_iota(jnp.int32, (1, w), 1)
    return jnp.where(lane % d < d // 2, pltpu.roll(x, w - d // 2, 1), pltpu.roll(x, d // 2, 1))


def _swap_halves_rows(x, d):
    parts = []
    for r0 in range(0, x.shape[0], d):
        parts += [x[r0 + d // 2:r0 + d], x[r0:r0 + d // 2]]
    return jnp.concatenate(parts, axis=0)


def _in_proj_kernel(x_ref, wa_ref, wb_ref, wc_ref, wd_ref, we_ref, wf_ref,
                    cosa_ref, sina_ref, cosd_ref, sind_ref, bc_ref,
                    kslc_ref, kwin_ref, dk_ref, fk_ref, kcmp_ref, vcmp_ref, fls_ref,
                    nq_ref, dq_ref, vslc_ref, vwin_ref, dv_ref, fq_ref, fv_ref, gate_ref, *, tm):
    xb = x_ref[0].astype(BF16)
    pa = _dot(xb, wa_ref[...])
    swapped = jnp.concatenate([_swap_halves_lanes(pa[:, 0:2 * LANES], HEAD_DIM),
                               _swap_halves_lanes(pa[:, 2 * LANES:], DIFF_QK_DIM)], axis=1)
    ra = pa * cosa_ref[...] + swapped * sina_ref[...]
    pos = pl.program_id(1) * tm + lax.broadcasted_iota(jnp.int32, (tm, 1), 0)
    lane = lax.broadcasted_iota(jnp.int32, (1, LANES), 1)
    block_id = jnp.where(lane - HEAD_DIM == pos // SLC_BLOCK, 1.0, 0.0)
    kslc_ref[0] = (ra[:, 0:LANES] + block_id).astype(BF16)
    kwin_ref[0] = ra[:, LANES:2 * LANES].astype(BF16)
    dk_ref[0] = ra[:, 2 * LANES:].astype(BF16)
    pb = _dot(xb, wb_ref[...]).astype(BF16)
    fk_ref[0] = pb[:, 0:FOX_WIDTH]
    kcmp_ref[0] = pb[:, FOX_WIDTH:FOX_WIDTH + HEAD_DIM]
    vcmp_ref[0] = pb[:, FOX_WIDTH + HEAD_DIM:]
    z = _dot(xb, wc_ref[...]) + bc_ref[...]
    fls_ref[0] = jnp.minimum(z, 0.0) - jnp.log(1.0 + jnp.exp(-jnp.abs(z)))

    pd = _dot_nt(wd_ref[...], xb)
    swapped = jnp.concatenate([_swap_halves_rows(pd[0:NSA_WIDTH], HEAD_DIM),
                               _swap_halves_rows(pd[NSA_WIDTH:], DIFF_QK_DIM)], axis=0)
    rd = pd * cosd_ref[...] + swapped * sind_ref[...]
    nq_ref[0] = rd[0:NSA_WIDTH].astype(BF16)
    dq_ref[0] = rd[NSA_WIDTH:].astype(BF16)
    pe = _dot_nt(we_ref[...], xb).astype(BF16)
    vslc_ref[0] = pe[0:HEAD_DIM]
    vwin_ref[0] = pe[HEAD_DIM:2 * HEAD_DIM]
    off = 2 * HEAD_DIM
    dv_ref[0] = pe[off:off + DIFF_WIDTH]
    off += DIFF_WIDTH
    fq_ref[0] = pe[off:off + FOX_WIDTH]
    off += FOX_WIDTH
    fv_ref[0] = pe[off:off + FOX_WIDTH]
    gate_ref[0] = _sigmoid(_dot_nt(wf_ref[...], xb))


def _in_proj(x, wa, wb, wc, wd, we, wf, cosa, sina, cosd, sind, bc):
    b, s, _ = x.shape
    tm = ROW_TILE
    grid = (b, s // tm)
    w2 = lambda i, j: (0, 0)
    tok = lambda i, j: (i, j, 0)
    fm = lambda i, j: (i, 0, j)
    in_specs = [
        pl.BlockSpec((1, tm, D_MODEL), tok),
        pl.BlockSpec((D_MODEL, TM_ROPE_COLS), w2),
        pl.BlockSpec((D_MODEL, TM_PLAIN_COLS), w2),
        pl.BlockSpec((D_MODEL, LANES), w2),
        pl.BlockSpec((FM_ROPE_ROWS, D_MODEL), w2),
        pl.BlockSpec((FM_PLAIN_ROWS, D_MODEL), w2),
        pl.BlockSpec((GATE_ROWS, D_MODEL), w2),
        pl.BlockSpec((tm, TM_ROPE_COLS), lambda i, j: (j, 0)),
        pl.BlockSpec((tm, TM_ROPE_COLS), lambda i, j: (j, 0)),
        pl.BlockSpec((FM_ROPE_ROWS, tm), lambda i, j: (0, j)),
        pl.BlockSpec((FM_ROPE_ROWS, tm), lambda i, j: (0, j)),
        pl.BlockSpec((1, LANES), w2),
    ]

    def tok_out(width, dtype=BF16):
        return pl.BlockSpec((1, tm, width), tok), jax.ShapeDtypeStruct((b, s, width), dtype)

    def fm_out(rows, dtype=BF16):
        return pl.BlockSpec((1, rows, tm), fm), jax.ShapeDtypeStruct((b, rows, s), dtype)

    outs = [
        tok_out(LANES), tok_out(LANES), tok_out(DIFF_WIDTH), tok_out(FOX_WIDTH),
        tok_out(HEAD_DIM), tok_out(HEAD_DIM), tok_out(LANES, F32),
        fm_out(NSA_WIDTH), fm_out(DIFF_WIDTH),
        fm_out(HEAD_DIM), fm_out(HEAD_DIM), fm_out(DIFF_WIDTH),
        fm_out(FOX_WIDTH), fm_out(FOX_WIDTH), fm_out(GATE_ROWS, F32),
    ]
    return pl.pallas_call(
        functools.partial(_in_proj_kernel, tm=tm),
        grid=grid,
        in_specs=in_specs,
        out_specs=[o[0] for o in outs],
        out_shape=[o[1] for o in outs],
        compiler_params=_cparams("parallel", "parallel"),
        name="in_proj",
    )(x, wa, wb, wc, wd, we, wf, cosa, sina, cosd, sind, bc)


def _gelu_tanh(x):
    return 0.5 * x * (1.0 + jnp.tanh(math.sqrt(2.0 / math.pi) * (x + 0.044715 * (x * x * x))))


def _compress_kernel(k16_ref, v16_ref, pek_ref, pev_ref, wk1_ref, wk2_ref, wk2r_ref, wv1_ref, wv2t_ref,
                     cos_ref, sin_ref, kc_ref, vct_ref):
    def hidden(x16, pe_ref, w1_ref):
        top = _dot(x16, w1_ref[0])
        bot = _dot(x16, w1_ref[1])
        bias = _dot(pe_ref[0], w1_ref[0]) + _dot(pe_ref[1], w1_ref[1])
        bot = pltpu.roll(bot, N_CMP - 1, 0)
        return _gelu_tanh(top + bot + bias[0:1, :]).astype(BF16)

    hk = hidden(k16_ref[0], pek_ref, wk1_ref)
    kc = _dot(hk, wk2_ref[...]) * cos_ref[...] + _dot(hk, wk2r_ref[...]) * sin_ref[...]
    kc_ref[0] = kc.astype(BF16)
    hv = hidden(v16_ref[0], pev_ref, wv1_ref)
    vct_ref[0] = _dot_nt(wv2t_ref[...], hv).astype(BF16)


def _compress(k16, v16, pek, pev, wk1, wk2, wk2r, wv1, wv2t, cos_c, sin_c):
    b = k16.shape[0]
    half = CMP_STRIDE * HEAD_DIM
    bat = lambda i: (i, 0, 0)
    c2 = lambda i: (0, 0)
    c3 = lambda i: (0, 0, 0)
    in_specs = [
        pl.BlockSpec((1, N_CMP, half), bat),
        pl.BlockSpec((1, N_CMP, half), bat),
        pl.BlockSpec((2, 8, half), c3),
        pl.BlockSpec((2, 8, half), c3),
        pl.BlockSpec((2, half, CMP_HIDDEN), c3),
        pl.BlockSpec((CMP_HIDDEN, HEAD_DIM), c2),
        pl.BlockSpec((CMP_HIDDEN, HEAD_DIM), c2),
        pl.BlockSpec((2, half, CMP_HIDDEN), c3),
        pl.BlockSpec((HEAD_DIM, CMP_HIDDEN), c2),
        pl.BlockSpec((N_CMP, HEAD_DIM), c2),
        pl.BlockSpec((N_CMP, HEAD_DIM), c2),
    ]
    return pl.pallas_call(
        _compress_kernel,
        grid=(b,),
        in_specs=in_specs,
        out_specs=[pl.BlockSpec((1, N_CMP, HEAD_DIM), bat), pl.BlockSpec((1, HEAD_DIM, N_CMP), bat)],
        out_shape=[jax.ShapeDtypeStruct((b, N_CMP, HEAD_DIM), BF16), jax.ShapeDtypeStruct((b, HEAD_DIM, N_CMP), BF16)],
        compiler_params=_cparams("parallel"),
        name="nsa_compress",
    )(k16, v16, pek, pev, wk1, wk2, wk2r, wv1, wv2t, cos_c, sin_c)


def _key_minus_query(tk, tq, k0, q0):
    d = lax.broadcasted_iota(jnp.int32, (tk, tq), 0) - lax.broadcasted_iota(jnp.int32, (tk, tq), 1)
    return d, q0 - k0


def _causal_mask(tk, q0):
    def mask(i, k0, diagonal, q_lo, width):
        if not diagonal:
            return None
        d, off = _key_minus_query(tk, width, k0, q0 + q_lo)
        return d <= off
    return mask


def _head_rows(h, width):
    return slice(h * width, (h + 1) * width)


def _attend_t(n, tq, tk, qi, lo, scores, values, acc_ref, sbuf_ref, *, mask, n_early=None, join=None):
    n_early = n if n_early is None else n_early
    ahead = min(QK_AHEAD, n_early)
    per_tile = tq // tk
    full = slice(0, tq)
    acc_ref[...] = jnp.zeros(acc_ref.shape, F32)
    k_lo = pl.multiple_of(lo * tk, tk)
    for i in range(ahead):
        sbuf_ref[i] = scores(i, k_lo, full)
    ones = jnp.ones((BF16_SUBLANES, tk), BF16)

    def step(c, heads, diag, ms, ls):
        diagonal = diag is not None
        q_lo = diag * tk if diagonal else 0
        cols = slice(q_lo, tq)
        nxt = slice(q_lo + tk, tq) if diagonal else full
        k0 = pl.multiple_of(c * tk, tk)
        k1 = pl.multiple_of(c * tk + tk, tk)
        ms, ls = list(ms), list(ls)
        pend = [sbuf_ref[i, :, cols] for i in range(ahead)]
        for i in range(heads):
            s = pend.pop(0)
            j = i + ahead
            if j < heads:
                pend.append(scores(j, k0, cols))
            elif nxt.start < tq:
                sbuf_ref[j - heads, :, nxt] = scores(j - heads, k1, nxt)
            mk = mask(i, k0, diagonal, q_lo, tq - q_lo)
            if mk is not None:
                s = jnp.where(mk, s, NEG_BIG)
            m_old, l_old = ms[i][:, cols], ls[i][:, cols]
            m_new = jnp.maximum(m_old, jnp.max(s, axis=0, keepdims=True))
            a = jnp.exp2(m_old - m_new)
            p = jnp.exp2(s - m_new)
            pv = _dot(jnp.concatenate([values(i, k0), ones], axis=0), p.astype(BF16))
            rows = _head_rows(i, HEAD_DIM)
            acc_ref[rows, cols] = a * acc_ref[rows, cols] + pv[0:HEAD_DIM]
            l_new = a * l_old + pv[HEAD_DIM:HEAD_DIM + 1]
            if q_lo:
                m_new = jnp.concatenate([ms[i][:, 0:q_lo], m_new], axis=1)
                l_new = jnp.concatenate([ls[i][:, 0:q_lo], l_new], axis=1)
            ms[i], ls[i] = m_new, l_new
        return tuple(ms), tuple(ls)

    def init(count):
        return (tuple(jnp.full((1, tq), NEG_BIG, F32) for _ in range(count)),
                tuple(jnp.zeros((1, tq), F32) for _ in range(count)))

    first_diag = qi * per_tile
    ms, ls = init(n_early)
    if n_early < n:
        ms, ls = lax.fori_loop(lo, join, lambda c, carry: step(c, n_early, None, *carry), (ms, ls))
        late = init(n - n_early)
        ms, ls, lo = ms + late[0], ls + late[1], join
    ms, ls = lax.fori_loop(lo, first_diag, lambda c, carry: step(c, n, None, *carry), (ms, ls))
    for d in range(per_tile):
        ms, ls = step(first_diag + d, n, d, ms, ls)
    return ls


def _pad_queries(q_t, qpad_ref, n, width):
    per = MXU_COLS // width
    row = lax.broadcasted_iota(jnp.int32, (MXU_COLS, q_t.shape[1]), 0)
    for i in range(n):
        g, r = divmod(i, per)
        qg = q_t[g * MXU_COLS:(g + 1) * MXU_COLS]
        qpad_ref[i] = jnp.where((row >= r * width) & (row < (r + 1) * width), qg, jnp.zeros_like(qg))


def _split3(x):
    hi = x.astype(BF16)
    r1 = x - hi.astype(F32)
    mid = r1.astype(BF16)
    lo = (r1 - mid.astype(F32)).astype(BF16)
    return hi, mid, lo


def _fox_kernel(qt_ref, k_ref, vt_ref, fls_ref, tri_ref, place_ref, o_ref,
                kaug_ref, crow_ref, qpad_ref, acc_ref, sbuf_ref, *, tq, tk, seq):
    qi = pl.program_id(1)
    q0 = pl.multiple_of(qi * tq, tq)
    n_q = FOX_GROUP_HEADS * HEAD_DIM
    n_b = MXU_COLS - n_q
    n_groups = FOX_QK_WIDTH // MXU_COLS

    @pl.when(qi == 0)
    def _():
        carry = jnp.zeros((1, LANES), F32)
        for i in range(seq // CUM_TILE):
            rows = slice(i * CUM_TILE, (i + 1) * CUM_TILE)
            c = carry + sum(_dot(tri_ref[...], part) for part in _split3(fls_ref[0, rows, :]))
            carry = c[CUM_TILE - 1:CUM_TILE, :]
            c = c * LOG2E
            crow_ref[:, rows] = c.T[0:FOX_HEADS, :]
            hi, mid, lo = _split3(c)
            bias = (_dot(hi, place_ref[0]) + _dot(mid, place_ref[1]) + _dot(lo, place_ref[2])
                    + place_ref[3, 0:1, :].astype(F32))
            k = k_ref[0, rows, :].astype(F32)
            pieces = []
            for g in range(n_groups):
                kg = k[:, g * n_q:min((g + 1) * n_q, FOX_WIDTH)]
                pieces.append(kg)
                if kg.shape[1] < n_q:
                    pieces.append(jnp.zeros((CUM_TILE, n_q - kg.shape[1]), F32))
                pieces.append(bias[:, g * n_b:(g + 1) * n_b])
            kaug_ref[rows, :] = jnp.concatenate(pieces, axis=1).astype(BF16)

    brow = lax.broadcasted_iota(jnp.int32, (n_b, 1), 0)
    for h in range(FOX_HEADS):
        r = h % FOX_GROUP_HEADS
        bias = jnp.where((brow >= 8 * r) & (brow < 8 * r + 3), 1.0, 0.0)
        for j, part in enumerate(_split3(crow_ref[h:h + 1, pl.ds(q0, tq)])):
            bias = jnp.where(brow == 8 * r + 3 + j, part.astype(F32), bias)
        zero = lambda heads: [jnp.zeros((heads * HEAD_DIM, tq), BF16)] if heads else []
        qpad_ref[h] = jnp.concatenate(zero(r) + [qt_ref[0, _head_rows(h, HEAD_DIM), :]]
                                      + zero(FOX_GROUP_HEADS - 1 - r) + [bias.astype(BF16)], axis=0)

    ls = _attend_t(
        FOX_HEADS, tq, tk, qi, 0,
        lambda h, k0, cols: _dot(kaug_ref[pl.ds(k0, tk), _head_rows(h // FOX_GROUP_HEADS, MXU_COLS)],
                                 qpad_ref[h, :, cols]),
        lambda h, k0: vt_ref[0, _head_rows(h, HEAD_DIM), pl.ds(k0, tk)],
        acc_ref, sbuf_ref, mask=_causal_mask(tk, q0))
    for h in range(FOX_HEADS):
        rows = _head_rows(h, HEAD_DIM)
        acc_ref[rows, :] = acc_ref[rows, :] * (1.0 / ls[h])
    o_ref[0] = acc_ref[...].T.astype(BF16)


def _fox_attention(fq_t, fk, fv_t, fls, tri, place):
    b, w, s = fv_t.shape
    wk = FOX_QK_WIDTH
    tq, tk = ATT_Q_TILE, ATT_K_TILE
    return pl.pallas_call(
        functools.partial(_fox_kernel, tq=tq, tk=tk, seq=s),
        grid=(b, s // tq),
        in_specs=[
            pl.BlockSpec((1, w, tq), lambda i, j: (i, 0, j)),
            pl.BlockSpec((1, s, w), lambda i, j: (i, 0, 0)),
            pl.BlockSpec((1, w, s), lambda i, j: (i, 0, 0)),
            pl.BlockSpec((1, s, LANES), lambda i, j: (i, 0, 0)),
            pl.BlockSpec((CUM_TILE, CUM_TILE), lambda i, j: (0, 0)),
            pl.BlockSpec(place.shape, lambda i, j: (0, 0, 0)),
        ],
        out_specs=pl.BlockSpec((1, tq, w), lambda i, j: (i, j, 0)),
        out_shape=jax.ShapeDtypeStruct((b, s, w), BF16),
        scratch_shapes=[pltpu.VMEM((s, wk), BF16), pltpu.VMEM((FOX_HEADS, s), F32),
                        pltpu.VMEM((FOX_HEADS, MXU_COLS, tq), BF16), pltpu.VMEM((w, tq), F32),
                        pltpu.VMEM((QK_AHEAD, tk, tq), F32)],
        compiler_params=_cparams("parallel", "arbitrary"),
        name="fox_attention",
    )(fq_t, fk, fv_t, fls, tri, place)


def _diff_kernel(qt_ref, k_ref, vt_ref, lam_ref, g_ref, o_ref, qpad_ref, acc_ref, sbuf_ref, *, tq, tk, lambda_init):
    qi = pl.program_id(1)
    _pad_queries(qt_ref[0], qpad_ref, DIFF_MAPS, DIFF_QK_DIM)
    ls = _attend_t(
        DIFF_MAPS, tq, tk, qi, 0,
        lambda i, k0, cols: _dot(k_ref[0, pl.ds(k0, tk), :], qpad_ref[i, :, cols]),
        lambda i, k0: vt_ref[0, _head_rows(i // 2, HEAD_DIM), pl.ds(k0, tk)],
        acc_ref, sbuf_ref, mask=_causal_mask(tk, qi * tq))
    lp = lam_ref[...]
    lam = (jnp.exp(jnp.sum(lp[0:1] * lp[1:2], axis=1, keepdims=True))
           - jnp.exp(jnp.sum(lp[2:3] * lp[3:4], axis=1, keepdims=True)) + lambda_init)
    heads = []
    for h in range(DIFF_HEADS):
        o1 = acc_ref[_head_rows(2 * h, HEAD_DIM), :] * (1.0 / ls[2 * h])
        o2 = acc_ref[_head_rows(2 * h + 1, HEAD_DIM), :] * (1.0 / ls[2 * h + 1])
        o = o1 - lam * o2
        o = o * lax.rsqrt(jnp.mean(o * o, axis=0, keepdims=True) + LN_EPS)
        heads.append(o * (1.0 - lambda_init))
    o_t = jnp.concatenate(heads, axis=0)
    o_ref[0] = (o_t.T * g_ref[...]).astype(BF16)


def _diff_attention(dq_t, dk, dv_t, lam_params, subln_g, lambda_init):
    b, w, s = dq_t.shape
    tq, tk = ATT_Q_TILE, ATT_K_TILE
    return pl.pallas_call(
        functools.partial(_diff_kernel, tq=tq, tk=tk, lambda_init=lambda_init),
        grid=(b, s // tq),
        in_specs=[
            pl.BlockSpec((1, w, tq), lambda i, j: (i, 0, j)),
            pl.BlockSpec((1, s, w), lambda i, j: (i, 0, 0)),
            pl.BlockSpec((1, w, s), lambda i, j: (i, 0, 0)),
            pl.BlockSpec((4, DIFF_QK_DIM), lambda i, j: (0, 0)),
            pl.BlockSpec((1, w), lambda i, j: (0, 0)),
        ],
        out_specs=pl.BlockSpec((1, tq, w), lambda i, j: (i, j, 0)),
        out_shape=jax.ShapeDtypeStruct((b, s, w), BF16),
        scratch_shapes=[pltpu.VMEM((DIFF_MAPS, MXU_COLS, tq), BF16), pltpu.VMEM((DIFF_MAPS * HEAD_DIM, tq), F32),
                        pltpu.VMEM((QK_AHEAD, tk, tq), F32)],
        compiler_params=_cparams("parallel", "arbitrary"),
        name="diff_attention",
    )(dq_t, dk, dv_t, lam_params, subln_g)


def _nsa_cmp_scores(qt_ref, kc_ref, t0, width):
    cols = slice(t0, t0 + width)
    return [_dot(kc_ref[0], qt_ref[0, _head_rows(h, HEAD_DIM), cols]) for h in range(NSA_HEADS)]


def _nsa_select(scores, vct_ref, g_ref, ovl_ref, sel_ref, ocmp_ref, t0, width):
    cols = slice(t0, t0 + width)
    t_q = t0 + lax.broadcasted_iota(jnp.int32, (1, width), 1)
    block_end = CMP_STRIDE * lax.broadcasted_iota(jnp.int32, (N_CMP, 1), 0) + (CMP_BLOCK - 1)
    vis = block_end <= t_q
    any_vis = t_q >= CMP_BLOCK - 1
    lhs = jnp.concatenate([vct_ref[0], ovl_ref[...], jnp.ones((BF16_SUBLANES, N_CMP), BF16)], axis=0)
    imp = jnp.zeros((N_SLC, width), F32)
    for h in range(NSA_HEADS):
        rows = _head_rows(h, HEAD_DIM)
        s = jnp.where(vis, scores[h], NEG_BIG)
        p = jnp.exp2(s - jnp.max(s, axis=0, keepdims=True))
        r = _dot(lhs, p.astype(BF16))
        l = r[HEAD_DIM + N_SLC:HEAD_DIM + N_SLC + 1]
        scale = jnp.where(any_vis, 1.0 / l, 0.0)
        ocmp_ref[rows, cols] = (g_ref[0, 3 * h:3 * h + 1, cols] * scale) * r[0:HEAD_DIM]
        imp = imp + scale * r[HEAD_DIM:HEAD_DIM + N_SLC]
    j_idx = lax.broadcasted_iota(jnp.int32, (N_SLC, 1), 0)
    blk_t = t_q // SLC_BLOCK
    forced = (j_idx == 0) | (j_idx == blk_t) | (j_idx == blk_t - 1)
    valid = j_idx * SLC_BLOCK <= t_q
    score = jnp.where(forced, 1e9, jnp.where(valid, imp, -1.0))
    groups = [score[g * 8:(g + 1) * 8] for g in range(N_SLC // 8)]
    ranks = [jnp.zeros((8, width), F32) for _ in groups]
    row = lax.broadcasted_iota(jnp.int32, (8, 1), 0)
    for i in range(N_SLC):
        gi, ri = divmod(i, 8)
        si = groups[gi][ri:ri + 1, :]
        for g, sg in enumerate(groups):
            if g < gi:
                ranks[g] = ranks[g] + jnp.where(si > sg, 1.0, 0.0)
            elif g > gi:
                ranks[g] = ranks[g] + jnp.where(si >= sg, 1.0, 0.0)
            else:
                tie = jnp.where(row > ri, 1.0, 0.0)
                ranks[g] = ranks[g] + jnp.where(si > sg, 1.0, 0.0) + jnp.where(si == sg, tie, 0.0)
    rank = jnp.concatenate(ranks, axis=0)
    sel_ref[:, cols] = jnp.where(rank < float(SLC_TOPK), 0.0, NEG_BIG).astype(BF16)


def _nsa_kernel(qt_ref, kc_ref, vct_ref, ks_ref, vst_ref, kw_ref, vwt_ref, g_ref, ovl_ref,
                o_ref, sel_ref, ocmp_ref, qaug_ref, acc_ref, out_ref, sbuf_ref, *, tq, tk, seq):
    qi = pl.program_id(1)
    q0 = pl.multiple_of(qi * tq, tq)

    @pl.when(qi == 0)
    def _():
        nxt = _nsa_cmp_scores(qt_ref, kc_ref, 0, NSA_PRE_TILE)
        for t0 in range(0, seq, NSA_PRE_TILE):
            scores = nxt
            if t0 + NSA_PRE_TILE < seq:
                nxt = _nsa_cmp_scores(qt_ref, kc_ref, t0 + NSA_PRE_TILE, NSA_PRE_TILE)
            _nsa_select(scores, vct_ref, g_ref, ovl_ref, sel_ref, ocmp_ref, t0, NSA_PRE_TILE)

    pad = jnp.zeros((LANES - HEAD_DIM - N_SLC, tq), BF16)
    sel_neg = sel_ref[:, pl.ds(q0, tq)]
    for h in range(NSA_HEADS):
        qaug_ref[h] = jnp.concatenate([qt_ref[0, _head_rows(h, HEAD_DIM), pl.ds(q0, tq)], sel_neg, pad], axis=0)

    def scores(i, k0, cols):
        k_ref = ks_ref if i < NSA_HEADS else kw_ref
        return _dot(k_ref[0, pl.ds(k0, tk), :], qaug_ref[i % NSA_HEADS, :, cols])

    def values(i, k0):
        vt_ref = vst_ref if i < NSA_HEADS else vwt_ref
        return vt_ref[0, :, pl.ds(k0, tk)]

    def mask(i, k0, diagonal, q_lo, width):
        d, off = _key_minus_query(tk, width, k0, q0 + q_lo)
        if diagonal:
            return d <= off
        return None if i < NSA_HEADS else d > off - WINDOW

    ls = _attend_t(2 * NSA_HEADS, tq, tk, qi, 0, scores, values, acc_ref, sbuf_ref, mask=mask,
                   n_early=NSA_HEADS, join=jnp.maximum(q0 - WINDOW, 0) // tk)
    for h in range(NSA_HEADS):
        rows = _head_rows(h, HEAD_DIM)
        out = ocmp_ref[rows, pl.ds(q0, tq)]
        for branch in (1, 2):
            i = (branch - 1) * NSA_HEADS + h
            gate = g_ref[0, 3 * h + branch:3 * h + branch + 1, pl.ds(q0, tq)]
            out = out + (gate * (1.0 / ls[i])) * acc_ref[_head_rows(i, HEAD_DIM), :]
        out_ref[rows, :] = out
    o_ref[0] = out_ref[...].T.astype(BF16)


def _nsa_attention(nq_t, kc, vc_t, kslc, vslc_t, kwin, vwin_t, gates_t, ovl):
    b, w, s = nq_t.shape
    tq, tk = NSA_Q_TILE, ATT_K_TILE
    seq = lambda i, j: (i, 0, 0)
    return pl.pallas_call(
        functools.partial(_nsa_kernel, tq=tq, tk=tk, seq=s),
        grid=(b, s // tq),
        in_specs=[
            pl.BlockSpec((1, w, s), seq),
            pl.BlockSpec((1, N_CMP, HEAD_DIM), seq),
            pl.BlockSpec((1, HEAD_DIM, N_CMP), seq),
            pl.BlockSpec((1, s, LANES), seq),
            pl.BlockSpec((1, HEAD_DIM, s), seq),
            pl.BlockSpec((1, s, LANES), seq),
            pl.BlockSpec((1, HEAD_DIM, s), seq),
            pl.BlockSpec((1, GATE_ROWS, s), seq),
            pl.BlockSpec((N_SLC, N_CMP), lambda i, j: (0, 0)),
        ],
        out_specs=pl.BlockSpec((1, tq, w), lambda i, j: (i, j, 0)),
        out_shape=jax.ShapeDtypeStruct((b, s, w), BF16),
        scratch_shapes=[pltpu.VMEM((N_SLC, s), BF16), pltpu.VMEM((w, s), F32),
                        pltpu.VMEM((NSA_HEADS, LANES, tq), BF16), pltpu.VMEM((2 * w, tq), F32),
                        pltpu.VMEM((w, tq), F32), pltpu.VMEM((QK_AHEAD, tk, tq), F32)],
        compiler_params=_cparams("parallel", "arbitrary"),
        name="nsa_attention",
    )(nq_t, kc, vc_t, kslc, vslc_t, kwin, vwin_t, gates_t, ovl)


def _out_ln_kernel(x_ref, on_ref, od_ref, of_ref, w_ref, lng_ref, lnb_ref, o_ref):
    o = jnp.concatenate([on_ref[...], od_ref[...], of_ref[...]], axis=1)
    o_ref[...] = _layer_norm(DEEPNORM_ALPHA * x_ref[...] + _dot(o, w_ref[...]), lng_ref[...], lnb_ref[...])


def _out_ln(x, o_nsa, o_diff, o_fox, w, ln_g, ln_b):
    t = x.shape[0]
    tm = ROW_TILE
    row = lambda i: (i, 0)
    c2 = lambda i: (0, 0)
    return pl.pallas_call(
        _out_ln_kernel,
        grid=(t // tm,),
        in_specs=[
            pl.BlockSpec((tm, D_MODEL), row),
            pl.BlockSpec((tm, NSA_WIDTH), row),
            pl.BlockSpec((tm, DIFF_WIDTH), row),
            pl.BlockSpec((tm, FOX_WIDTH), row),
            pl.BlockSpec((NSA_WIDTH + DIFF_WIDTH + FOX_WIDTH, D_MODEL), c2),
            pl.BlockSpec((1, D_MODEL), c2),
            pl.BlockSpec((1, D_MODEL), c2),
        ],
        out_specs=pl.BlockSpec((tm, D_MODEL), row),
        out_shape=jax.ShapeDtypeStruct((t, D_MODEL), F32),
        compiler_params=_cparams("parallel"),
        name="out_ln",
    )(x, o_nsa, o_diff, o_fox, w, ln_g, ln_b)


def _prep_ffn(wg, wu, wd):
    pad = D_FF_PAD - D_FF
    up = lambda w: jnp.pad(w, ((0, 0), (0, pad))).astype(BF16)
    return up(wg), up(wu), jnp.pad(wd, ((0, pad), (0, 0))).astype(BF16)


def _rot_cols(w, d):
    k, n = w.shape
    w = w.reshape(k, n // d, d)
    return jnp.concatenate([-w[..., d // 2:], w[..., :d // 2]], axis=-1).reshape(k, n)


def _rope_table(pos, d, signed=False):
    inv = ROPE_THETA ** (-jnp.arange(0, d, 2, dtype=F32) / d)
    ang = pos.astype(F32)[:, None] * inv[None, :]
    cos = jnp.concatenate([jnp.cos(ang), jnp.cos(ang)], axis=-1)
    sin = jnp.concatenate([-jnp.sin(ang) if signed else jnp.sin(ang), jnp.sin(ang)], axis=-1)
    return cos, sin


def _fox_placement():
    n_b = MXU_COLS - FOX_GROUP_HEADS * HEAD_DIM
    place = np.zeros((4, LANES, (FOX_QK_WIDTH // MXU_COLS) * n_b), np.float32)
    for h in range(FOX_HEADS):
        g, r = divmod(h, FOX_GROUP_HEADS)
        for j in range(3):
            place[j, h, g * n_b + 8 * r + j] = -1.0
            place[3, 0, g * n_b + 8 * r + 3 + j] = 1.0
    return jnp.asarray(place, dtype=BF16)


def _prep_in_proj(w_in, fox_b_f, s):
    o = 0
    nsa_q = w_in[:, o:o + NSA_WIDTH]; o += NSA_WIDTH
    kv = [w_in[:, o + i * HEAD_DIM:o + (i + 1) * HEAD_DIM] for i in range(6)]; o += 6 * HEAD_DIM
    k_cmp, v_cmp, k_slc, v_slc, k_win, v_win = kv
    nsa_g = w_in[:, o:o + N_NSA_GATES]; o += N_NSA_GATES
    diff_q = w_in[:, o:o + DIFF_WIDTH]; o += DIFF_WIDTH
    diff_k = w_in[:, o:o + DIFF_WIDTH]; o += DIFF_WIDTH
    diff_v = w_in[:, o:o + DIFF_WIDTH]; o += DIFF_WIDTH
    fox_q = w_in[:, o:o + FOX_WIDTH]; o += FOX_WIDTH
    fox_k = w_in[:, o:o + FOX_WIDTH]; o += FOX_WIDTH
    fox_v = w_in[:, o:o + FOX_WIDTH]; o += FOX_WIDTH
    fox_f = w_in[:, o:o + FOX_HEADS]
    zero = jnp.zeros((D_MODEL, LANES - HEAD_DIM), F32)

    wa = jnp.concatenate([k_slc, zero, k_win, zero, diff_k], axis=1)
    wb = jnp.concatenate([fox_k, k_cmp, v_cmp], axis=1)
    wc = jnp.concatenate([fox_f, jnp.zeros((D_MODEL, LANES - FOX_HEADS), F32)], axis=1)
    bc = jnp.concatenate([fox_b_f, jnp.zeros((LANES - FOX_HEADS,), F32)])[None, :]
    wd = jnp.concatenate([nsa_q, diff_q], axis=1).T
    we = jnp.concatenate([v_slc, v_win, diff_v, fox_q * (HEAD_DIM ** -0.5 * LOG2E), fox_v], axis=1).T
    wf = jnp.concatenate([nsa_g, jnp.zeros((D_MODEL, GATE_ROWS - N_NSA_GATES), F32)], axis=1).T

    pos = jnp.arange(s, dtype=jnp.int32)
    c64, s64 = _rope_table(pos, HEAD_DIM, signed=True)
    c32, s32 = _rope_table(pos, DIFF_QK_DIM, signed=True)
    nsa_scale = HEAD_DIM ** -0.5 * LOG2E
    diff_scale = DIFF_QK_DIM ** -0.5 * LOG2E

    def tm_table(t64, t32):
        return jnp.concatenate([t64, t64, t64, t64, jnp.tile(t32, (1, DIFF_MAPS))], axis=1)

    def fm_table(t64, t32):
        return jnp.concatenate([jnp.tile(t64, (1, NSA_HEADS)) * nsa_scale,
                                jnp.tile(t32, (1, DIFF_MAPS)) * diff_scale], axis=1).T

    bf = lambda w: w.astype(BF16)
    return (bf(wa), bf(wb), bf(wc), bf(wd), bf(we), bf(wf),
            tm_table(c64, c32), tm_table(s64, s32), fm_table(c64, c32), fm_table(s64, s32), bc)


def _prep_compress(pos_k, pos_v, phi_k1, phi_k2, phi_v1, phi_v2):
    half = CMP_STRIDE * HEAD_DIM

    def pe(p):
        return jnp.broadcast_to(p.reshape(2, 1, half), (2, 8, half)).astype(BF16)

    block_end = jnp.arange(N_CMP, dtype=jnp.int32) * CMP_STRIDE + (CMP_BLOCK - 1)
    cos_c, sin_c = _rope_table(block_end, HEAD_DIM)
    return (pe(pos_k), pe(pos_v), phi_k1.reshape(2, half, CMP_HIDDEN).astype(BF16), phi_k2.astype(BF16),
            _rot_cols(phi_k2, HEAD_DIM).astype(BF16), phi_v1.reshape(2, half, CMP_HIDDEN).astype(BF16),
            phi_v2.T.astype(BF16), cos_c, sin_c)


def _overlap_matrix(s):
    c0 = np.arange(N_CMP) * CMP_STRIDE
    s0 = np.arange(N_SLC) * SLC_BLOCK
    ovl = (c0[None, :] < s0[:, None] + SLC_BLOCK) & (c0[None, :] + CMP_BLOCK > s0[:, None])
    ovl[:, (s - CMP_BLOCK) // CMP_STRIDE + 1:] = False
    return jnp.asarray(ovl.astype(np.float32), dtype=BF16)


def kernel(x, p, ln_g, ln_b, ffn1_w_gate, ffn1_w_up, ffn1_w_down, ffn2_w_gate, ffn2_w_up, ffn2_w_down, w_in, fox_b_f, nsa_pos_k, nsa_pos_v, nsa_phi_k1, nsa_phi_k2, nsa_phi_v1, nsa_phi_v2, diff_lambda, diff_subln_g, w_out, ple_w_gate, ple_b_gate, ple_w_proj):
    b, s, _ = x.shape
    assert s // SLC_BLOCK == N_SLC and (s - CMP_BLOCK) // CMP_STRIDE + 1 <= N_CMP
    t = b * s
    ovl = _overlap_matrix(s)
    tri = jnp.asarray(np.tril(np.ones((CUM_TILE, CUM_TILE), np.float32)), dtype=BF16)
    place = _fox_placement()
    x = x.reshape(t, D_MODEL)
    for i in range(DEPTH):
        lambda_init = 0.8 - 0.6 * math.exp(-0.3 * i)
        lng = ln_g[i][:, None, :]
        lnb = ln_b[i][:, None, :]
        x = _ffn_ln(x, *_prep_ffn(ffn1_w_gate[i], ffn1_w_up[i], ffn1_w_down[i]), lng[0], lnb[0])
        proj = _in_proj(x.reshape(b, s, D_MODEL), *_prep_in_proj(w_in[i], fox_b_f[i], s))
        kslc, kwin, dk, fk, kcmp, vcmp, fls, nq_t, dq_t, vslc_t, vwin_t, dv_t, fq_t, fv_t, gates_t = proj
        half = CMP_STRIDE * HEAD_DIM
        kc, vc_t = _compress(kcmp.reshape(b, s // CMP_STRIDE, half), vcmp.reshape(b, s // CMP_STRIDE, half),
                             *_prep_compress(nsa_pos_k[i], nsa_pos_v[i], nsa_phi_k1[i], nsa_phi_k2[i],
                                             nsa_phi_v1[i], nsa_phi_v2[i]))
        o_nsa = _nsa_attention(nq_t, kc, vc_t, kslc, vslc_t, kwin, vwin_t, gates_t, ovl)
        o_diff = _diff_attention(dq_t, dk, dv_t, diff_lambda[i], jnp.tile(diff_subln_g[i], DIFF_HEADS)[None, :],
                                 lambda_init)
        o_fox = _fox_attention(fq_t, fk, fv_t, fls, tri, place)
        x = _out_ln(x, o_nsa.reshape(t, NSA_WIDTH), o_diff.reshape(t, DIFF_WIDTH), o_fox.reshape(t, FOX_WIDTH),
                    w_out[i].astype(BF16), lng[1], lnb[1])
        ple = (p.reshape(DEPTH, t, PLE_DIM), i, ple_w_gate[i].astype(BF16), ple_b_gate[i][None, :],
               ple_w_proj[i].astype(BF16))
        x = _ffn_ln(x, *_prep_ffn(ffn2_w_gate[i], ffn2_w_up[i], ffn2_w_down[i]), lng[2], lnb[2], ple=ple)
    return x.reshape(b, s, D_MODEL)
```

```python
import functools
import math

import numpy as np
import jax
import jax.numpy as jnp
from jax import lax
from jax.experimental import pallas as pl
from jax.experimental.pallas import tpu as pltpu

F32 = jnp.float32
BF16 = jnp.bfloat16

D_MODEL = 1024
HEAD_DIM = 64
NSA_HEADS = 4
DIFF_HEADS = 4
FOX_HEADS = 8
NSA_WIDTH = NSA_HEADS * HEAD_DIM
DIFF_WIDTH = DIFF_HEADS * HEAD_DIM
FOX_WIDTH = FOX_HEADS * HEAD_DIM
CMP_BLOCK = 32
CMP_STRIDE = 16
CMP_HIDDEN = 256
SLC_BLOCK = 64
SLC_TOPK = 16
WINDOW = 512
DIFF_QK_DIM = HEAD_DIM // 2
DIFF_MAPS = 2 * DIFF_HEADS
D_FF = 2752
PLE_DIM = 256
ROPE_THETA = 10000.0
LN_EPS = 1e-5
NEG_BIG = -1e30
DEPTH = 2
DEEPNORM_ALPHA = (2.0 * DEPTH) ** 0.25
LOG2E = math.log2(math.e)

LANES = 128
MXU_COLS = 256
BF16_SUBLANES = 16
VMEM_LIMIT_BYTES = 56 * 1024 * 1024

FF_CHUNK = 2 * MXU_COLS
FF_BOUNDS = tuple((c0, min(c0 + FF_CHUNK, D_FF)) for c0 in range(0, D_FF, FF_CHUNK))

ROW_TILE = 512
ATT_Q_TILE = 512
ATT_K_TILE = 256
NSA_Q_TILE = 512
NSA_PRE_TILE = 512
CUM_TILE = 256
QK_AHEAD = 2

TM_ROPE_COLS = 2 * LANES + DIFF_WIDTH
TM_PLAIN_COLS = FOX_WIDTH + 2 * HEAD_DIM
FM_ROPE_ROWS = NSA_WIDTH + DIFF_WIDTH
FM_PLAIN_ROWS = 2 * HEAD_DIM + DIFF_WIDTH + 2 * FOX_WIDTH
FOX_GROUP_HEADS = 3
FOX_QK_WIDTH = -(-FOX_HEADS // FOX_GROUP_HEADS) * MXU_COLS
N_NSA_GATES = 3 * NSA_HEADS
GATE_ROWS = BF16_SUBLANES
N_CMP = 128
N_SLC = 32


def _cparams(*sem):
    return pltpu.CompilerParams(dimension_semantics=sem, vmem_limit_bytes=VMEM_LIMIT_BYTES)


def _dot(a, b):
    return jnp.dot(a, b, preferred_element_type=F32)


def _dot_nt(a, b):
    return lax.dot_general(a, b, (((1,), (1,)), ((), ())), preferred_element_type=F32)


def _sigmoid(z):
    return 1.0 / (1.0 + jnp.exp(-z))


def _layer_norm(y, g, b):
    mu = jnp.mean(y, axis=-1, keepdims=True)
    yc = y - mu
    var = jnp.mean(yc * yc, axis=-1, keepdims=True)
    return yc * lax.rsqrt(var + LN_EPS) * g + b


def _ffn_kernel(*refs, with_ple):
    if with_ple:
        (x_ref, wg_ref, wu_ref, wd_ref, lng_ref, lnb_ref, p_ref, pwg_ref, pbg_ref, pwp_ref,
         o_ref, xb_ref) = refs
    else:
        x_ref, wg_ref, wu_ref, wd_ref, lng_ref, lnb_ref, o_ref, xb_ref = refs
    xb_ref[...] = x_ref[...].astype(BF16)
    xb = xb_ref[...]

    def gate_up(c):
        c0, c1 = FF_BOUNDS[c]
        return _dot(xb, wg_ref[:, c0:c1]), _dot(xb, wu_ref[:, c0:c1])

    nxt = gate_up(0)
    acc = None
    for c, (c0, c1) in enumerate(FF_BOUNDS):
        g, u = nxt
        if c + 1 < len(FF_BOUNDS):
            nxt = gate_up(c + 1)
        h = (g * _sigmoid(g)) * u
        part = _dot(h.astype(BF16), wd_ref[c0:c1, :])
        acc = part if acc is None else acc + part
    y = DEEPNORM_ALPHA * x_ref[...] + 0.5 * acc
    z = _layer_norm(y, lng_ref[...], lnb_ref[...])
    if with_ple:
        gate = _sigmoid(_dot(z.astype(BF16), pwg_ref[...]) + pbg_ref[...])
        z = z + gate * _dot(p_ref[0].astype(BF16), pwp_ref[...])
    o_ref[...] = z


def _ffn_ln(x, wg, wu, wd, ln_g, ln_b, ple=None):
    t = x.shape[0]
    tm = ROW_TILE
    row = lambda i: (i, 0)
    full2 = lambda i: (0, 0)
    in_specs = [
        pl.BlockSpec((tm, D_MODEL), row),
        pl.BlockSpec((D_MODEL, D_FF), full2),
        pl.BlockSpec((D_MODEL, D_FF), full2),
        pl.BlockSpec((D_FF, D_MODEL), full2),
        pl.BlockSpec((1, D_MODEL), full2),
        pl.BlockSpec((1, D_MODEL), full2),
    ]
    args = [x, wg, wu, wd, ln_g, ln_b]
    if ple is not None:
        p, layer, pwg, pbg, pwp = ple
        in_specs += [
            pl.BlockSpec((1, tm, PLE_DIM), lambda i: (layer, i, 0)),
            pl.BlockSpec((D_MODEL, D_MODEL), full2),
            pl.BlockSpec((1, D_MODEL), full2),
            pl.BlockSpec((PLE_DIM, D_MODEL), full2),
        ]
        args += [p, pwg, pbg, pwp]
    return pl.pallas_call(
        functools.partial(_ffn_kernel, with_ple=ple is not None),
        grid=(t // tm,),
        in_specs=in_specs,
        out_specs=pl.BlockSpec((tm, D_MODEL), row),
        out_shape=jax.ShapeDtypeStruct((t, D_MODEL), F32),
        scratch_shapes=[pltpu.VMEM((tm, D_MODEL), BF16)],
        compiler_params=_cparams("parallel"),
        name="ffn_ln_ple" if ple is not None else "ffn_ln",
    )(*args)


def _swap_halves_lanes(x, d):
    w = x.shape[1]
    lane = lax.broadcasted_iota(jnp.int32, (1, w), 1)
    return jnp.where(lane % d < d // 2, pltpu.roll(x, w - d // 2, 1), pltpu.roll(x, d // 2, 1))


def _swap_halves_rows(x, d):
    parts = []
    for r0 in range(0, x.shape[0], d):
        parts += [x[r0 + d // 2:r0 + d], x[r0:r0 + d // 2]]
    return jnp.concatenate(parts, axis=0)


def _in_proj_kernel(x_ref, wa_ref, wb_ref, wc_ref, wd_ref, we_ref, wf_ref,
                    cosa_ref, sina_ref, cosd_ref, sind_ref, bc_ref,
                    kslc_ref, kwin_ref, dk_ref, fk_ref, kcmp_ref, vcmp_ref, fls_ref,
                    nq_ref, dq_ref, vslc_ref, vwin_ref, dv_ref, fq_ref, fv_ref, gate_ref, *, tm):
    xb = x_ref[0].astype(BF16)
    pa = _dot(xb, wa_ref[...])
    swapped = jnp.concatenate([_swap_halves_lanes(pa[:, 0:2 * LANES], HEAD_DIM),
                               _swap_halves_lanes(pa[:, 2 * LANES:], DIFF_QK_DIM)], axis=1)
    ra = pa * cosa_ref[...] + swapped * sina_ref[...]
    pos = pl.program_id(1) * tm + lax.broadcasted_iota(jnp.int32, (tm, 1), 0)
    lane = lax.broadcasted_iota(jnp.int32, (1, LANES), 1)
    block_id = jnp.where(lane - HEAD_DIM == pos // SLC_BLOCK, 1.0, 0.0)
    kslc_ref[0] = (ra[:, 0:LANES] + block_id).astype(BF16)
    kwin_ref[0] = ra[:, LANES:2 * LANES].astype(BF16)
    dk_ref[0] = ra[:, 2 * LANES:].astype(BF16)
    pb = _dot(xb, wb_ref[...]).astype(BF16)
    fk_ref[0] = pb[:, 0:FOX_WIDTH]
    kcmp_ref[0] = pb[:, FOX_WIDTH:FOX_WIDTH + HEAD_DIM]
    vcmp_ref[0] = pb[:, FOX_WIDTH + HEAD_DIM:]
    z = _dot(xb, wc_ref[...]) + bc_ref[...]
    fls_ref[0] = jnp.minimum(z, 0.0) - jnp.log(1.0 + jnp.exp(-jnp.abs(z)))

    pd = _dot_nt(wd_ref[...], xb)
    swapped = jnp.concatenate([_swap_halves_rows(pd[0:NSA_WIDTH], HEAD_DIM),
                               _swap_halves_rows(pd[NSA_WIDTH:], DIFF_QK_DIM)], axis=0)
    rd = pd * cosd_ref[...] + swapped * sind_ref[...]
    nq_ref[0] = rd[0:NSA_WIDTH].astype(BF16)
    dq_ref[0] = rd[NSA_WIDTH:].astype(BF16)
    pe = _dot_nt(we_ref[...], xb).astype(BF16)
    vslc_ref[0] = pe[0:HEAD_DIM]
    vwin_ref[0] = pe[HEAD_DIM:2 * HEAD_DIM]
    off = 2 * HEAD_DIM
    dv_ref[0] = pe[off:off + DIFF_WIDTH]
    off += DIFF_WIDTH
    fq_ref[0] = pe[off:off + FOX_WIDTH]
    off += FOX_WIDTH
    fv_ref[0] = pe[off:off + FOX_WIDTH]
    gate_ref[0] = _sigmoid(_dot_nt(wf_ref[...], xb))


def _in_proj(x, wa, wb, wc, wd, we, wf, cosa, sina, cosd, sind, bc):
    b, s, _ = x.shape
    tm = 2 * ROW_TILE
    grid = (b, s // tm)
    w2 = lambda i, j: (0, 0)
    tok = lambda i, j: (i, j, 0)
    fm = lambda i, j: (i, 0, j)
    in_specs = [
        pl.BlockSpec((1, tm, D_MODEL), tok),
        pl.BlockSpec((D_MODEL, TM_ROPE_COLS), w2),
        pl.BlockSpec((D_MODEL, TM_PLAIN_COLS), w2),
        pl.BlockSpec((D_MODEL, LANES), w2),
        pl.BlockSpec((FM_ROPE_ROWS, D_MODEL), w2),
        pl.BlockSpec((FM_PLAIN_ROWS, D_MODEL), w2),
        pl.BlockSpec((GATE_ROWS, D_MODEL), w2),
        pl.BlockSpec((tm, TM_ROPE_COLS), lambda i, j: (j, 0)),
        pl.BlockSpec((tm, TM_ROPE_COLS), lambda i, j: (j, 0)),
        pl.BlockSpec((FM_ROPE_ROWS, tm), lambda i, j: (0, j)),
        pl.BlockSpec((FM_ROPE_ROWS, tm), lambda i, j: (0, j)),
        pl.BlockSpec((1, LANES), w2),
    ]

    def tok_out(width, dtype=BF16):
        return pl.BlockSpec((1, tm, width), tok), jax.ShapeDtypeStruct((b, s, width), dtype)

    def fm_out(rows, dtype=BF16):
        return pl.BlockSpec((1, rows, tm), fm), jax.ShapeDtypeStruct((b, rows, s), dtype)

    outs = [
        tok_out(LANES), tok_out(LANES), tok_out(DIFF_WIDTH), tok_out(FOX_WIDTH),
        tok_out(HEAD_DIM), tok_out(HEAD_DIM), tok_out(LANES, F32),
        fm_out(NSA_WIDTH), fm_out(DIFF_WIDTH),
        fm_out(HEAD_DIM), fm_out(HEAD_DIM), fm_out(DIFF_WIDTH),
        fm_out(FOX_WIDTH), fm_out(FOX_WIDTH), fm_out(GATE_ROWS, F32),
    ]
    return pl.pallas_call(
        functools.partial(_in_proj_kernel, tm=tm),
        grid=grid,
        in_specs=in_specs,
        out_specs=[o[0] for o in outs],
        out_shape=[o[1] for o in outs],
        compiler_params=_cparams("parallel", "parallel"),
        name="in_proj",
    )(x, wa, wb, wc, wd, we, wf, cosa, sina, cosd, sind, bc)


def _gelu_tanh(x):
    return 0.5 * x * (1.0 + jnp.tanh(math.sqrt(2.0 / math.pi) * (x + 0.044715 * (x * x * x))))


def _compress_kernel(k16_ref, v16_ref, pek_ref, pev_ref, wk1_ref, wk2_ref, wk2r_ref, wv1_ref, wv2t_ref,
                     cos_ref, sin_ref, kc_ref, vct_ref):
    def hidden(x16, pe_ref, w1_ref):
        top = _dot(x16, w1_ref[0])
        bot = _dot(x16, w1_ref[1])
        bias = _dot(pe_ref[0], w1_ref[0]) + _dot(pe_ref[1], w1_ref[1])
        bot = pltpu.roll(bot, N_CMP - 1, 0)
        return _gelu_tanh(top + bot + bias[0:1, :]).astype(BF16)

    hk = hidden(k16_ref[0], pek_ref, wk1_ref)
    kc = _dot(hk, wk2_ref[...]) * cos_ref[...] + _dot(hk, wk2r_ref[...]) * sin_ref[...]
    kc_ref[0] = kc.astype(BF16)
    hv = hidden(v16_ref[0], pev_ref, wv1_ref)
    vct_ref[0] = _dot_nt(wv2t_ref[...], hv).astype(BF16)


def _compress(k16, v16, pek, pev, wk1, wk2, wk2r, wv1, wv2t, cos_c, sin_c):
    b = k16.shape[0]
    half = CMP_STRIDE * HEAD_DIM
    bat = lambda i: (i, 0, 0)
    c2 = lambda i: (0, 0)
    c3 = lambda i: (0, 0, 0)
    in_specs = [
        pl.BlockSpec((1, N_CMP, half), bat),
        pl.BlockSpec((1, N_CMP, half), bat),
        pl.BlockSpec((2, 8, half), c3),
        pl.BlockSpec((2, 8, half), c3),
        pl.BlockSpec((2, half, CMP_HIDDEN), c3),
        pl.BlockSpec((CMP_HIDDEN, HEAD_DIM), c2),
        pl.BlockSpec((CMP_HIDDEN, HEAD_DIM), c2),
        pl.BlockSpec((2, half, CMP_HIDDEN), c3),
        pl.BlockSpec((HEAD_DIM, CMP_HIDDEN), c2),
        pl.BlockSpec((N_CMP, HEAD_DIM), c2),
        pl.BlockSpec((N_CMP, HEAD_DIM), c2),
    ]
    return pl.pallas_call(
        _compress_kernel,
        grid=(b,),
        in_specs=in_specs,
        out_specs=[pl.BlockSpec((1, N_CMP, HEAD_DIM), bat), pl.BlockSpec((1, HEAD_DIM, N_CMP), bat)],
        out_shape=[jax.ShapeDtypeStruct((b, N_CMP, HEAD_DIM), BF16), jax.ShapeDtypeStruct((b, HEAD_DIM, N_CMP), BF16)],
        compiler_params=_cparams("parallel"),
        name="nsa_compress",
    )(k16, v16, pek, pev, wk1, wk2, wk2r, wv1, wv2t, cos_c, sin_c)


def _key_minus_query(tk, tq, k0, q0):
    d = lax.broadcasted_iota(jnp.int32, (tk, tq), 0) - lax.broadcasted_iota(jnp.int32, (tk, tq), 1)
    return d, q0 - k0


def _causal_mask(tk, q0):
    def mask(i, k0, diagonal, q_lo, width):
        if not diagonal:
            return None
        d, off = _key_minus_query(tk, width, k0, q0 + q_lo)
        return d <= off
    return mask


def _head_rows(h, width):
    return slice(h * width, (h + 1) * width)


def _attend_t(n, tq, tk, qi, lo, scores, values, acc_ref, sbuf_ref, *, mask, n_early=None, join=None):
    n_early = n if n_early is None else n_early
    ahead = min(QK_AHEAD, n_early)
    per_tile = tq // tk
    full = slice(0, tq)
    acc_ref[...] = jnp.zeros(acc_ref.shape, F32)
    k_lo = pl.multiple_of(lo * tk, tk)
    for i in range(ahead):
        sbuf_ref[i] = scores(i, k_lo, full)
    ones = jnp.ones((BF16_SUBLANES, tk), BF16)

    def step(c, heads, diag, ms, ls):
        diagonal = diag is not None
        q_lo = diag * tk if diagonal else 0
        cols = slice(q_lo, tq)
        nxt = slice(q_lo + tk, tq) if diagonal else full
        k0 = pl.multiple_of(c * tk, tk)
        k1 = pl.multiple_of(c * tk + tk, tk)
        ms, ls = list(ms), list(ls)
        pend = [sbuf_ref[i, :, cols] for i in range(ahead)]
        for i in range(heads):
            s = pend.pop(0)
            j = i + ahead
            if j < heads:
                pend.append(scores(j, k0, cols))
            elif nxt.start < tq:
                sbuf_ref[j - heads, :, nxt] = scores(j - heads, k1, nxt)
            mk = mask(i, k0, diagonal, q_lo, tq - q_lo)
            if mk is not None:
                s = jnp.where(mk, s, NEG_BIG)
            m_old, l_old = ms[i][:, cols], ls[i][:, cols]
            m_new = jnp.maximum(m_old, jnp.max(s, axis=0, keepdims=True))
            a = jnp.exp2(m_old - m_new)
            p = jnp.exp2(s - m_new)
            pv = _dot(jnp.concatenate([values(i, k0), ones], axis=0), p.astype(BF16))
            rows = _head_rows(i, HEAD_DIM)
            acc_ref[rows, cols] = a * acc_ref[rows, cols] + pv[0:HEAD_DIM]
            l_new = a * l_old + pv[HEAD_DIM:HEAD_DIM + 1]
            if q_lo:
                m_new = jnp.concatenate([ms[i][:, 0:q_lo], m_new], axis=1)
                l_new = jnp.concatenate([ls[i][:, 0:q_lo], l_new], axis=1)
            ms[i], ls[i] = m_new, l_new
        return tuple(ms), tuple(ls)

    def init(count):
        return (tuple(jnp.full((1, tq), NEG_BIG, F32) for _ in range(count)),
                tuple(jnp.zeros((1, tq), F32) for _ in range(count)))

    first_diag = qi * per_tile
    ms, ls = init(n_early)
    if n_early < n:
        ms, ls = lax.fori_loop(lo, join, lambda c, carry: step(c, n_early, None, *carry), (ms, ls))
        late = init(n - n_early)
        ms, ls, lo = ms + late[0], ls + late[1], join
    ms, ls = lax.fori_loop(lo, first_diag, lambda c, carry: step(c, n, None, *carry), (ms, ls))
    for d in range(per_tile):
        ms, ls = step(first_diag + d, n, d, ms, ls)
    return ls


def _pad_queries(q_t, qpad_ref, n, width):
    per = MXU_COLS // width
    row = lax.broadcasted_iota(jnp.int32, (MXU_COLS, q_t.shape[1]), 0)
    for i in range(n):
        g, r = divmod(i, per)
        qg = q_t[g * MXU_COLS:(g + 1) * MXU_COLS]
        qpad_ref[i] = jnp.where((row >= r * width) & (row < (r + 1) * width), qg, jnp.zeros_like(qg))


def _split3(x):
    hi = x.astype(BF16)
    r1 = x - hi.astype(F32)
    mid = r1.astype(BF16)
    lo = (r1 - mid.astype(F32)).astype(BF16)
    return hi, mid, lo


def _fox_kernel(qt_ref, k_ref, vt_ref, fls_ref, tri_ref, place_ref, o_ref,
                kaug_ref, crow_ref, qpad_ref, acc_ref, sbuf_ref, *, tq, tk, seq):
    qi = pl.program_id(1)
    q0 = pl.multiple_of(qi * tq, tq)
    n_q = FOX_GROUP_HEADS * HEAD_DIM
    n_b = MXU_COLS - n_q
    n_groups = FOX_QK_WIDTH // MXU_COLS

    @pl.when(qi == 0)
    def _():
        carry = jnp.zeros((1, LANES), F32)
        for i in range(seq // CUM_TILE):
            rows = slice(i * CUM_TILE, (i + 1) * CUM_TILE)
            c = carry + sum(_dot(tri_ref[...], part) for part in _split3(fls_ref[0, rows, :]))
            carry = c[CUM_TILE - 1:CUM_TILE, :]
            c = c * LOG2E
            crow_ref[:, rows] = c.T[0:FOX_HEADS, :]
            hi, mid, lo = _split3(c)
            bias = (_dot(hi, place_ref[0]) + _dot(mid, place_ref[1]) + _dot(lo, place_ref[2])
                    + place_ref[3, 0:1, :].astype(F32))
            k = k_ref[0, rows, :].astype(F32)
            pieces = []
            for g in range(n_groups):
                kg = k[:, g * n_q:min((g + 1) * n_q, FOX_WIDTH)]
                pieces.append(kg)
                if kg.shape[1] < n_q:
                    pieces.append(jnp.zeros((CUM_TILE, n_q - kg.shape[1]), F32))
                pieces.append(bias[:, g * n_b:(g + 1) * n_b])
            kaug_ref[rows, :] = jnp.concatenate(pieces, axis=1).astype(BF16)

    brow = lax.broadcasted_iota(jnp.int32, (n_b, 1), 0)
    for h in range(FOX_HEADS):
        r = h % FOX_GROUP_HEADS
        bias = jnp.where((brow >= 8 * r) & (brow < 8 * r + 3), 1.0, 0.0)
        for j, part in enumerate(_split3(crow_ref[h:h + 1, pl.ds(q0, tq)])):
            bias = jnp.where(brow == 8 * r + 3 + j, part.astype(F32), bias)
        zero = lambda heads: [jnp.zeros((heads * HEAD_DIM, tq), BF16)] if heads else []
        qpad_ref[h] = jnp.concatenate(zero(r) + [qt_ref[0, _head_rows(h, HEAD_DIM), :]]
                                      + zero(FOX_GROUP_HEADS - 1 - r) + [bias.astype(BF16)], axis=0)

    ls = _attend_t(
        FOX_HEADS, tq, tk, qi, 0,
        lambda h, k0, cols: _dot(kaug_ref[pl.ds(k0, tk), _head_rows(h // FOX_GROUP_HEADS, MXU_COLS)],
                                 qpad_ref[h, :, cols]),
        lambda h, k0: vt_ref[0, _head_rows(h, HEAD_DIM), pl.ds(k0, tk)],
        acc_ref, sbuf_ref, mask=_causal_mask(tk, q0))
    for h in range(FOX_HEADS):
        rows = _head_rows(h, HEAD_DIM)
        acc_ref[rows, :] = acc_ref[rows, :] * (1.0 / ls[h])
    o_ref[0] = acc_ref[...].T.astype(BF16)


def _fox_attention(fq_t, fk, fv_t, fls, tri, place):
    b, w, s = fv_t.shape
    wk = FOX_QK_WIDTH
    tq, tk = ATT_Q_TILE, ATT_K_TILE
    return pl.pallas_call(
        functools.partial(_fox_kernel, tq=tq, tk=tk, seq=s),
        grid=(b, s // tq),
        in_specs=[
            pl.BlockSpec((1, w, tq), lambda i, j: (i, 0, j)),
            pl.BlockSpec((1, s, w), lambda i, j: (i, 0, 0)),
            pl.BlockSpec((1, w, s), lambda i, j: (i, 0, 0)),
            pl.BlockSpec((1, s, LANES), lambda i, j: (i, 0, 0)),
            pl.BlockSpec((CUM_TILE, CUM_TILE), lambda i, j: (0, 0)),
            pl.BlockSpec(place.shape, lambda i, j: (0, 0, 0)),
        ],
        out_specs=pl.BlockSpec((1, tq, w), lambda i, j: (i, j, 0)),
        out_shape=jax.ShapeDtypeStruct((b, s, w), BF16),
        scratch_shapes=[pltpu.VMEM((s, wk), BF16), pltpu.VMEM((FOX_HEADS, s), F32),
                        pltpu.VMEM((FOX_HEADS, MXU_COLS, tq), BF16), pltpu.VMEM((w, tq), F32),
                        pltpu.VMEM((QK_AHEAD, tk, tq), F32)],
        compiler_params=_cparams("parallel", "arbitrary"),
        name="fox_attention",
    )(fq_t, fk, fv_t, fls, tri, place)


def _diff_kernel(qt_ref, k_ref, vt_ref, lam_ref, g_ref, o_ref, qpad_ref, acc_ref, sbuf_ref, *, tq, tk, lambda_init):
    qi = pl.program_id(1)
    _pad_queries(qt_ref[0], qpad_ref, DIFF_MAPS, DIFF_QK_DIM)
    ls = _attend_t(
        DIFF_MAPS, tq, tk, qi, 0,
        lambda i, k0, cols: _dot(k_ref[0, pl.ds(k0, tk), :], qpad_ref[i, :, cols]),
        lambda i, k0: vt_ref[0, _head_rows(i // 2, HEAD_DIM), pl.ds(k0, tk)],
        acc_ref, sbuf_ref, mask=_causal_mask(tk, qi * tq))
    lp = lam_ref[...]
    lam = (jnp.exp(jnp.sum(lp[0:1] * lp[1:2], axis=1, keepdims=True))
           - jnp.exp(jnp.sum(lp[2:3] * lp[3:4], axis=1, keepdims=True)) + lambda_init)
    heads = []
    for h in range(DIFF_HEADS):
        o1 = acc_ref[_head_rows(2 * h, HEAD_DIM), :] * (1.0 / ls[2 * h])
        o2 = acc_ref[_head_rows(2 * h + 1, HEAD_DIM), :] * (1.0 / ls[2 * h + 1])
        o = o1 - lam * o2
        o = o * lax.rsqrt(jnp.mean(o * o, axis=0, keepdims=True) + LN_EPS)
        heads.append(o * (1.0 - lambda_init))
    o_t = jnp.concatenate(heads, axis=0)
    o_ref[0] = (o_t.T * g_ref[...]).astype(BF16)


def _diff_attention(dq_t, dk, dv_t, lam_params, subln_g, lambda_init):
    b, w, s = dq_t.shape
    tq, tk = ATT_Q_TILE, ATT_K_TILE
    return pl.pallas_call(
        functools.partial(_diff_kernel, tq=tq, tk=tk, lambda_init=lambda_init),
        grid=(b, s // tq),
        in_specs=[
            pl.BlockSpec((1, w, tq), lambda i, j: (i, 0, j)),
            pl.BlockSpec((1, s, w), lambda i, j: (i, 0, 0)),
            pl.BlockSpec((1, w, s), lambda i, j: (i, 0, 0)),
            pl.BlockSpec((4, DIFF_QK_DIM), lambda i, j: (0, 0)),
            pl.BlockSpec((1, w), lambda i, j: (0, 0)),
        ],
        out_specs=pl.BlockSpec((1, tq, w), lambda i, j: (i, j, 0)),
        out_shape=jax.ShapeDtypeStruct((b, s, w), BF16),
        scratch_shapes=[pltpu.VMEM((DIFF_MAPS, MXU_COLS, tq), BF16), pltpu.VMEM((DIFF_MAPS * HEAD_DIM, tq), F32),
                        pltpu.VMEM((QK_AHEAD, tk, tq), F32)],
        compiler_params=_cparams("parallel", "arbitrary"),
        name="diff_attention",
    )(dq_t, dk, dv_t, lam_params, subln_g)


def _nsa_cmp_scores(qt_ref, kc_ref, t0, width):
    cols = slice(t0, t0 + width)
    return [_dot(kc_ref[0], qt_ref[0, _head_rows(h, HEAD_DIM), cols]) for h in range(NSA_HEADS)]


def _nsa_select(scores, vct_ref, g_ref, ovl_ref, sel_ref, ocmp_ref, t0, width):
    cols = slice(t0, t0 + width)
    t_q = t0 + lax.broadcasted_iota(jnp.int32, (1, width), 1)
    block_end = CMP_STRIDE * lax.broadcasted_iota(jnp.int32, (N_CMP, 1), 0) + (CMP_BLOCK - 1)
    vis = block_end <= t_q
    any_vis = t_q >= CMP_BLOCK - 1
    lhs = jnp.concatenate([vct_ref[0], ovl_ref[...], jnp.ones((BF16_SUBLANES, N_CMP), BF16)], axis=0)
    imp = jnp.zeros((N_SLC, width), F32)
    for h in range(NSA_HEADS):
        rows = _head_rows(h, HEAD_DIM)
        s = jnp.where(vis, scores[h], NEG_BIG)
        p = jnp.exp2(s - jnp.max(s, axis=0, keepdims=True))
        r = _dot(lhs, p.astype(BF16))
        l = r[HEAD_DIM + N_SLC:HEAD_DIM + N_SLC + 1]
        scale = jnp.where(any_vis, 1.0 / l, 0.0)
        ocmp_ref[rows, cols] = (g_ref[0, 3 * h:3 * h + 1, cols] * scale) * r[0:HEAD_DIM]
        imp = imp + scale * r[HEAD_DIM:HEAD_DIM + N_SLC]
    j_idx = lax.broadcasted_iota(jnp.int32, (N_SLC, 1), 0)
    blk_t = t_q // SLC_BLOCK
    forced = (j_idx == 0) | (j_idx == blk_t) | (j_idx == blk_t - 1)
    valid = j_idx * SLC_BLOCK <= t_q
    score = jnp.where(forced, 1e9, jnp.where(valid, imp, -1.0))
    groups = [score[g * 8:(g + 1) * 8] for g in range(N_SLC // 8)]
    ranks = [jnp.zeros((8, width), F32) for _ in groups]
    row = lax.broadcasted_iota(jnp.int32, (8, 1), 0)
    for i in range(N_SLC):
        gi, ri = divmod(i, 8)
        si = groups[gi][ri:ri + 1, :]
        for g, sg in enumerate(groups):
            if g < gi:
                ranks[g] = ranks[g] + jnp.where(si > sg, 1.0, 0.0)
            elif g > gi:
                ranks[g] = ranks[g] + jnp.where(si >= sg, 1.0, 0.0)
            else:
                tie = jnp.where(row > ri, 1.0, 0.0)
                ranks[g] = ranks[g] + jnp.where(si > sg, 1.0, 0.0) + jnp.where(si == sg, tie, 0.0)
    rank = jnp.concatenate(ranks, axis=0)
    sel_ref[:, cols] = jnp.where(rank < float(SLC_TOPK), 0.0, NEG_BIG).astype(BF16)


def _nsa_kernel(qt_ref, kc_ref, vct_ref, ks_ref, vst_ref, kw_ref, vwt_ref, g_ref, ovl_ref,
                o_ref, sel_ref, ocmp_ref, qaug_ref, acc_ref, out_ref, sbuf_ref, *, tq, tk, seq):
    qi = pl.program_id(1)
    q0 = pl.multiple_of(qi * tq, tq)

    @pl.when(qi == 0)
    def _():
        nxt = _nsa_cmp_scores(qt_ref, kc_ref, 0, NSA_PRE_TILE)
        for t0 in range(0, seq, NSA_PRE_TILE):
            scores = nxt
            if t0 + NSA_PRE_TILE < seq:
                nxt = _nsa_cmp_scores(qt_ref, kc_ref, t0 + NSA_PRE_TILE, NSA_PRE_TILE)
            _nsa_select(scores, vct_ref, g_ref, ovl_ref, sel_ref, ocmp_ref, t0, NSA_PRE_TILE)

    pad = jnp.zeros((LANES - HEAD_DIM - N_SLC, tq), BF16)
    sel_neg = sel_ref[:, pl.ds(q0, tq)]
    for h in range(NSA_HEADS):
        qaug_ref[h] = jnp.concatenate([qt_ref[0, _head_rows(h, HEAD_DIM), pl.ds(q0, tq)], sel_neg, pad], axis=0)

    def scores(i, k0, cols):
        k_ref = ks_ref if i < NSA_HEADS else kw_ref
        return _dot(k_ref[0, pl.ds(k0, tk), :], qaug_ref[i % NSA_HEADS, :, cols])

    def values(i, k0):
        vt_ref = vst_ref if i < NSA_HEADS else vwt_ref
        return vt_ref[0, :, pl.ds(k0, tk)]

    def mask(i, k0, diagonal, q_lo, width):
        d, off = _key_minus_query(tk, width, k0, q0 + q_lo)
        if diagonal:
            return d <= off
        return None if i < NSA_HEADS else d > off - WINDOW

    ls = _attend_t(2 * NSA_HEADS, tq, tk, qi, 0, scores, values, acc_ref, sbuf_ref, mask=mask,
                   n_early=NSA_HEADS, join=jnp.maximum(q0 - WINDOW, 0) // tk)
    for h in range(NSA_HEADS):
        rows = _head_rows(h, HEAD_DIM)
        out = ocmp_ref[rows, pl.ds(q0, tq)]
        for branch in (1, 2):
            i = (branch - 1) * NSA_HEADS + h
            gate = g_ref[0, 3 * h + branch:3 * h + branch + 1, pl.ds(q0, tq)]
            out = out + (gate * (1.0 / ls[i])) * acc_ref[_head_rows(i, HEAD_DIM), :]
        out_ref[rows, :] = out
    o_ref[0] = out_ref[...].T.astype(BF16)


def _nsa_attention(nq_t, kc, vc_t, kslc, vslc_t, kwin, vwin_t, gates_t, ovl):
    b, w, s = nq_t.shape
    tq, tk = NSA_Q_TILE, ATT_K_TILE
    seq = lambda i, j: (i, 0, 0)
    return pl.pallas_call(
        functools.partial(_nsa_kernel, tq=tq, tk=tk, seq=s),
        grid=(b, s // tq),
        in_specs=[
            pl.BlockSpec((1, w, s), seq),
            pl.BlockSpec((1, N_CMP, HEAD_DIM), seq),
            pl.BlockSpec((1, HEAD_DIM, N_CMP), seq),
            pl.BlockSpec((1, s, LANES), seq),
            pl.BlockSpec((1, HEAD_DIM, s), seq),
            pl.BlockSpec((1, s, LANES), seq),
            pl.BlockSpec((1, HEAD_DIM, s), seq),
            pl.BlockSpec((1, GATE_ROWS, s), seq),
            pl.BlockSpec((N_SLC, N_CMP), lambda i, j: (0, 0)),
        ],
        out_specs=pl.BlockSpec((1, tq, w), lambda i, j: (i, j, 0)),
        out_shape=jax.ShapeDtypeStruct((b, s, w), BF16),
        scratch_shapes=[pltpu.VMEM((N_SLC, s), BF16), pltpu.VMEM((w, s), F32),
                        pltpu.VMEM((NSA_HEADS, LANES, tq), BF16), pltpu.VMEM((2 * w, tq), F32),
                        pltpu.VMEM((w, tq), F32), pltpu.VMEM((QK_AHEAD, tk, tq), F32)],
        compiler_params=_cparams("parallel", "arbitrary"),
        name="nsa_attention",
    )(nq_t, kc, vc_t, kslc, vslc_t, kwin, vwin_t, gates_t, ovl)


def _out_ln_kernel(x_ref, on_ref, od_ref, of_ref, w_ref, lng_ref, lnb_ref, o_ref):
    o = jnp.concatenate([on_ref[...], od_ref[...], of_ref[...]], axis=1)
    o_ref[...] = _layer_norm(DEEPNORM_ALPHA * x_ref[...] + _dot(o, w_ref[...]), lng_ref[...], lnb_ref[...])


def _out_ln(x, o_nsa, o_diff, o_fox, w, ln_g, ln_b):
    t = x.shape[0]
    tm = 2 * ROW_TILE
    row = lambda i: (i, 0)
    c2 = lambda i: (0, 0)
    return pl.pallas_call(
        _out_ln_kernel,
        grid=(t // tm,),
        in_specs=[
            pl.BlockSpec((tm, D_MODEL), row),
            pl.BlockSpec((tm, NSA_WIDTH), row),
            pl.BlockSpec((tm, DIFF_WIDTH), row),
            pl.BlockSpec((tm, FOX_WIDTH), row),
            pl.BlockSpec((NSA_WIDTH + DIFF_WIDTH + FOX_WIDTH, D_MODEL), c2),
            pl.BlockSpec((1, D_MODEL), c2),
            pl.BlockSpec((1, D_MODEL), c2),
        ],
        out_specs=pl.BlockSpec((tm, D_MODEL), row),
        out_shape=jax.ShapeDtypeStruct((t, D_MODEL), F32),
        compiler_params=_cparams("parallel"),
        name="out_ln",
    )(x, o_nsa, o_diff, o_fox, w, ln_g, ln_b)


def _prep_ffn(wg, wu, wd):
    return wg.astype(BF16), wu.astype(BF16), wd.astype(BF16)


def _rot_cols(w, d):
    k, n = w.shape
    w = w.reshape(k, n // d, d)
    return jnp.concatenate([-w[..., d // 2:], w[..., :d // 2]], axis=-1).reshape(k, n)


def _rope_table(pos, d, signed=False):
    inv = ROPE_THETA ** (-jnp.arange(0, d, 2, dtype=F32) / d)
    ang = pos.astype(F32)[:, None] * inv[None, :]
    cos = jnp.concatenate([jnp.cos(ang), jnp.cos(ang)], axis=-1)
    sin = jnp.concatenate([-jnp.sin(ang) if signed else jnp.sin(ang), jnp.sin(ang)], axis=-1)
    return cos, sin


def _fox_placement():
    n_b = MXU_COLS - FOX_GROUP_HEADS * HEAD_DIM
    place = np.zeros((4, LANES, (FOX_QK_WIDTH // MXU_COLS) * n_b), np.float32)
    for h in range(FOX_HEADS):
        g, r = divmod(h, FOX_GROUP_HEADS)
        for j in range(3):
            place[j, h, g * n_b + 8 * r + j] = -1.0
            place[3, 0, g * n_b + 8 * r + 3 + j] = 1.0
    return jnp.asarray(place, dtype=BF16)


def _prep_in_proj(w_in, fox_b_f, s):
    o = 0
    nsa_q = w_in[:, o:o + NSA_WIDTH]; o += NSA_WIDTH
    kv = [w_in[:, o + i * HEAD_DIM:o + (i + 1) * HEAD_DIM] for i in range(6)]; o += 6 * HEAD_DIM
    k_cmp, v_cmp, k_slc, v_slc, k_win, v_win = kv
    nsa_g = w_in[:, o:o + N_NSA_GATES]; o += N_NSA_GATES
    diff_q = w_in[:, o:o + DIFF_WIDTH]; o += DIFF_WIDTH
    diff_k = w_in[:, o:o + DIFF_WIDTH]; o += DIFF_WIDTH
    diff_v = w_in[:, o:o + DIFF_WIDTH]; o += DIFF_WIDTH
    fox_q = w_in[:, o:o + FOX_WIDTH]; o += FOX_WIDTH
    fox_k = w_in[:, o:o + FOX_WIDTH]; o += FOX_WIDTH
    fox_v = w_in[:, o:o + FOX_WIDTH]; o += FOX_WIDTH
    fox_f = w_in[:, o:o + FOX_HEADS]
    zero = jnp.zeros((D_MODEL, LANES - HEAD_DIM), F32)

    wa = jnp.concatenate([k_slc, zero, k_win, zero, diff_k], axis=1)
    wb = jnp.concatenate([fox_k, k_cmp, v_cmp], axis=1)
    wc = jnp.concatenate([fox_f, jnp.zeros((D_MODEL, LANES - FOX_HEADS), F32)], axis=1)
    bc = jnp.concatenate([fox_b_f, jnp.zeros((LANES - FOX_HEADS,), F32)])[None, :]
    wd = jnp.concatenate([nsa_q, diff_q], axis=1).T
    we = jnp.concatenate([v_slc, v_win, diff_v, fox_q * (HEAD_DIM ** -0.5 * LOG2E), fox_v], axis=1).T
    wf = jnp.concatenate([nsa_g, jnp.zeros((D_MODEL, GATE_ROWS - N_NSA_GATES), F32)], axis=1).T

    pos = jnp.arange(s, dtype=jnp.int32)
    c64, s64 = _rope_table(pos, HEAD_DIM, signed=True)
    c32, s32 = _rope_table(pos, DIFF_QK_DIM, signed=True)
    nsa_scale = HEAD_DIM ** -0.5 * LOG2E
    diff_scale = DIFF_QK_DIM ** -0.5 * LOG2E

    def tm_table(t64, t32):
        return jnp.concatenate([t64, t64, t64, t64, jnp.tile(t32, (1, DIFF_MAPS))], axis=1)

    def fm_table(t64, t32):
        return jnp.concatenate([jnp.tile(t64, (1, NSA_HEADS)) * nsa_scale,
                                jnp.tile(t32, (1, DIFF_MAPS)) * diff_scale], axis=1).T

    bf = lambda w: w.astype(BF16)
    return (bf(wa), bf(wb), bf(wc), bf(wd), bf(we), bf(wf),
            tm_table(c64, c32), tm_table(s64, s32), fm_table(c64, c32), fm_table(s64, s32), bc)


def _prep_compress(pos_k, pos_v, phi_k1, phi_k2, phi_v1, phi_v2):
    half = CMP_STRIDE * HEAD_DIM

    def pe(p):
        return jnp.broadcast_to(p.reshape(2, 1, half), (2, 8, half)).astype(BF16)

    block_end = jnp.arange(N_CMP, dtype=jnp.int32) * CMP_STRIDE + (CMP_BLOCK - 1)
    cos_c, sin_c = _rope_table(block_end, HEAD_DIM)
    return (pe(pos_k), pe(pos_v), phi_k1.reshape(2, half, CMP_HIDDEN).astype(BF16), phi_k2.astype(BF16),
            _rot_cols(phi_k2, HEAD_DIM).astype(BF16), phi_v1.reshape(2, half, CMP_HIDDEN).astype(BF16),
            phi_v2.T.astype(BF16), cos_c, sin_c)


def _overlap_matrix(s):
    c0 = np.arange(N_CMP) * CMP_STRIDE
    s0 = np.arange(N_SLC) * SLC_BLOCK
    ovl = (c0[None, :] < s0[:, None] + SLC_BLOCK) & (c0[None, :] + CMP_BLOCK > s0[:, None])
    ovl[:, (s - CMP_BLOCK) // CMP_STRIDE + 1:] = False
    return jnp.asarray(ovl.astype(np.float32), dtype=BF16)


def kernel(x, p, ln_g, ln_b, ffn1_w_gate, ffn1_w_up, ffn1_w_down, ffn2_w_gate, ffn2_w_up, ffn2_w_down, w_in, fox_b_f, nsa_pos_k, nsa_pos_v, nsa_phi_k1, nsa_phi_k2, nsa_phi_v1, nsa_phi_v2, diff_lambda, diff_subln_g, w_out, ple_w_gate, ple_b_gate, ple_w_proj):
    b, s, _ = x.shape
    assert s // SLC_BLOCK == N_SLC and (s - CMP_BLOCK) // CMP_STRIDE + 1 <= N_CMP
    t = b * s
    ovl = _overlap_matrix(s)
    tri = jnp.asarray(np.tril(np.ones((CUM_TILE, CUM_TILE), np.float32)), dtype=BF16)
    place = _fox_placement()
    x = x.reshape(t, D_MODEL)
    for i in range(DEPTH):
        lambda_init = 0.8 - 0.6 * math.exp(-0.3 * i)
        lng = ln_g[i][:, None, :]
        lnb = ln_b[i][:, None, :]
        x = _ffn_ln(x, *_prep_ffn(ffn1_w_gate[i], ffn1_w_up[i], ffn1_w_down[i]), lng[0], lnb[0])
        proj = _in_proj(x.reshape(b, s, D_MODEL), *_prep_in_proj(w_in[i], fox_b_f[i], s))
        kslc, kwin, dk, fk, kcmp, vcmp, fls, nq_t, dq_t, vslc_t, vwin_t, dv_t, fq_t, fv_t, gates_t = proj
        half = CMP_STRIDE * HEAD_DIM
        kc, vc_t = _compress(kcmp.reshape(b, s // CMP_STRIDE, half), vcmp.reshape(b, s // CMP_STRIDE, half),
                             *_prep_compress(nsa_pos_k[i], nsa_pos_v[i], nsa_phi_k1[i], nsa_phi_k2[i],
                                             nsa_phi_v1[i], nsa_phi_v2[i]))
        o_nsa = _nsa_attention(nq_t, kc, vc_t, kslc, vslc_t, kwin, vwin_t, gates_t, ovl)
        o_diff = _diff_attention(dq_t, dk, dv_t, diff_lambda[i], jnp.tile(diff_subln_g[i], DIFF_HEADS)[None, :],
                                 lambda_init)
        o_fox = _fox_attention(fq_t, fk, fv_t, fls, tri, place)
        x = _out_ln(x, o_nsa.reshape(t, NSA_WIDTH), o_diff.reshape(t, DIFF_WIDTH), o_fox.reshape(t, FOX_WIDTH),
                    w_out[i].astype(BF16), lng[1], lnb[1])
        ple = (p.reshape(DEPTH, t, PLE_DIM), i, ple_w_gate[i].astype(BF16), ple_b_gate[i][None, :],
               ple_w_proj[i].astype(BF16))
        x = _ffn_ln(x, *_prep_ffn(ffn2_w_gate[i], ffn2_w_up[i], ffn2_w_down[i]), lng[2], lnb[2], ple=ple)
    return x.reshape(b, s, D_MODEL)
```

```python
import functools
import math

import numpy as np
import jax
import jax.numpy as jnp
from jax import lax
from jax.experimental import pallas as pl
from jax.experimental.pallas import tpu as pltpu

F32 = jnp.float32
BF16 = jnp.bfloat16

D_MODEL = 1024
HEAD_DIM = 64
NSA_HEADS = 4
DIFF_HEADS = 4
FOX_HEADS = 8
NSA_WIDTH = NSA_HEADS * HEAD_DIM
DIFF_WIDTH = DIFF_HEADS * HEAD_DIM
FOX_WIDTH = FOX_HEADS * HEAD_DIM
CMP_BLOCK = 32
CMP_STRIDE = 16
CMP_HIDDEN = 256
SLC_BLOCK = 64
SLC_TOPK = 16
WINDOW = 512
DIFF_QK_DIM = HEAD_DIM // 2
DIFF_MAPS = 2 * DIFF_HEADS
D_FF = 2752
PLE_DIM = 256
ROPE_THETA = 10000.0
LN_EPS = 1e-5
NEG_BIG = -1e30
DEPTH = 2
DEEPNORM_ALPHA = (2.0 * DEPTH) ** 0.25
LOG2E = math.log2(math.e)

LANES = 128
MXU_COLS = 256
BF16_SUBLANES = 16
VMEM_LIMIT_BYTES = 56 * 1024 * 1024

FF_CHUNK = 2 * MXU_COLS
FF_BOUNDS = tuple((c0, min(c0 + FF_CHUNK, D_FF)) for c0 in range(0, D_FF, FF_CHUNK))

STEP_ROWS = 1024
ROW_TILE = 512
ATT_Q_TILE = 512
ATT_K_TILE = 256
NSA_Q_TILE = 512
NSA_PRE_TILE = 512
CUM_TILE = 256
QK_AHEAD = 2

TM_ROPE_COLS = 2 * LANES + DIFF_WIDTH
TM_PLAIN_COLS = FOX_WIDTH + 2 * HEAD_DIM
FM_ROPE_ROWS = NSA_WIDTH + DIFF_WIDTH
FM_PLAIN_ROWS = 2 * HEAD_DIM + DIFF_WIDTH + 2 * FOX_WIDTH
FOX_GROUP_HEADS = 3
FOX_QK_WIDTH = -(-FOX_HEADS // FOX_GROUP_HEADS) * MXU_COLS
N_NSA_GATES = 3 * NSA_HEADS
GATE_ROWS = BF16_SUBLANES
N_CMP = 128
N_SLC = 32


def _cparams(*sem):
    return pltpu.CompilerParams(dimension_semantics=sem, vmem_limit_bytes=VMEM_LIMIT_BYTES)


def _dot(a, b):
    return jnp.dot(a, b, preferred_element_type=F32)


def _dot_nt(a, b):
    return lax.dot_general(a, b, (((1,), (1,)), ((), ())), preferred_element_type=F32)


def _sigmoid(z):
    return 1.0 / (1.0 + jnp.exp(-z))


def _layer_norm(y, g, b):
    mu = jnp.mean(y, axis=-1, keepdims=True)
    yc = y - mu
    var = jnp.mean(yc * yc, axis=-1, keepdims=True)
    return yc * lax.rsqrt(var + LN_EPS) * g + b


def _ffn_kernel(*refs, with_ple):
    if with_ple:
        (x_ref, wg_ref, wu_ref, wd_ref, lng_ref, lnb_ref, p_ref, pwg_ref, pbg_ref, pwp_ref,
         o_ref, xb_ref) = refs
    else:
        x_ref, wg_ref, wu_ref, wd_ref, lng_ref, lnb_ref, o_ref, xb_ref = refs
    xb_ref[...] = x_ref[...].astype(BF16)

    for r0 in range(0, x_ref.shape[0], ROW_TILE):
        rows = slice(r0, r0 + ROW_TILE)
        xb = xb_ref[rows, :]

        def gate_up(c):
            c0, c1 = FF_BOUNDS[c]
            return _dot(xb, wg_ref[:, c0:c1]), _dot(xb, wu_ref[:, c0:c1])

        nxt = gate_up(0)
        acc = None
        for c, (c0, c1) in enumerate(FF_BOUNDS):
            g, u = nxt
            if c + 1 < len(FF_BOUNDS):
                nxt = gate_up(c + 1)
            h = (g * _sigmoid(g)) * u
            part = _dot(h.astype(BF16), wd_ref[c0:c1, :])
            acc = part if acc is None else acc + part
        y = DEEPNORM_ALPHA * x_ref[rows, :] + 0.5 * acc
        z = _layer_norm(y, lng_ref[...], lnb_ref[...])
        if with_ple:
            gate = _sigmoid(_dot(z.astype(BF16), pwg_ref[...]) + pbg_ref[...])
            z = z + gate * _dot(p_ref[0, rows, :].astype(BF16), pwp_ref[...])
        o_ref[rows, :] = z


def _ffn_ln(x, wg, wu, wd, ln_g, ln_b, ple=None):
    t = x.shape[0]
    tm = STEP_ROWS
    row = lambda i: (i, 0)
    const = lambda shape: pl.BlockSpec(shape, lambda i: (0, 0), pipeline_mode=pl.Buffered(1))
    in_specs = [
        pl.BlockSpec((tm, D_MODEL), row),
        const((D_MODEL, D_FF)),
        const((D_MODEL, D_FF)),
        const((D_FF, D_MODEL)),
        const((1, D_MODEL)),
        const((1, D_MODEL)),
    ]
    args = [x, wg, wu, wd, ln_g, ln_b]
    if ple is not None:
        p, layer, pwg, pbg, pwp = ple
        in_specs += [
            pl.BlockSpec((1, tm, PLE_DIM), lambda i: (layer, i, 0)),
            const((D_MODEL, D_MODEL)),
            const((1, D_MODEL)),
            const((PLE_DIM, D_MODEL)),
        ]
        args += [p, pwg, pbg, pwp]
    return pl.pallas_call(
        functools.partial(_ffn_kernel, with_ple=ple is not None),
        grid=(t // tm,),
        in_specs=in_specs,
        out_specs=pl.BlockSpec((tm, D_MODEL), row),
        out_shape=jax.ShapeDtypeStruct((t, D_MODEL), F32),
        scratch_shapes=[pltpu.VMEM((tm, D_MODEL), BF16)],
        compiler_params=_cparams("parallel"),
        name="ffn_ln_ple" if ple is not None else "ffn_ln",
    )(*args)


def _swap_halves_lanes(x, d):
    w = x.shape[1]
    lane = lax.broadcasted_iota(jnp.int32, (1, w), 1)
    return jnp.where(lane % d < d // 2, pltpu.roll(x, w - d // 2, 1), pltpu.roll(x, d // 2, 1))


def _swap_halves_rows(x, d):
    parts = []
    for r0 in range(0, x.shape[0], d):
        parts += [x[r0 + d // 2:r0 + d], x[r0:r0 + d // 2]]
    return jnp.concatenate(parts, axis=0)


def _in_proj_kernel(x_ref, wa_ref, wb_ref, wc_ref, wd_ref, we_ref, wf_ref,
                    cosa_ref, sina_ref, cosd_ref, sind_ref, bc_ref,
                    kslc_ref, kwin_ref, dk_ref, fk_ref, kcmp_ref, vcmp_ref, fls_ref,
                    nq_ref, dq_ref, vslc_ref, vwin_ref, dv_ref, fq_ref, fv_ref, gate_ref, *, tm):
    xb = x_ref[0].astype(BF16)
    pa = _dot(xb, wa_ref[...])
    swapped = jnp.concatenate([_swap_halves_lanes(pa[:, 0:2 * LANES], HEAD_DIM),
                               _swap_halves_lanes(pa[:, 2 * LANES:], DIFF_QK_DIM)], axis=1)
    ra = pa * cosa_ref[...] + swapped * sina_ref[...]
    pos = pl.program_id(1) * tm + lax.broadcasted_iota(jnp.int32, (tm, 1), 0)
    lane = lax.broadcasted_iota(jnp.int32, (1, LANES), 1)
    block_id = jnp.where(lane - HEAD_DIM == pos // SLC_BLOCK, 1.0, 0.0)
    kslc_ref[0] = (ra[:, 0:LANES] + block_id).astype(BF16)
    kwin_ref[0] = ra[:, LANES:2 * LANES].astype(BF16)
    dk_ref[0] = ra[:, 2 * LANES:].astype(BF16)
    pb = _dot(xb, wb_ref[...]).astype(BF16)
    fk_ref[0] = pb[:, 0:FOX_WIDTH]
    kcmp_ref[0] = pb[:, FOX_WIDTH:FOX_WIDTH + HEAD_DIM]
    vcmp_ref[0] = pb[:, FOX_WIDTH + HEAD_DIM:]
    z = _dot(xb, wc_ref[...]) + bc_ref[...]
    fls_ref[0] = jnp.minimum(z, 0.0) - jnp.log(1.0 + jnp.exp(-jnp.abs(z)))

    pd = _dot_nt(wd_ref[...], xb)
    swapped = jnp.concatenate([_swap_halves_rows(pd[0:NSA_WIDTH], HEAD_DIM),
                               _swap_halves_rows(pd[NSA_WIDTH:], DIFF_QK_DIM)], axis=0)
    rd = pd * cosd_ref[...] + swapped * sind_ref[...]
    nq_ref[0] = rd[0:NSA_WIDTH].astype(BF16)
    dq_ref[0] = rd[NSA_WIDTH:].astype(BF16)
    pe = _dot_nt(we_ref[...], xb).astype(BF16)
    vslc_ref[0] = pe[0:HEAD_DIM]
    vwin_ref[0] = pe[HEAD_DIM:2 * HEAD_DIM]
    off = 2 * HEAD_DIM
    dv_ref[0] = pe[off:off + DIFF_WIDTH]
    off += DIFF_WIDTH
    fq_ref[0] = pe[off:off + FOX_WIDTH]
    off += FOX_WIDTH
    fv_ref[0] = pe[off:off + FOX_WIDTH]
    gate_ref[0] = _sigmoid(_dot_nt(wf_ref[...], xb))


def _in_proj(x, wa, wb, wc, wd, we, wf, cosa, sina, cosd, sind, bc):
    b, s, _ = x.shape
    tm = STEP_ROWS
    grid = (b, s // tm)
    w2 = lambda i, j: (0, 0)
    tok = lambda i, j: (i, j, 0)
    fm = lambda i, j: (i, 0, j)
    in_specs = [
        pl.BlockSpec((1, tm, D_MODEL), tok),
        pl.BlockSpec((D_MODEL, TM_ROPE_COLS), w2),
        pl.BlockSpec((D_MODEL, TM_PLAIN_COLS), w2),
        pl.BlockSpec((D_MODEL, LANES), w2),
        pl.BlockSpec((FM_ROPE_ROWS, D_MODEL), w2),
        pl.BlockSpec((FM_PLAIN_ROWS, D_MODEL), w2),
        pl.BlockSpec((GATE_ROWS, D_MODEL), w2),
        pl.BlockSpec((tm, TM_ROPE_COLS), lambda i, j: (j, 0)),
        pl.BlockSpec((tm, TM_ROPE_COLS), lambda i, j: (j, 0)),
        pl.BlockSpec((FM_ROPE_ROWS, tm), lambda i, j: (0, j)),
        pl.BlockSpec((FM_ROPE_ROWS, tm), lambda i, j: (0, j)),
        pl.BlockSpec((1, LANES), w2),
    ]

    def tok_out(width, dtype=BF16):
        return pl.BlockSpec((1, tm, width), tok), jax.ShapeDtypeStruct((b, s, width), dtype)

    def fm_out(rows, dtype=BF16):
        return pl.BlockSpec((1, rows, tm), fm), jax.ShapeDtypeStruct((b, rows, s), dtype)

    outs = [
        tok_out(LANES), tok_out(LANES), tok_out(DIFF_WIDTH), tok_out(FOX_WIDTH),
        tok_out(HEAD_DIM), tok_out(HEAD_DIM), tok_out(LANES, F32),
        fm_out(NSA_WIDTH), fm_out(DIFF_WIDTH),
        fm_out(HEAD_DIM), fm_out(HEAD_DIM), fm_out(DIFF_WIDTH),
        fm_out(FOX_WIDTH), fm_out(FOX_WIDTH), fm_out(GATE_ROWS, F32),
    ]
    return pl.pallas_call(
        functools.partial(_in_proj_kernel, tm=tm),
        grid=grid,
        in_specs=in_specs,
        out_specs=[o[0] for o in outs],
        out_shape=[o[1] for o in outs],
        compiler_params=_cparams("parallel", "parallel"),
        name="in_proj",
    )(x, wa, wb, wc, wd, we, wf, cosa, sina, cosd, sind, bc)


def _gelu_tanh(x):
    return 0.5 * x * (1.0 + jnp.tanh(math.sqrt(2.0 / math.pi) * (x + 0.044715 * (x * x * x))))


def _compress_kernel(k16_ref, v16_ref, pek_ref, pev_ref, wk1_ref, wk2_ref, wk2r_ref, wv1_ref, wv2t_ref,
                     cos_ref, sin_ref, kc_ref, vct_ref):
    def hidden(x16, pe_ref, w1_ref):
        top = _dot(x16, w1_ref[0])
        bot = _dot(x16, w1_ref[1])
        bias = _dot(pe_ref[0], w1_ref[0]) + _dot(pe_ref[1], w1_ref[1])
        bot = pltpu.roll(bot, N_CMP - 1, 0)
        return _gelu_tanh(top + bot + bias[0:1, :]).astype(BF16)

    hk = hidden(k16_ref[0], pek_ref, wk1_ref)
    kc = _dot(hk, wk2_ref[...]) * cos_ref[...] + _dot(hk, wk2r_ref[...]) * sin_ref[...]
    kc_ref[0] = kc.astype(BF16)
    hv = hidden(v16_ref[0], pev_ref, wv1_ref)
    vct_ref[0] = _dot_nt(wv2t_ref[...], hv).astype(BF16)


def _compress(k16, v16, pek, pev, wk1, wk2, wk2r, wv1, wv2t, cos_c, sin_c):
    b = k16.shape[0]
    half = CMP_STRIDE * HEAD_DIM
    bat = lambda i: (i, 0, 0)
    c2 = lambda i: (0, 0)
    c3 = lambda i: (0, 0, 0)
    in_specs = [
        pl.BlockSpec((1, N_CMP, half), bat),
        pl.BlockSpec((1, N_CMP, half), bat),
        pl.BlockSpec((2, 8, half), c3),
        pl.BlockSpec((2, 8, half), c3),
        pl.BlockSpec((2, half, CMP_HIDDEN), c3),
        pl.BlockSpec((CMP_HIDDEN, HEAD_DIM), c2),
        pl.BlockSpec((CMP_HIDDEN, HEAD_DIM), c2),
        pl.BlockSpec((2, half, CMP_HIDDEN), c3),
        pl.BlockSpec((HEAD_DIM, CMP_HIDDEN), c2),
        pl.BlockSpec((N_CMP, HEAD_DIM), c2),
        pl.BlockSpec((N_CMP, HEAD_DIM), c2),
    ]
    return pl.pallas_call(
        _compress_kernel,
        grid=(b,),
        in_specs=in_specs,
        out_specs=[pl.BlockSpec((1, N_CMP, HEAD_DIM), bat), pl.BlockSpec((1, HEAD_DIM, N_CMP), bat)],
        out_shape=[jax.ShapeDtypeStruct((b, N_CMP, HEAD_DIM), BF16), jax.ShapeDtypeStruct((b, HEAD_DIM, N_CMP), BF16)],
        compiler_params=_cparams("parallel"),
        name="nsa_compress",
    )(k16, v16, pek, pev, wk1, wk2, wk2r, wv1, wv2t, cos_c, sin_c)


def _key_minus_query(tk, tq, k0, q0):
    d = lax.broadcasted_iota(jnp.int32, (tk, tq), 0) - lax.broadcasted_iota(jnp.int32, (tk, tq), 1)
    return d, q0 - k0


def _causal_mask(tk, q0):
    def mask(i, k0, diagonal, q_lo, width):
        if not diagonal:
            return None
        d, off = _key_minus_query(tk, width, k0, q0 + q_lo)
        return d <= off
    return mask


def _head_rows(h, width):
    return slice(h * width, (h + 1) * width)


def _attend_t(n, tq, tk, qi, lo, scores, values, acc_ref, sbuf_ref, *, mask, n_early=None, join=None):
    n_early = n if n_early is None else n_early
    ahead = min(QK_AHEAD, n_early)
    per_tile = tq // tk
    full = slice(0, tq)
    acc_ref[...] = jnp.zeros(acc_ref.shape, F32)
    k_lo = pl.multiple_of(lo * tk, tk)
    for i in range(ahead):
        sbuf_ref[i] = scores(i, k_lo, full)
    ones = jnp.ones((BF16_SUBLANES, tk), BF16)

    def step(c, heads, diag, ms, ls):
        diagonal = diag is not None
        q_lo = diag * tk if diagonal else 0
        cols = slice(q_lo, tq)
        nxt = slice(q_lo + tk, tq) if diagonal else full
        k0 = pl.multiple_of(c * tk, tk)
        k1 = pl.multiple_of(c * tk + tk, tk)
        ms, ls = list(ms), list(ls)
        pend = [sbuf_ref[i, :, cols] for i in range(ahead)]
        for i in range(heads):
            s = pend.pop(0)
            j = i + ahead
            if j < heads:
                pend.append(scores(j, k0, cols))
            elif nxt.start < tq:
                sbuf_ref[j - heads, :, nxt] = scores(j - heads, k1, nxt)
            mk = mask(i, k0, diagonal, q_lo, tq - q_lo)
            if mk is not None:
                s = jnp.where(mk, s, NEG_BIG)
            m_old, l_old = ms[i][:, cols], ls[i][:, cols]
            m_new = jnp.maximum(m_old, jnp.max(s, axis=0, keepdims=True))
            a = jnp.exp2(m_old - m_new)
            p = jnp.exp2(s - m_new)
            pv = _dot(jnp.concatenate([values(i, k0), ones], axis=0), p.astype(BF16))
            rows = _head_rows(i, HEAD_DIM)
            acc_ref[rows, cols] = a * acc_ref[rows, cols] + pv[0:HEAD_DIM]
            l_new = a * l_old + pv[HEAD_DIM:HEAD_DIM + 1]
            if q_lo:
                m_new = jnp.concatenate([ms[i][:, 0:q_lo], m_new], axis=1)
                l_new = jnp.concatenate([ls[i][:, 0:q_lo], l_new], axis=1)
            ms[i], ls[i] = m_new, l_new
        return tuple(ms), tuple(ls)

    def init(count):
        return (tuple(jnp.full((1, tq), NEG_BIG, F32) for _ in range(count)),
                tuple(jnp.zeros((1, tq), F32) for _ in range(count)))

    first_diag = qi * per_tile
    ms, ls = init(n_early)
    if n_early < n:
        ms, ls = lax.fori_loop(lo, join, lambda c, carry: step(c, n_early, None, *carry), (ms, ls))
        late = init(n - n_early)
        ms, ls, lo = ms + late[0], ls + late[1], join
    ms, ls = lax.fori_loop(lo, first_diag, lambda c, carry: step(c, n, None, *carry), (ms, ls))
    for d in range(per_tile):
        ms, ls = step(first_diag + d, n, d, ms, ls)
    return ls


def _pad_queries(q_t, qpad_ref, n, width):
    per = MXU_COLS // width
    row = lax.broadcasted_iota(jnp.int32, (MXU_COLS, q_t.shape[1]), 0)
    for i in range(n):
        g, r = divmod(i, per)
        qg = q_t[g * MXU_COLS:(g + 1) * MXU_COLS]
        qpad_ref[i] = jnp.where((row >= r * width) & (row < (r + 1) * width), qg, jnp.zeros_like(qg))


def _split3(x):
    hi = x.astype(BF16)
    r1 = x - hi.astype(F32)
    mid = r1.astype(BF16)
    lo = (r1 - mid.astype(F32)).astype(BF16)
    return hi, mid, lo


def _fox_kernel(qt_ref, k_ref, vt_ref, fls_ref, tri_ref, place_ref, o_ref,
                kaug_ref, crow_ref, qpad_ref, acc_ref, sbuf_ref, *, tq, tk, seq):
    qi = pl.program_id(1)
    q0 = pl.multiple_of(qi * tq, tq)
    n_q = FOX_GROUP_HEADS * HEAD_DIM
    n_b = MXU_COLS - n_q
    n_groups = FOX_QK_WIDTH // MXU_COLS

    @pl.when(qi == 0)
    def _():
        carry = jnp.zeros((1, LANES), F32)
        for i in range(seq // CUM_TILE):
            rows = slice(i * CUM_TILE, (i + 1) * CUM_TILE)
            c = carry + sum(_dot(tri_ref[...], part) for part in _split3(fls_ref[0, rows, :]))
            carry = c[CUM_TILE - 1:CUM_TILE, :]
            c = c * LOG2E
            crow_ref[:, rows] = c.T[0:FOX_HEADS, :]
            hi, mid, lo = _split3(c)
            bias = (_dot(hi, place_ref[0]) + _dot(mid, place_ref[1]) + _dot(lo, place_ref[2])
                    + place_ref[3, 0:1, :].astype(F32))
            k = k_ref[0, rows, :].astype(F32)
            pieces = []
            for g in range(n_groups):
                kg = k[:, g * n_q:min((g + 1) * n_q, FOX_WIDTH)]
                pieces.append(kg)
                if kg.shape[1] < n_q:
                    pieces.append(jnp.zeros((CUM_TILE, n_q - kg.shape[1]), F32))
                pieces.append(bias[:, g * n_b:(g + 1) * n_b])
            kaug_ref[rows, :] = jnp.concatenate(pieces, axis=1).astype(BF16)

    brow = lax.broadcasted_iota(jnp.int32, (n_b, 1), 0)
    for h in range(FOX_HEADS):
        r = h % FOX_GROUP_HEADS
        bias = jnp.where((brow >= 8 * r) & (brow < 8 * r + 3), 1.0, 0.0)
        for j, part in enumerate(_split3(crow_ref[h:h + 1, pl.ds(q0, tq)])):
            bias = jnp.where(brow == 8 * r + 3 + j, part.astype(F32), bias)
        zero = lambda heads: [jnp.zeros((heads * HEAD_DIM, tq), BF16)] if heads else []
        qpad_ref[h] = jnp.concatenate(zero(r) + [qt_ref[0, _head_rows(h, HEAD_DIM), :]]
                                      + zero(FOX_GROUP_HEADS - 1 - r) + [bias.astype(BF16)], axis=0)

    ls = _attend_t(
        FOX_HEADS, tq, tk, qi, 0,
        lambda h, k0, cols: _dot(kaug_ref[pl.ds(k0, tk), _head_rows(h // FOX_GROUP_HEADS, MXU_COLS)],
                                 qpad_ref[h, :, cols]),
        lambda h, k0: vt_ref[0, _head_rows(h, HEAD_DIM), pl.ds(k0, tk)],
        acc_ref, sbuf_ref, mask=_causal_mask(tk, q0))
    for h in range(FOX_HEADS):
        rows = _head_rows(h, HEAD_DIM)
        acc_ref[rows, :] = acc_ref[rows, :] * (1.0 / ls[h])
    o_ref[0] = acc_ref[...].T.astype(BF16)


def _fox_attention(fq_t, fk, fv_t, fls, tri, place):
    b, w, s = fv_t.shape
    wk = FOX_QK_WIDTH
    tq, tk = ATT_Q_TILE, ATT_K_TILE
    return pl.pallas_call(
        functools.partial(_fox_kernel, tq=tq, tk=tk, seq=s),
        grid=(b, s // tq),
        in_specs=[
            pl.BlockSpec((1, w, tq), lambda i, j: (i, 0, j)),
            pl.BlockSpec((1, s, w), lambda i, j: (i, 0, 0)),
            pl.BlockSpec((1, w, s), lambda i, j: (i, 0, 0)),
            pl.BlockSpec((1, s, LANES), lambda i, j: (i, 0, 0)),
            pl.BlockSpec((CUM_TILE, CUM_TILE), lambda i, j: (0, 0)),
            pl.BlockSpec(place.shape, lambda i, j: (0, 0, 0)),
        ],
        out_specs=pl.BlockSpec((1, tq, w), lambda i, j: (i, j, 0)),
        out_shape=jax.ShapeDtypeStruct((b, s, w), BF16),
        scratch_shapes=[pltpu.VMEM((s, wk), BF16), pltpu.VMEM((FOX_HEADS, s), F32),
                        pltpu.VMEM((FOX_HEADS, MXU_COLS, tq), BF16), pltpu.VMEM((w, tq), F32),
                        pltpu.VMEM((QK_AHEAD, tk, tq), F32)],
        compiler_params=_cparams("parallel", "arbitrary"),
        name="fox_attention",
    )(fq_t, fk, fv_t, fls, tri, place)


def _diff_kernel(qt_ref, k_ref, vt_ref, lam_ref, g_ref, o_ref, qpad_ref, acc_ref, sbuf_ref, *, tq, tk, lambda_init):
    qi = pl.program_id(1)
    _pad_queries(qt_ref[0], qpad_ref, DIFF_MAPS, DIFF_QK_DIM)
    ls = _attend_t(
        DIFF_MAPS, tq, tk, qi, 0,
        lambda i, k0, cols: _dot(k_ref[0, pl.ds(k0, tk), :], qpad_ref[i, :, cols]),
        lambda i, k0: vt_ref[0, _head_rows(i // 2, HEAD_DIM), pl.ds(k0, tk)],
        acc_ref, sbuf_ref, mask=_causal_mask(tk, qi * tq))
    lp = lam_ref[...]
    lam = (jnp.exp(jnp.sum(lp[0:1] * lp[1:2], axis=1, keepdims=True))
           - jnp.exp(jnp.sum(lp[2:3] * lp[3:4], axis=1, keepdims=True)) + lambda_init)
    heads = []
    for h in range(DIFF_HEADS):
        o1 = acc_ref[_head_rows(2 * h, HEAD_DIM), :] * (1.0 / ls[2 * h])
        o2 = acc_ref[_head_rows(2 * h + 1, HEAD_DIM), :] * (1.0 / ls[2 * h + 1])
        o = o1 - lam * o2
        o = o * lax.rsqrt(jnp.mean(o * o, axis=0, keepdims=True) + LN_EPS)
        heads.append(o * (1.0 - lambda_init))
    o_t = jnp.concatenate(heads, axis=0)
    o_ref[0] = (o_t.T * g_ref[...]).astype(BF16)


def _diff_attention(dq_t, dk, dv_t, lam_params, subln_g, lambda_init):
    b, w, s = dq_t.shape
    tq, tk = ATT_Q_TILE, ATT_K_TILE
    return pl.pallas_call(
        functools.partial(_diff_kernel, tq=tq, tk=tk, lambda_init=lambda_init),
        grid=(b, s // tq),
        in_specs=[
            pl.BlockSpec((1, w, tq), lambda i, j: (i, 0, j)),
            pl.BlockSpec((1, s, w), lambda i, j: (i, 0, 0)),
            pl.BlockSpec((1, w, s), lambda i, j: (i, 0, 0)),
            pl.BlockSpec((4, DIFF_QK_DIM), lambda i, j: (0, 0)),
            pl.BlockSpec((1, w), lambda i, j: (0, 0)),
        ],
        out_specs=pl.BlockSpec((1, tq, w), lambda i, j: (i, j, 0)),
        out_shape=jax.ShapeDtypeStruct((b, s, w), BF16),
        scratch_shapes=[pltpu.VMEM((DIFF_MAPS, MXU_COLS, tq), BF16), pltpu.VMEM((DIFF_MAPS * HEAD_DIM, tq), F32),
                        pltpu.VMEM((QK_AHEAD, tk, tq), F32)],
        compiler_params=_cparams("parallel", "arbitrary"),
        name="diff_attention",
    )(dq_t, dk, dv_t, lam_params, subln_g)


def _nsa_cmp_scores(qt_ref, kc_ref, t0, width):
    cols = slice(t0, t0 + width)
    return [_dot(kc_ref[0], qt_ref[0, _head_rows(h, HEAD_DIM), cols]) for h in range(NSA_HEADS)]


def _nsa_select(scores, vct_ref, g_ref, ovl_ref, sel_ref, ocmp_ref, t0, width):
    cols = slice(t0, t0 + width)
    t_q = t0 + lax.broadcasted_iota(jnp.int32, (1, width), 1)
    block_end = CMP_STRIDE * lax.broadcasted_iota(jnp.int32, (N_CMP, 1), 0) + (CMP_BLOCK - 1)
    vis = block_end <= t_q
    any_vis = t_q >= CMP_BLOCK - 1
    lhs = jnp.concatenate([vct_ref[0], ovl_ref[...], jnp.ones((BF16_SUBLANES, N_CMP), BF16)], axis=0)
    imp = jnp.zeros((N_SLC, width), F32)
    for h in range(NSA_HEADS):
        rows = _head_rows(h, HEAD_DIM)
        s = jnp.where(vis, scores[h], NEG_BIG)
        p = jnp.exp2(s - jnp.max(s, axis=0, keepdims=True))
        r = _dot(lhs, p.astype(BF16))
        l = r[HEAD_DIM + N_SLC:HEAD_DIM + N_SLC + 1]
        scale = jnp.where(any_vis, 1.0 / l, 0.0)
        ocmp_ref[rows, cols] = (g_ref[0, 3 * h:3 * h + 1, cols] * scale) * r[0:HEAD_DIM]
        imp = imp + scale * r[HEAD_DIM:HEAD_DIM + N_SLC]
    j_idx = lax.broadcasted_iota(jnp.int32, (N_SLC, 1), 0)
    blk_t = t_q // SLC_BLOCK
    forced = (j_idx == 0) | (j_idx == blk_t) | (j_idx == blk_t - 1)
    valid = j_idx * SLC_BLOCK <= t_q
    score = jnp.where(forced, 1e9, jnp.where(valid, imp, -1.0))
    groups = [score[g * 8:(g + 1) * 8] for g in range(N_SLC // 8)]
    ranks = [jnp.zeros((8, width), F32) for _ in groups]
    row = lax.broadcasted_iota(jnp.int32, (8, 1), 0)
    for i in range(N_SLC):
        gi, ri = divmod(i, 8)
        si = groups[gi][ri:ri + 1, :]
        for g, sg in enumerate(groups):
            if g < gi:
                ranks[g] = ranks[g] + jnp.where(si > sg, 1.0, 0.0)
            elif g > gi:
                ranks[g] = ranks[g] + jnp.where(si >= sg, 1.0, 0.0)
            else:
                tie = jnp.where(row > ri, 1.0, 0.0)
                ranks[g] = ranks[g] + jnp.where(si > sg, 1.0, 0.0) + jnp.where(si == sg, tie, 0.0)
    rank = jnp.concatenate(ranks, axis=0)
    sel_ref[:, cols] = jnp.where(rank < float(SLC_TOPK), 0.0, NEG_BIG).astype(BF16)


def _nsa_kernel(qt_ref, kc_ref, vct_ref, ks_ref, vst_ref, kw_ref, vwt_ref, g_ref, ovl_ref,
                o_ref, sel_ref, ocmp_ref, qaug_ref, acc_ref, out_ref, sbuf_ref, *, tq, tk, seq):
    qi = pl.program_id(1)
    q0 = pl.multiple_of(qi * tq, tq)

    @pl.when(qi == 0)
    def _():
        nxt = _nsa_cmp_scores(qt_ref, kc_ref, 0, NSA_PRE_TILE)
        for t0 in range(0, seq, NSA_PRE_TILE):
            scores = nxt
            if t0 + NSA_PRE_TILE < seq:
                nxt = _nsa_cmp_scores(qt_ref, kc_ref, t0 + NSA_PRE_TILE, NSA_PRE_TILE)
            _nsa_select(scores, vct_ref, g_ref, ovl_ref, sel_ref, ocmp_ref, t0, NSA_PRE_TILE)

    pad = jnp.zeros((LANES - HEAD_DIM - N_SLC, tq), BF16)
    sel_neg = sel_ref[:, pl.ds(q0, tq)]
    for h in range(NSA_HEADS):
        qaug_ref[h] = jnp.concatenate([qt_ref[0, _head_rows(h, HEAD_DIM), pl.ds(q0, tq)], sel_neg, pad], axis=0)

    def scores(i, k0, cols):
        k_ref = ks_ref if i < NSA_HEADS else kw_ref
        return _dot(k_ref[0, pl.ds(k0, tk), :], qaug_ref[i % NSA_HEADS, :, cols])

    def values(i, k0):
        vt_ref = vst_ref if i < NSA_HEADS else vwt_ref
        return vt_ref[0, :, pl.ds(k0, tk)]

    def mask(i, k0, diagonal, q_lo, width):
        d, off = _key_minus_query(tk, width, k0, q0 + q_lo)
        if diagonal:
            return d <= off
        return None if i < NSA_HEADS else d > off - WINDOW

    ls = _attend_t(2 * NSA_HEADS, tq, tk, qi, 0, scores, values, acc_ref, sbuf_ref, mask=mask,
                   n_early=NSA_HEADS, join=jnp.maximum(q0 - WINDOW, 0) // tk)
    for h in range(NSA_HEADS):
        rows = _head_rows(h, HEAD_DIM)
        out = ocmp_ref[rows, pl.ds(q0, tq)]
        for branch in (1, 2):
            i = (branch - 1) * NSA_HEADS + h
            gate = g_ref[0, 3 * h + branch:3 * h + branch + 1, pl.ds(q0, tq)]
            out = out + (gate * (1.0 / ls[i])) * acc_ref[_head_rows(i, HEAD_DIM), :]
        out_ref[rows, :] = out
    o_ref[0] = out_ref[...].T.astype(BF16)


def _nsa_attention(nq_t, kc, vc_t, kslc, vslc_t, kwin, vwin_t, gates_t, ovl):
    b, w, s = nq_t.shape
    tq, tk = NSA_Q_TILE, ATT_K_TILE
    seq = lambda i, j: (i, 0, 0)
    return pl.pallas_call(
        functools.partial(_nsa_kernel, tq=tq, tk=tk, seq=s),
        grid=(b, s // tq),
        in_specs=[
            pl.BlockSpec((1, w, s), seq),
            pl.BlockSpec((1, N_CMP, HEAD_DIM), seq),
            pl.BlockSpec((1, HEAD_DIM, N_CMP), seq),
            pl.BlockSpec((1, s, LANES), seq),
            pl.BlockSpec((1, HEAD_DIM, s), seq),
            pl.BlockSpec((1, s, LANES), seq),
            pl.BlockSpec((1, HEAD_DIM, s), seq),
            pl.BlockSpec((1, GATE_ROWS, s), seq),
            pl.BlockSpec((N_SLC, N_CMP), lambda i, j: (0, 0)),
        ],
        out_specs=pl.BlockSpec((1, tq, w), lambda i, j: (i, j, 0)),
        out_shape=jax.ShapeDtypeStruct((b, s, w), BF16),
        scratch_shapes=[pltpu.VMEM((N_SLC, s), BF16), pltpu.VMEM((w, s), F32),
                        pltpu.VMEM((NSA_HEADS, LANES, tq), BF16), pltpu.VMEM((2 * w, tq), F32),
                        pltpu.VMEM((w, tq), F32), pltpu.VMEM((QK_AHEAD, tk, tq), F32)],
        compiler_params=_cparams("parallel", "arbitrary"),
        name="nsa_attention",
    )(nq_t, kc, vc_t, kslc, vslc_t, kwin, vwin_t, gates_t, ovl)


def _out_ln_kernel(x_ref, on_ref, od_ref, of_ref, w_ref, lng_ref, lnb_ref, o_ref):
    o = jnp.concatenate([on_ref[...], od_ref[...], of_ref[...]], axis=1)
    o_ref[...] = _layer_norm(DEEPNORM_ALPHA * x_ref[...] + _dot(o, w_ref[...]), lng_ref[...], lnb_ref[...])


def _out_ln(x, o_nsa, o_diff, o_fox, w, ln_g, ln_b):
    t = x.shape[0]
    tm = STEP_ROWS
    row = lambda i: (i, 0)
    c2 = lambda i: (0, 0)
    return pl.pallas_call(
        _out_ln_kernel,
        grid=(t // tm,),
        in_specs=[
            pl.BlockSpec((tm, D_MODEL), row),
            pl.BlockSpec((tm, NSA_WIDTH), row),
            pl.BlockSpec((tm, DIFF_WIDTH), row),
            pl.BlockSpec((tm, FOX_WIDTH), row),
            pl.BlockSpec((NSA_WIDTH + DIFF_WIDTH + FOX_WIDTH, D_MODEL), c2),
            pl.BlockSpec((1, D_MODEL), c2),
            pl.BlockSpec((1, D_MODEL), c2),
        ],
        out_specs=pl.BlockSpec((tm, D_MODEL), row),
        out_shape=jax.ShapeDtypeStruct((t, D_MODEL), F32),
        compiler_params=_cparams("parallel"),
        name="out_ln",
    )(x, o_nsa, o_diff, o_fox, w, ln_g, ln_b)


def _prep_ffn(wg, wu, wd):
    return wg.astype(BF16), wu.astype(BF16), wd.astype(BF16)


def _rot_cols(w, d):
    k, n = w.shape
    w = w.reshape(k, n // d, d)
    return jnp.concatenate([-w[..., d // 2:], w[..., :d // 2]], axis=-1).reshape(k, n)


def _rope_table(pos, d, signed=False):
    inv = ROPE_THETA ** (-jnp.arange(0, d, 2, dtype=F32) / d)
    ang = pos.astype(F32)[:, None] * inv[None, :]
    cos = jnp.concatenate([jnp.cos(ang), jnp.cos(ang)], axis=-1)
    sin = jnp.concatenate([-jnp.sin(ang) if signed else jnp.sin(ang), jnp.sin(ang)], axis=-1)
    return cos, sin


def _fox_placement():
    n_b = MXU_COLS - FOX_GROUP_HEADS * HEAD_DIM
    place = np.zeros((4, LANES, (FOX_QK_WIDTH // MXU_COLS) * n_b), np.float32)
    for h in range(FOX_HEADS):
        g, r = divmod(h, FOX_GROUP_HEADS)
        for j in range(3):
            place[j, h, g * n_b + 8 * r + j] = -1.0
            place[3, 0, g * n_b + 8 * r + 3 + j] = 1.0
    return jnp.asarray(place, dtype=BF16)


def _prep_in_proj(w_in, fox_b_f, s):
    o = 0
    nsa_q = w_in[:, o:o + NSA_WIDTH]; o += NSA_WIDTH
    kv = [w_in[:, o + i * HEAD_DIM:o + (i + 1) * HEAD_DIM] for i in range(6)]; o += 6 * HEAD_DIM
    k_cmp, v_cmp, k_slc, v_slc, k_win, v_win = kv
    nsa_g = w_in[:, o:o + N_NSA_GATES]; o += N_NSA_GATES
    diff_q = w_in[:, o:o + DIFF_WIDTH]; o += DIFF_WIDTH
    diff_k = w_in[:, o:o + DIFF_WIDTH]; o += DIFF_WIDTH
    diff_v = w_in[:, o:o + DIFF_WIDTH]; o += DIFF_WIDTH
    fox_q = w_in[:, o:o + FOX_WIDTH]; o += FOX_WIDTH
    fox_k = w_in[:, o:o + FOX_WIDTH]; o += FOX_WIDTH
    fox_v = w_in[:, o:o + FOX_WIDTH]; o += FOX_WIDTH
    fox_f = w_in[:, o:o + FOX_HEADS]
    zero = jnp.zeros((D_MODEL, LANES - HEAD_DIM), F32)

    wa = jnp.concatenate([k_slc, zero, k_win, zero, diff_k], axis=1)
    wb = jnp.concatenate([fox_k, k_cmp, v_cmp], axis=1)
    wc = jnp.concatenate([fox_f, jnp.zeros((D_MODEL, LANES - FOX_HEADS), F32)], axis=1)
    bc = jnp.concatenate([fox_b_f, jnp.zeros((LANES - FOX_HEADS,), F32)])[None, :]
    wd = jnp.concatenate([nsa_q, diff_q], axis=1).T
    we = jnp.concatenate([v_slc, v_win, diff_v, fox_q * (HEAD_DIM ** -0.5 * LOG2E), fox_v], axis=1).T
    wf = jnp.concatenate([nsa_g, jnp.zeros((D_MODEL, GATE_ROWS - N_NSA_GATES), F32)], axis=1).T

    pos = jnp.arange(s, dtype=jnp.int32)
    c64, s64 = _rope_table(pos, HEAD_DIM, signed=True)
    c32, s32 = _rope_table(pos, DIFF_QK_DIM, signed=True)
    nsa_scale = HEAD_DIM ** -0.5 * LOG2E
    diff_scale = DIFF_QK_DIM ** -0.5 * LOG2E

    def tm_table(t64, t32):
        return jnp.concatenate([t64, t64, t64, t64, jnp.tile(t32, (1, DIFF_MAPS))], axis=1)

    def fm_table(t64, t32):
        return jnp.concatenate([jnp.tile(t64, (1, NSA_HEADS)) * nsa_scale,
                                jnp.tile(t32, (1, DIFF_MAPS)) * diff_scale], axis=1).T

    bf = lambda w: w.astype(BF16)
    return (bf(wa), bf(wb), bf(wc), bf(wd), bf(we), bf(wf),
            tm_table(c64, c32), tm_table(s64, s32), fm_table(c64, c32), fm_table(s64, s32), bc)


def _prep_compress(pos_k, pos_v, phi_k1, phi_k2, phi_v1, phi_v2):
    half = CMP_STRIDE * HEAD_DIM

    def pe(p):
        return jnp.broadcast_to(p.reshape(2, 1, half), (2, 8, half)).astype(BF16)

    block_end = jnp.arange(N_CMP, dtype=jnp.int32) * CMP_STRIDE + (CMP_BLOCK - 1)
    cos_c, sin_c = _rope_table(block_end, HEAD_DIM)
    return (pe(pos_k), pe(pos_v), phi_k1.reshape(2, half, CMP_HIDDEN).astype(BF16), phi_k2.astype(BF16),
            _rot_cols(phi_k2, HEAD_DIM).astype(BF16), phi_v1.reshape(2, half, CMP_HIDDEN).astype(BF16),
            phi_v2.T.astype(BF16), cos_c, sin_c)


def _overlap_matrix(s):
    c0 = np.arange(N_CMP) * CMP_STRIDE
    s0 = np.arange(N_SLC) * SLC_BLOCK
    ovl = (c0[None, :] < s0[:, None] + SLC_BLOCK) & (c0[None, :] + CMP_BLOCK > s0[:, None])
    ovl[:, (s - CMP_BLOCK) // CMP_STRIDE + 1:] = False
    return jnp.asarray(ovl.astype(np.float32), dtype=BF16)


def kernel(x, p, ln_g, ln_b, ffn1_w_gate, ffn1_w_up, ffn1_w_down, ffn2_w_gate, ffn2_w_up, ffn2_w_down, w_in, fox_b_f, nsa_pos_k, nsa_pos_v, nsa_phi_k1, nsa_phi_k2, nsa_phi_v1, nsa_phi_v2, diff_lambda, diff_subln_g, w_out, ple_w_gate, ple_b_gate, ple_w_proj):
    b, s, _ = x.shape
    assert s // SLC_BLOCK == N_SLC and (s - CMP_BLOCK) // CMP_STRIDE + 1 <= N_CMP
    t = b * s
    ovl = _overlap_matrix(s)
    tri = jnp.asarray(np.tril(np.ones((CUM_TILE, CUM_TILE), np.float32)), dtype=BF16)
    place = _fox_placement()
    x = x.reshape(t, D_MODEL)
    for i in range(DEPTH):
        lambda_init = 0.8 - 0.6 * math.exp(-0.3 * i)
        lng = ln_g[i][:, None, :]
        lnb = ln_b[i][:, None, :]
        x = _ffn_ln(x, *_prep_ffn(ffn1_w_gate[i], ffn1_w_up[i], ffn1_w_down[i]), lng[0], lnb[0])
        proj = _in_proj(x.reshape(b, s, D_MODEL), *_prep_in_proj(w_in[i], fox_b_f[i], s))
        kslc, kwin, dk, fk, kcmp, vcmp, fls, nq_t, dq_t, vslc_t, vwin_t, dv_t, fq_t, fv_t, gates_t = proj
        half = CMP_STRIDE * HEAD_DIM
        kc, vc_t = _compress(kcmp.reshape(b, s // CMP_STRIDE, half), vcmp.reshape(b, s // CMP_STRIDE, half),
                             *_prep_compress(nsa_pos_k[i], nsa_pos_v[i], nsa_phi_k1[i], nsa_phi_k2[i],
                                             nsa_phi_v1[i], nsa_phi_v2[i]))
        o_nsa = _nsa_attention(nq_t, kc, vc_t, kslc, vslc_t, kwin, vwin_t, gates_t, ovl)
        o_diff = _diff_attention(dq_t, dk, dv_t, diff_lambda[i], jnp.tile(diff_subln_g[i], DIFF_HEADS)[None, :],
                                 lambda_init)
        o_fox = _fox_attention(fq_t, fk, fv_t, fls, tri, place)
        x = _out_ln(x, o_nsa.reshape(t, NSA_WIDTH), o_diff.reshape(t, DIFF_WIDTH), o_fox.reshape(t, FOX_WIDTH),
                    w_out[i].astype(BF16), lng[1], lnb[1])
        ple = (p.reshape(DEPTH, t, PLE_DIM), i, ple_w_gate[i].astype(BF16), ple_b_gate[i][None, :],
               ple_w_proj[i].astype(BF16))
        x = _ffn_ln(x, *_prep_ffn(ffn2_w_gate[i], ffn2_w_up[i], ffn2_w_down[i]), lng[2], lnb[2], ple=ple)
    return x.reshape(b, s, D_MODEL)
```

```python
import functools
import math

import numpy as np
import jax
import jax.numpy as jnp
from jax import lax
from jax.experimental import pallas as pl
from jax.experimental.pallas import tpu as pltpu

F32 = jnp.float32
BF16 = jnp.bfloat16

D_MODEL = 1024
HEAD_DIM = 64
NSA_HEADS = 4
DIFF_HEADS = 4
FOX_HEADS = 8
NSA_WIDTH = NSA_HEADS * HEAD_DIM
DIFF_WIDTH = DIFF_HEADS * HEAD_DIM
FOX_WIDTH = FOX_HEADS * HEAD_DIM
CMP_BLOCK = 32
CMP_STRIDE = 16
CMP_HIDDEN = 256
SLC_BLOCK = 64
SLC_TOPK = 16
WINDOW = 512
DIFF_QK_DIM = HEAD_DIM // 2
DIFF_MAPS = 2 * DIFF_HEADS
D_FF = 2752
PLE_DIM = 256
ROPE_THETA = 10000.0
LN_EPS = 1e-5
NEG_BIG = -1e30
DEPTH = 2
DEEPNORM_ALPHA = (2.0 * DEPTH) ** 0.25
LOG2E = math.log2(math.e)

LANES = 128
MXU_COLS = 256
BF16_SUBLANES = 16
VMEM_LIMIT_BYTES = 56 * 1024 * 1024

FF_CHUNK = 2 * MXU_COLS
FF_BOUNDS = tuple((c0, min(c0 + FF_CHUNK, D_FF)) for c0 in range(0, D_FF, FF_CHUNK))

STEP_ROWS = 1024
ROW_TILE = 512
ATT_Q_TILE = 512
ATT_K_TILE = 256
NSA_Q_TILE = 512
NSA_PRE_TILE = 512
CUM_TILE = 256
QK_AHEAD = 2

TM_ROPE_COLS = 2 * LANES + DIFF_WIDTH
TM_PLAIN_COLS = FOX_WIDTH + 2 * HEAD_DIM
FM_ROPE_ROWS = NSA_WIDTH + DIFF_WIDTH
FM_PLAIN_ROWS = 2 * HEAD_DIM + DIFF_WIDTH + 2 * FOX_WIDTH
FOX_GROUP_HEADS = 3
FOX_QK_WIDTH = -(-FOX_HEADS // FOX_GROUP_HEADS) * MXU_COLS
N_NSA_GATES = 3 * NSA_HEADS
GATE_ROWS = BF16_SUBLANES
N_CMP = 128
N_SLC = 32


def _cparams(*sem):
    return pltpu.CompilerParams(dimension_semantics=sem, vmem_limit_bytes=VMEM_LIMIT_BYTES)


def _dot(a, b):
    return jnp.dot(a, b, preferred_element_type=F32)


def _dot_nt(a, b):
    return lax.dot_general(a, b, (((1,), (1,)), ((), ())), preferred_element_type=F32)


def _sigmoid(z):
    return 1.0 / (1.0 + jnp.exp(-z))


def _layer_norm(y, g, b):
    mu = jnp.mean(y, axis=-1, keepdims=True)
    yc = y - mu
    var = jnp.mean(yc * yc, axis=-1, keepdims=True)
    return yc * lax.rsqrt(var + LN_EPS) * g + b


def _ffn_kernel(*refs, with_ple):
    if with_ple:
        (x_ref, wg_ref, wu_ref, wd_ref, lng_ref, lnb_ref, p_ref, pwg_ref, pbg_ref, pwp_ref,
         o_ref, xb_ref) = refs
    else:
        x_ref, wg_ref, wu_ref, wd_ref, lng_ref, lnb_ref, o_ref, xb_ref = refs
    xb_ref[...] = x_ref[...].astype(BF16)

    for r0 in range(0, x_ref.shape[0], ROW_TILE):
        rows = slice(r0, r0 + ROW_TILE)
        xb = xb_ref[rows, :]

        def gate_up(c):
            c0, c1 = FF_BOUNDS[c]
            return _dot(xb, wg_ref[:, c0:c1]), _dot(xb, wu_ref[:, c0:c1])

        nxt = gate_up(0)
        acc = None
        for c, (c0, c1) in enumerate(FF_BOUNDS):
            g, u = nxt
            if c + 1 < len(FF_BOUNDS):
                nxt = gate_up(c + 1)
            h = (g * _sigmoid(g)) * u
            part = _dot(h.astype(BF16), wd_ref[c0:c1, :])
            acc = part if acc is None else acc + part
        y = DEEPNORM_ALPHA * x_ref[rows, :] + 0.5 * acc
        z = _layer_norm(y, lng_ref[...], lnb_ref[...])
        if with_ple:
            gate = _sigmoid(_dot(z.astype(BF16), pwg_ref[...]) + pbg_ref[...])
            z = z + gate * _dot(p_ref[0, 0, rows, :].astype(BF16), pwp_ref[...])
        o_ref[rows, :] = z


def _ffn_ln(x, wg, wu, wd, ln_g, ln_b, ple=None):
    t = x.shape[0]
    tm = STEP_ROWS
    row = lambda i: (i, 0)
    const = lambda shape: pl.BlockSpec(shape, lambda i: (0, 0), pipeline_mode=pl.Buffered(1))
    in_specs = [
        pl.BlockSpec((tm, D_MODEL), row),
        const((D_MODEL, D_FF)),
        const((D_MODEL, D_FF)),
        const((D_FF, D_MODEL)),
        const((1, D_MODEL)),
        const((1, D_MODEL)),
    ]
    args = [x, wg, wu, wd, ln_g, ln_b]
    if ple is not None:
        p, layer, pwg, pbg, pwp = ple
        per_seq = p.shape[2] // tm
        in_specs += [
            pl.BlockSpec((1, 1, tm, PLE_DIM), lambda i: (layer, i // per_seq, i % per_seq, 0)),
            const((D_MODEL, D_MODEL)),
            const((1, D_MODEL)),
            const((PLE_DIM, D_MODEL)),
        ]
        args += [p, pwg, pbg, pwp]
    return pl.pallas_call(
        functools.partial(_ffn_kernel, with_ple=ple is not None),
        grid=(t // tm,),
        in_specs=in_specs,
        out_specs=pl.BlockSpec((tm, D_MODEL), row),
        out_shape=jax.ShapeDtypeStruct((t, D_MODEL), F32),
        scratch_shapes=[pltpu.VMEM((tm, D_MODEL), BF16)],
        compiler_params=_cparams("parallel"),
        name="ffn_ln_ple" if ple is not None else "ffn_ln",
    )(*args)


def _swap_halves_lanes(x, d):
    w = x.shape[1]
    lane = lax.broadcasted_iota(jnp.int32, (1, w), 1)
    return jnp.where(lane % d < d // 2, pltpu.roll(x, w - d // 2, 1), pltpu.roll(x, d // 2, 1))


def _swap_halves_rows(x, d):
    parts = []
    for r0 in range(0, x.shape[0], d):
        parts += [x[r0 + d // 2:r0 + d], x[r0:r0 + d // 2]]
    return jnp.concatenate(parts, axis=0)


def _in_proj_kernel(x_ref, wa_ref, wb_ref, wc_ref, wd_ref, we_ref, wf_ref,
                    cosa_ref, sina_ref, cosd_ref, sind_ref, bc_ref,
                    kslc_ref, kwin_ref, dk_ref, fk_ref, kcmp_ref, vcmp_ref, fls_ref,
                    nq_ref, dq_ref, vslc_ref, vwin_ref, dv_ref, fq_ref, fv_ref, gate_ref, *, tm):
    xb = x_ref[0].astype(BF16)
    pa = _dot(xb, wa_ref[...])
    swapped = jnp.concatenate([_swap_halves_lanes(pa[:, 0:2 * LANES], HEAD_DIM),
                               _swap_halves_lanes(pa[:, 2 * LANES:], DIFF_QK_DIM)], axis=1)
    ra = pa * cosa_ref[...] + swapped * sina_ref[...]
    pos = pl.program_id(1) * tm + lax.broadcasted_iota(jnp.int32, (tm, 1), 0)
    lane = lax.broadcasted_iota(jnp.int32, (1, LANES), 1)
    block_id = jnp.where(lane - HEAD_DIM == pos // SLC_BLOCK, 1.0, 0.0)
    kslc_ref[0] = (ra[:, 0:LANES] + block_id).astype(BF16)
    kwin_ref[0] = ra[:, LANES:2 * LANES].astype(BF16)
    dk_ref[0] = ra[:, 2 * LANES:].astype(BF16)
    pb = _dot(xb, wb_ref[...]).astype(BF16)
    fk_ref[0] = pb[:, 0:FOX_WIDTH]
    kcmp_ref[0] = pb[:, FOX_WIDTH:FOX_WIDTH + HEAD_DIM]
    vcmp_ref[0] = pb[:, FOX_WIDTH + HEAD_DIM:]
    z = _dot(xb, wc_ref[...]) + bc_ref[...]
    fls_ref[0] = jnp.minimum(z, 0.0) - jnp.log(1.0 + jnp.exp(-jnp.abs(z)))

    pd = _dot_nt(wd_ref[...], xb)
    swapped = jnp.concatenate([_swap_halves_rows(pd[0:NSA_WIDTH], HEAD_DIM),
                               _swap_halves_rows(pd[NSA_WIDTH:], DIFF_QK_DIM)], axis=0)
    rd = pd * cosd_ref[...] + swapped * sind_ref[...]
    nq_ref[0] = rd[0:NSA_WIDTH].astype(BF16)
    dq_ref[0] = rd[NSA_WIDTH:].astype(BF16)
    pe = _dot_nt(we_ref[...], xb).astype(BF16)
    vslc_ref[0] = pe[0:HEAD_DIM]
    vwin_ref[0] = pe[HEAD_DIM:2 * HEAD_DIM]
    off = 2 * HEAD_DIM
    dv_ref[0] = pe[off:off + DIFF_WIDTH]
    off += DIFF_WIDTH
    fq_ref[0] = pe[off:off + FOX_WIDTH]
    off += FOX_WIDTH
    fv_ref[0] = pe[off:off + FOX_WIDTH]
    gate_ref[0] = _sigmoid(_dot_nt(wf_ref[...], xb))


def _in_proj(x, wa, wb, wc, wd, we, wf, cosa, sina, cosd, sind, bc):
    b, s, _ = x.shape
    tm = STEP_ROWS
    grid = (b, s // tm)
    w2 = lambda i, j: (0, 0)
    tok = lambda i, j: (i, j, 0)
    fm = lambda i, j: (i, 0, j)
    in_specs = [
        pl.BlockSpec((1, tm, D_MODEL), tok),
        pl.BlockSpec((D_MODEL, TM_ROPE_COLS), w2),
        pl.BlockSpec((D_MODEL, TM_PLAIN_COLS), w2),
        pl.BlockSpec((D_MODEL, LANES), w2),
        pl.BlockSpec((FM_ROPE_ROWS, D_MODEL), w2),
        pl.BlockSpec((FM_PLAIN_ROWS, D_MODEL), w2),
        pl.BlockSpec((GATE_ROWS, D_MODEL), w2),
        pl.BlockSpec((tm, TM_ROPE_COLS), lambda i, j: (j, 0)),
        pl.BlockSpec((tm, TM_ROPE_COLS), lambda i, j: (j, 0)),
        pl.BlockSpec((FM_ROPE_ROWS, tm), lambda i, j: (0, j)),
        pl.BlockSpec((FM_ROPE_ROWS, tm), lambda i, j: (0, j)),
        pl.BlockSpec((1, LANES), w2),
    ]

    def tok_out(width, dtype=BF16):
        return pl.BlockSpec((1, tm, width), tok), jax.ShapeDtypeStruct((b, s, width), dtype)

    def fm_out(rows, dtype=BF16):
        return pl.BlockSpec((1, rows, tm), fm), jax.ShapeDtypeStruct((b, rows, s), dtype)

    outs = [
        tok_out(LANES), tok_out(LANES), tok_out(DIFF_WIDTH), tok_out(FOX_WIDTH),
        tok_out(HEAD_DIM), tok_out(HEAD_DIM), tok_out(LANES, F32),
        fm_out(NSA_WIDTH), fm_out(DIFF_WIDTH),
        fm_out(HEAD_DIM), fm_out(HEAD_DIM), fm_out(DIFF_WIDTH),
        fm_out(FOX_WIDTH), fm_out(FOX_WIDTH), fm_out(GATE_ROWS, F32),
    ]
    return pl.pallas_call(
        functools.partial(_in_proj_kernel, tm=tm),
        grid=grid,
        in_specs=in_specs,
        out_specs=[o[0] for o in outs],
        out_shape=[o[1] for o in outs],
        compiler_params=_cparams("parallel", "parallel"),
        name="in_proj",
    )(x, wa, wb, wc, wd, we, wf, cosa, sina, cosd, sind, bc)


def _gelu_tanh(x):
    return 0.5 * x * (1.0 + jnp.tanh(math.sqrt(2.0 / math.pi) * (x + 0.044715 * (x * x * x))))


def _compress_kernel(k16_ref, v16_ref, pek_ref, pev_ref, wk1_ref, wk2_ref, wk2r_ref, wv1_ref, wv2t_ref,
                     cos_ref, sin_ref, kc_ref, vct_ref):
    def hidden(x16, pe_ref, w1_ref):
        top = _dot(x16, w1_ref[0])
        bot = _dot(x16, w1_ref[1])
        bias = _dot(pe_ref[0], w1_ref[0]) + _dot(pe_ref[1], w1_ref[1])
        bot = pltpu.roll(bot, N_CMP - 1, 0)
        return _gelu_tanh(top + bot + bias[0:1, :]).astype(BF16)

    hk = hidden(k16_ref[0], pek_ref, wk1_ref)
    kc = _dot(hk, wk2_ref[...]) * cos_ref[...] + _dot(hk, wk2r_ref[...]) * sin_ref[...]
    kc_ref[0] = kc.astype(BF16)
    hv = hidden(v16_ref[0], pev_ref, wv1_ref)
    vct_ref[0] = _dot_nt(wv2t_ref[...], hv).astype(BF16)


def _compress(k16, v16, pek, pev, wk1, wk2, wk2r, wv1, wv2t, cos_c, sin_c):
    b = k16.shape[0]
    half = CMP_STRIDE * HEAD_DIM
    bat = lambda i: (i, 0, 0)
    c2 = lambda i: (0, 0)
    c3 = lambda i: (0, 0, 0)
    in_specs = [
        pl.BlockSpec((1, N_CMP, half), bat),
        pl.BlockSpec((1, N_CMP, half), bat),
        pl.BlockSpec((2, 8, half), c3),
        pl.BlockSpec((2, 8, half), c3),
        pl.BlockSpec((2, half, CMP_HIDDEN), c3),
        pl.BlockSpec((CMP_HIDDEN, HEAD_DIM), c2),
        pl.BlockSpec((CMP_HIDDEN, HEAD_DIM), c2),
        pl.BlockSpec((2, half, CMP_HIDDEN), c3),
        pl.BlockSpec((HEAD_DIM, CMP_HIDDEN), c2),
        pl.BlockSpec((N_CMP, HEAD_DIM), c2),
        pl.BlockSpec((N_CMP, HEAD_DIM), c2),
    ]
    return pl.pallas_call(
        _compress_kernel,
        grid=(b,),
        in_specs=in_specs,
        out_specs=[pl.BlockSpec((1, N_CMP, HEAD_DIM), bat), pl.BlockSpec((1, HEAD_DIM, N_CMP), bat)],
        out_shape=[jax.ShapeDtypeStruct((b, N_CMP, HEAD_DIM), BF16), jax.ShapeDtypeStruct((b, HEAD_DIM, N_CMP), BF16)],
        compiler_params=_cparams("parallel"),
        name="nsa_compress",
    )(k16, v16, pek, pev, wk1, wk2, wk2r, wv1, wv2t, cos_c, sin_c)


def _key_minus_query(tk, tq, k0, q0):
    d = lax.broadcasted_iota(jnp.int32, (tk, tq), 0) - lax.broadcasted_iota(jnp.int32, (tk, tq), 1)
    return d, q0 - k0


def _causal_mask(tk, q0):
    def mask(i, k0, diagonal, q_lo, width):
        if not diagonal:
            return None
        d, off = _key_minus_query(tk, width, k0, q0 + q_lo)
        return d <= off
    return mask


def _head_rows(h, width):
    return slice(h * width, (h + 1) * width)


def _attend_t(n, tq, tk, qi, lo, scores, values, acc_ref, sbuf_ref, *, mask, n_early=None, join=None):
    n_early = n if n_early is None else n_early
    ahead = min(QK_AHEAD, n_early)
    per_tile = tq // tk
    full = slice(0, tq)
    acc_ref[...] = jnp.zeros(acc_ref.shape, F32)
    k_lo = pl.multiple_of(lo * tk, tk)
    for i in range(ahead):
        sbuf_ref[i] = scores(i, k_lo, full)
    ones = jnp.ones((BF16_SUBLANES, tk), BF16)

    def step(c, heads, diag, ms, ls):
        diagonal = diag is not None
        q_lo = diag * tk if diagonal else 0
        cols = slice(q_lo, tq)
        nxt = slice(q_lo + tk, tq) if diagonal else full
        k0 = pl.multiple_of(c * tk, tk)
        k1 = pl.multiple_of(c * tk + tk, tk)
        ms, ls = list(ms), list(ls)
        pend = [sbuf_ref[i, :, cols] for i in range(ahead)]
        for i in range(heads):
            s = pend.pop(0)
            j = i + ahead
            if j < heads:
                pend.append(scores(j, k0, cols))
            elif nxt.start < tq:
                sbuf_ref[j - heads, :, nxt] = scores(j - heads, k1, nxt)
            mk = mask(i, k0, diagonal, q_lo, tq - q_lo)
            if mk is not None:
                s = jnp.where(mk, s, NEG_BIG)
            m_old, l_old = ms[i][:, cols], ls[i][:, cols]
            m_new = jnp.maximum(m_old, jnp.max(s, axis=0, keepdims=True))
            a = jnp.exp2(m_old - m_new)
            p = jnp.exp2(s - m_new)
            pv = _dot(jnp.concatenate([values(i, k0), ones], axis=0), p.astype(BF16))
            rows = _head_rows(i, HEAD_DIM)
            acc_ref[rows, cols] = a * acc_ref[rows, cols] + pv[0:HEAD_DIM]
            l_new = a * l_old + pv[HEAD_DIM:HEAD_DIM + 1]
            if q_lo:
                m_new = jnp.concatenate([ms[i][:, 0:q_lo], m_new], axis=1)
                l_new = jnp.concatenate([ls[i][:, 0:q_lo], l_new], axis=1)
            ms[i], ls[i] = m_new, l_new
        return tuple(ms), tuple(ls)

    def init(count):
        return (tuple(jnp.full((1, tq), NEG_BIG, F32) for _ in range(count)),
                tuple(jnp.zeros((1, tq), F32) for _ in range(count)))

    first_diag = qi * per_tile
    ms, ls = init(n_early)
    if n_early < n:
        ms, ls = lax.fori_loop(lo, join, lambda c, carry: step(c, n_early, None, *carry), (ms, ls))
        late = init(n - n_early)
        ms, ls, lo = ms + late[0], ls + late[1], join
    ms, ls = lax.fori_loop(lo, first_diag, lambda c, carry: step(c, n, None, *carry), (ms, ls))
    for d in range(per_tile):
        ms, ls = step(first_diag + d, n, d, ms, ls)
    return ls


def _pad_queries(q_t, qpad_ref, n, width):
    per = MXU_COLS // width
    row = lax.broadcasted_iota(jnp.int32, (MXU_COLS, q_t.shape[1]), 0)
    for i in range(n):
        g, r = divmod(i, per)
        qg = q_t[g * MXU_COLS:(g + 1) * MXU_COLS]
        qpad_ref[i] = jnp.where((row >= r * width) & (row < (r + 1) * width), qg, jnp.zeros_like(qg))


def _split3(x):
    hi = x.astype(BF16)
    r1 = x - hi.astype(F32)
    mid = r1.astype(BF16)
    lo = (r1 - mid.astype(F32)).astype(BF16)
    return hi, mid, lo


def _fox_kernel(qt_ref, k_ref, vt_ref, fls_ref, tri_ref, place_ref, o_ref,
                kaug_ref, crow_ref, qpad_ref, acc_ref, sbuf_ref, *, tq, tk, seq):
    qi = pl.program_id(1)
    q0 = pl.multiple_of(qi * tq, tq)
    n_q = FOX_GROUP_HEADS * HEAD_DIM
    n_b = MXU_COLS - n_q
    n_groups = FOX_QK_WIDTH // MXU_COLS

    @pl.when(qi == 0)
    def _():
        carry = jnp.zeros((1, LANES), F32)
        for i in range(seq // CUM_TILE):
            rows = slice(i * CUM_TILE, (i + 1) * CUM_TILE)
            c = carry + sum(_dot(tri_ref[...], part) for part in _split3(fls_ref[0, rows, :]))
            carry = c[CUM_TILE - 1:CUM_TILE, :]
            c = c * LOG2E
            crow_ref[:, rows] = c.T[0:FOX_HEADS, :]
            hi, mid, lo = _split3(c)
            bias = (_dot(hi, place_ref[0]) + _dot(mid, place_ref[1]) + _dot(lo, place_ref[2])
                    + place_ref[3, 0:1, :].astype(F32))
            k = k_ref[0, rows, :].astype(F32)
            pieces = []
            for g in range(n_groups):
                kg = k[:, g * n_q:min((g + 1) * n_q, FOX_WIDTH)]
                pieces.append(kg)
                if kg.shape[1] < n_q:
                    pieces.append(jnp.zeros((CUM_TILE, n_q - kg.shape[1]), F32))
                pieces.append(bias[:, g * n_b:(g + 1) * n_b])
            kaug_ref[rows, :] = jnp.concatenate(pieces, axis=1).astype(BF16)

    brow = lax.broadcasted_iota(jnp.int32, (n_b, 1), 0)
    for h in range(FOX_HEADS):
        r = h % FOX_GROUP_HEADS
        bias = jnp.where((brow >= 8 * r) & (brow < 8 * r + 3), 1.0, 0.0)
        for j, part in enumerate(_split3(crow_ref[h:h + 1, pl.ds(q0, tq)])):
            bias = jnp.where(brow == 8 * r + 3 + j, part.astype(F32), bias)
        zero = lambda heads: [jnp.zeros((heads * HEAD_DIM, tq), BF16)] if heads else []
        qpad_ref[h] = jnp.concatenate(zero(r) + [qt_ref[0, _head_rows(h, HEAD_DIM), :]]
                                      + zero(FOX_GROUP_HEADS - 1 - r) + [bias.astype(BF16)], axis=0)

    ls = _attend_t(
        FOX_HEADS, tq, tk, qi, 0,
        lambda h, k0, cols: _dot(kaug_ref[pl.ds(k0, tk), _head_rows(h // FOX_GROUP_HEADS, MXU_COLS)],
                                 qpad_ref[h, :, cols]),
        lambda h, k0: vt_ref[0, _head_rows(h, HEAD_DIM), pl.ds(k0, tk)],
        acc_ref, sbuf_ref, mask=_causal_mask(tk, q0))
    for h in range(FOX_HEADS):
        rows = _head_rows(h, HEAD_DIM)
        acc_ref[rows, :] = acc_ref[rows, :] * (1.0 / ls[h])
    o_ref[0] = acc_ref[...].T.astype(BF16)


def _fox_attention(fq_t, fk, fv_t, fls, tri, place):
    b, w, s = fv_t.shape
    wk = FOX_QK_WIDTH
    tq, tk = ATT_Q_TILE, ATT_K_TILE
    return pl.pallas_call(
        functools.partial(_fox_kernel, tq=tq, tk=tk, seq=s),
        grid=(b, s // tq),
        in_specs=[
            pl.BlockSpec((1, w, tq), lambda i, j: (i, 0, j)),
            pl.BlockSpec((1, s, w), lambda i, j: (i, 0, 0)),
            pl.BlockSpec((1, w, s), lambda i, j: (i, 0, 0)),
            pl.BlockSpec((1, s, LANES), lambda i, j: (i, 0, 0)),
            pl.BlockSpec((CUM_TILE, CUM_TILE), lambda i, j: (0, 0)),
            pl.BlockSpec(place.shape, lambda i, j: (0, 0, 0)),
        ],
        out_specs=pl.BlockSpec((1, tq, w), lambda i, j: (i, j, 0)),
        out_shape=jax.ShapeDtypeStruct((b, s, w), BF16),
        scratch_shapes=[pltpu.VMEM((s, wk), BF16), pltpu.VMEM((FOX_HEADS, s), F32),
                        pltpu.VMEM((FOX_HEADS, MXU_COLS, tq), BF16), pltpu.VMEM((w, tq), F32),
                        pltpu.VMEM((QK_AHEAD, tk, tq), F32)],
        compiler_params=_cparams("parallel", "arbitrary"),
        name="fox_attention",
    )(fq_t, fk, fv_t, fls, tri, place)


def _diff_kernel(qt_ref, k_ref, vt_ref, lam_ref, g_ref, o_ref, qpad_ref, acc_ref, sbuf_ref, *, tq, tk, lambda_init):
    qi = pl.program_id(1)
    _pad_queries(qt_ref[0], qpad_ref, DIFF_MAPS, DIFF_QK_DIM)
    ls = _attend_t(
        DIFF_MAPS, tq, tk, qi, 0,
        lambda i, k0, cols: _dot(k_ref[0, pl.ds(k0, tk), :], qpad_ref[i, :, cols]),
        lambda i, k0: vt_ref[0, _head_rows(i // 2, HEAD_DIM), pl.ds(k0, tk)],
        acc_ref, sbuf_ref, mask=_causal_mask(tk, qi * tq))
    lp = lam_ref[...]
    lam = (jnp.exp(jnp.sum(lp[0:1] * lp[1:2], axis=1, keepdims=True))
           - jnp.exp(jnp.sum(lp[2:3] * lp[3:4], axis=1, keepdims=True)) + lambda_init)
    heads = []
    for h in range(DIFF_HEADS):
        o1 = acc_ref[_head_rows(2 * h, HEAD_DIM), :] * (1.0 / ls[2 * h])
        o2 = acc_ref[_head_rows(2 * h + 1, HEAD_DIM), :] * (1.0 / ls[2 * h + 1])
        o = o1 - lam * o2
        o = o * lax.rsqrt(jnp.mean(o * o, axis=0, keepdims=True) + LN_EPS)
        heads.append(o * (1.0 - lambda_init))
    o_t = jnp.concatenate(heads, axis=0)
    o_ref[0] = (o_t.T * g_ref[...]).astype(BF16)


def _diff_attention(dq_t, dk, dv_t, lam_params, subln_g, lambda_init):
    b, w, s = dq_t.shape
    tq, tk = ATT_Q_TILE, ATT_K_TILE
    return pl.pallas_call(
        functools.partial(_diff_kernel, tq=tq, tk=tk, lambda_init=lambda_init),
        grid=(b, s // tq),
        in_specs=[
            pl.BlockSpec((1, w, tq), lambda i, j: (i, 0, j)),
            pl.BlockSpec((1, s, w), lambda i, j: (i, 0, 0)),
            pl.BlockSpec((1, w, s), lambda i, j: (i, 0, 0)),
            pl.BlockSpec((4, DIFF_QK_DIM), lambda i, j: (0, 0)),
            pl.BlockSpec((1, w), lambda i, j: (0, 0)),
        ],
        out_specs=pl.BlockSpec((1, tq, w), lambda i, j: (i, j, 0)),
        out_shape=jax.ShapeDtypeStruct((b, s, w), BF16),
        scratch_shapes=[pltpu.VMEM((DIFF_MAPS, MXU_COLS, tq), BF16), pltpu.VMEM((DIFF_MAPS * HEAD_DIM, tq), F32),
                        pltpu.VMEM((QK_AHEAD, tk, tq), F32)],
        compiler_params=_cparams("parallel", "arbitrary"),
        name="diff_attention",
    )(dq_t, dk, dv_t, lam_params, subln_g)


def _nsa_cmp_scores(qt_ref, kc_ref, t0, width):
    cols = slice(t0, t0 + width)
    return [_dot(kc_ref[0], qt_ref[0, _head_rows(h, HEAD_DIM), cols]) for h in range(NSA_HEADS)]


def _nsa_select(scores, vct_ref, g_ref, ovl_ref, sel_ref, ocmp_ref, t0, width):
    cols = slice(t0, t0 + width)
    t_q = t0 + lax.broadcasted_iota(jnp.int32, (1, width), 1)
    block_end = CMP_STRIDE * lax.broadcasted_iota(jnp.int32, (N_CMP, 1), 0) + (CMP_BLOCK - 1)
    vis = block_end <= t_q
    any_vis = t_q >= CMP_BLOCK - 1
    lhs = jnp.concatenate([vct_ref[0], ovl_ref[...], jnp.ones((BF16_SUBLANES, N_CMP), BF16)], axis=0)
    imp = jnp.zeros((N_SLC, width), F32)
    for h in range(NSA_HEADS):
        rows = _head_rows(h, HEAD_DIM)
        s = jnp.where(vis, scores[h], NEG_BIG)
        p = jnp.exp2(s - jnp.max(s, axis=0, keepdims=True))
        r = _dot(lhs, p.astype(BF16))
        l = r[HEAD_DIM + N_SLC:HEAD_DIM + N_SLC + 1]
        scale = jnp.where(any_vis, 1.0 / l, 0.0)
        ocmp_ref[rows, cols] = (g_ref[0, 3 * h:3 * h + 1, cols] * scale) * r[0:HEAD_DIM]
        imp = imp + scale * r[HEAD_DIM:HEAD_DIM + N_SLC]
    j_idx = lax.broadcasted_iota(jnp.int32, (N_SLC, 1), 0)
    blk_t = t_q // SLC_BLOCK
    forced = (j_idx == 0) | (j_idx == blk_t) | (j_idx == blk_t - 1)
    valid = j_idx * SLC_BLOCK <= t_q
    score = jnp.where(forced, 1e9, jnp.where(valid, imp, -1.0))
    groups = [score[g * 8:(g + 1) * 8] for g in range(N_SLC // 8)]
    ranks = [jnp.zeros((8, width), F32) for _ in groups]
    row = lax.broadcasted_iota(jnp.int32, (8, 1), 0)
    for i in range(N_SLC):
        gi, ri = divmod(i, 8)
        si = groups[gi][ri:ri + 1, :]
        for g, sg in enumerate(groups):
            if g < gi:
                ranks[g] = ranks[g] + jnp.where(si > sg, 1.0, 0.0)
            elif g > gi:
                ranks[g] = ranks[g] + jnp.where(si >= sg, 1.0, 0.0)
            else:
                tie = jnp.where(row > ri, 1.0, 0.0)
                ranks[g] = ranks[g] + jnp.where(si > sg, 1.0, 0.0) + jnp.where(si == sg, tie, 0.0)
    rank = jnp.concatenate(ranks, axis=0)
    sel_ref[:, cols] = jnp.where(rank < float(SLC_TOPK), 0.0, NEG_BIG).astype(BF16)


def _nsa_kernel(qt_ref, kc_ref, vct_ref, ks_ref, vst_ref, kw_ref, vwt_ref, g_ref, ovl_ref,
                o_ref, sel_ref, ocmp_ref, qaug_ref, acc_ref, out_ref, sbuf_ref, *, tq, tk, seq):
    qi = pl.program_id(1)
    q0 = pl.multiple_of(qi * tq, tq)

    @pl.when(qi == 0)
    def _():
        nxt = _nsa_cmp_scores(qt_ref, kc_ref, 0, NSA_PRE_TILE)
        for t0 in range(0, seq, NSA_PRE_TILE):
            scores = nxt
            if t0 + NSA_PRE_TILE < seq:
                nxt = _nsa_cmp_scores(qt_ref, kc_ref, t0 + NSA_PRE_TILE, NSA_PRE_TILE)
            _nsa_select(scores, vct_ref, g_ref, ovl_ref, sel_ref, ocmp_ref, t0, NSA_PRE_TILE)

    pad = jnp.zeros((LANES - HEAD_DIM - N_SLC, tq), BF16)
    sel_neg = sel_ref[:, pl.ds(q0, tq)]
    for h in range(NSA_HEADS):
        qaug_ref[h] = jnp.concatenate([qt_ref[0, _head_rows(h, HEAD_DIM), pl.ds(q0, tq)], sel_neg, pad], axis=0)

    def scores(i, k0, cols):
        k_ref = ks_ref if i < NSA_HEADS else kw_ref
        return _dot(k_ref[0, pl.ds(k0, tk), :], qaug_ref[i % NSA_HEADS, :, cols])

    def values(i, k0):
        vt_ref = vst_ref if i < NSA_HEADS else vwt_ref
        return vt_ref[0, :, pl.ds(k0, tk)]

    def mask(i, k0, diagonal, q_lo, width):
        d, off = _key_minus_query(tk, width, k0, q0 + q_lo)
        if diagonal:
            return d <= off
        return None if i < NSA_HEADS else d > off - WINDOW

    ls = _attend_t(2 * NSA_HEADS, tq, tk, qi, 0, scores, values, acc_ref, sbuf_ref, mask=mask,
                   n_early=NSA_HEADS, join=jnp.maximum(q0 - WINDOW, 0) // tk)
    for h in range(NSA_HEADS):
        rows = _head_rows(h, HEAD_DIM)
        out = ocmp_ref[rows, pl.ds(q0, tq)]
        for branch in (1, 2):
            i = (branch - 1) * NSA_HEADS + h
            gate = g_ref[0, 3 * h + branch:3 * h + branch + 1, pl.ds(q0, tq)]
            out = out + (gate * (1.0 / ls[i])) * acc_ref[_head_rows(i, HEAD_DIM), :]
        out_ref[rows, :] = out
    o_ref[0] = out_ref[...].T.astype(BF16)


def _nsa_attention(nq_t, kc, vc_t, kslc, vslc_t, kwin, vwin_t, gates_t, ovl):
    b, w, s = nq_t.shape
    tq, tk = NSA_Q_TILE, ATT_K_TILE
    seq = lambda i, j: (i, 0, 0)
    return pl.pallas_call(
        functools.partial(_nsa_kernel, tq=tq, tk=tk, seq=s),
        grid=(b, s // tq),
        in_specs=[
            pl.BlockSpec((1, w, s), seq),
            pl.BlockSpec((1, N_CMP, HEAD_DIM), seq),
            pl.BlockSpec((1, HEAD_DIM, N_CMP), seq),
            pl.BlockSpec((1, s, LANES), seq),
            pl.BlockSpec((1, HEAD_DIM, s), seq),
            pl.BlockSpec((1, s, LANES), seq),
            pl.BlockSpec((1, HEAD_DIM, s), seq),
            pl.BlockSpec((1, GATE_ROWS, s), seq),
            pl.BlockSpec((N_SLC, N_CMP), lambda i, j: (0, 0)),
        ],
        out_specs=pl.BlockSpec((1, tq, w), lambda i, j: (i, j, 0)),
        out_shape=jax.ShapeDtypeStruct((b, s, w), BF16),
        scratch_shapes=[pltpu.VMEM((N_SLC, s), BF16), pltpu.VMEM((w, s), F32),
                        pltpu.VMEM((NSA_HEADS, LANES, tq), BF16), pltpu.VMEM((2 * w, tq), F32),
                        pltpu.VMEM((w, tq), F32), pltpu.VMEM((QK_AHEAD, tk, tq), F32)],
        compiler_params=_cparams("parallel", "arbitrary"),
        name="nsa_attention",
    )(nq_t, kc, vc_t, kslc, vslc_t, kwin, vwin_t, gates_t, ovl)


def _out_ln_kernel(x_ref, on_ref, od_ref, of_ref, w_ref, lng_ref, lnb_ref, o_ref):
    o = jnp.concatenate([on_ref[0], od_ref[0], of_ref[0]], axis=1)
    o_ref[...] = _layer_norm(DEEPNORM_ALPHA * x_ref[...] + _dot(o, w_ref[...]), lng_ref[...], lnb_ref[...])


def _out_ln(x, o_nsa, o_diff, o_fox, w, ln_g, ln_b):
    t = x.shape[0]
    tm = STEP_ROWS
    per_seq = o_nsa.shape[1] // tm
    row = lambda i: (i, 0)
    seq = lambda i: (i // per_seq, i % per_seq, 0)
    c2 = lambda i: (0, 0)
    return pl.pallas_call(
        _out_ln_kernel,
        grid=(t // tm,),
        in_specs=[
            pl.BlockSpec((tm, D_MODEL), row),
            pl.BlockSpec((1, tm, NSA_WIDTH), seq),
            pl.BlockSpec((1, tm, DIFF_WIDTH), seq),
            pl.BlockSpec((1, tm, FOX_WIDTH), seq),
            pl.BlockSpec((NSA_WIDTH + DIFF_WIDTH + FOX_WIDTH, D_MODEL), c2),
            pl.BlockSpec((1, D_MODEL), c2),
            pl.BlockSpec((1, D_MODEL), c2),
        ],
        out_specs=pl.BlockSpec((tm, D_MODEL), row),
        out_shape=jax.ShapeDtypeStruct((t, D_MODEL), F32),
        compiler_params=_cparams("parallel"),
        name="out_ln",
    )(x, o_nsa, o_diff, o_fox, w, ln_g, ln_b)


def _prep_ffn(wg, wu, wd):
    return wg.astype(BF16), wu.astype(BF16), wd.astype(BF16)


def _rot_cols(w, d):
    k, n = w.shape
    w = w.reshape(k, n // d, d)
    return jnp.concatenate([-w[..., d // 2:], w[..., :d // 2]], axis=-1).reshape(k, n)


def _rope_table(pos, d, signed=False):
    inv = ROPE_THETA ** (-jnp.arange(0, d, 2, dtype=F32) / d)
    ang = pos.astype(F32)[:, None] * inv[None, :]
    cos = jnp.concatenate([jnp.cos(ang), jnp.cos(ang)], axis=-1)
    sin = jnp.concatenate([-jnp.sin(ang) if signed else jnp.sin(ang), jnp.sin(ang)], axis=-1)
    return cos, sin


def _fox_placement():
    n_b = MXU_COLS - FOX_GROUP_HEADS * HEAD_DIM
    place = np.zeros((4, LANES, (FOX_QK_WIDTH // MXU_COLS) * n_b), np.float32)
    for h in range(FOX_HEADS):
        g, r = divmod(h, FOX_GROUP_HEADS)
        for j in range(3):
            place[j, h, g * n_b + 8 * r + j] = -1.0
            place[3, 0, g * n_b + 8 * r + 3 + j] = 1.0
    return jnp.asarray(place, dtype=BF16)


def _prep_in_proj(w_in, fox_b_f, s):
    o = 0
    nsa_q = w_in[:, o:o + NSA_WIDTH]; o += NSA_WIDTH
    kv = [w_in[:, o + i * HEAD_DIM:o + (i + 1) * HEAD_DIM] for i in range(6)]; o += 6 * HEAD_DIM
    k_cmp, v_cmp, k_slc, v_slc, k_win, v_win = kv
    nsa_g = w_in[:, o:o + N_NSA_GATES]; o += N_NSA_GATES
    diff_q = w_in[:, o:o + DIFF_WIDTH]; o += DIFF_WIDTH
    diff_k = w_in[:, o:o + DIFF_WIDTH]; o += DIFF_WIDTH
    diff_v = w_in[:, o:o + DIFF_WIDTH]; o += DIFF_WIDTH
    fox_q = w_in[:, o:o + FOX_WIDTH]; o += FOX_WIDTH
    fox_k = w_in[:, o:o + FOX_WIDTH]; o += FOX_WIDTH
    fox_v = w_in[:, o:o + FOX_WIDTH]; o += FOX_WIDTH
    fox_f = w_in[:, o:o + FOX_HEADS]
    zero = jnp.zeros((D_MODEL, LANES - HEAD_DIM), F32)

    wa = jnp.concatenate([k_slc, zero, k_win, zero, diff_k], axis=1)
    wb = jnp.concatenate([fox_k, k_cmp, v_cmp], axis=1)
    wc = jnp.concatenate([fox_f, jnp.zeros((D_MODEL, LANES - FOX_HEADS), F32)], axis=1)
    bc = jnp.concatenate([fox_b_f, jnp.zeros((LANES - FOX_HEADS,), F32)])[None, :]
    wd = jnp.concatenate([nsa_q, diff_q], axis=1).T
    we = jnp.concatenate([v_slc, v_win, diff_v, fox_q * (HEAD_DIM ** -0.5 * LOG2E), fox_v], axis=1).T
    wf = jnp.concatenate([nsa_g, jnp.zeros((D_MODEL, GATE_ROWS - N_NSA_GATES), F32)], axis=1).T

    pos = jnp.arange(s, dtype=jnp.int32)
    c64, s64 = _rope_table(pos, HEAD_DIM, signed=True)
    c32, s32 = _rope_table(pos, DIFF_QK_DIM, signed=True)
    nsa_scale = HEAD_DIM ** -0.5 * LOG2E
    diff_scale = DIFF_QK_DIM ** -0.5 * LOG2E

    def tm_table(t64, t32):
        return jnp.concatenate([t64, t64, t64, t64, jnp.tile(t32, (1, DIFF_MAPS))], axis=1)

    def fm_table(t64, t32):
        return jnp.concatenate([jnp.tile(t64, (1, NSA_HEADS)) * nsa_scale,
                                jnp.tile(t32, (1, DIFF_MAPS)) * diff_scale], axis=1).T

    bf = lambda w: w.astype(BF16)
    return (bf(wa), bf(wb), bf(wc), bf(wd), bf(we), bf(wf),
            tm_table(c64, c32), tm_table(s64, s32), fm_table(c64, c32), fm_table(s64, s32), bc)


def _prep_compress(pos_k, pos_v, phi_k1, phi_k2, phi_v1, phi_v2):
    half = CMP_STRIDE * HEAD_DIM

    def pe(p):
        return jnp.broadcast_to(p.reshape(2, 1, half), (2, 8, half)).astype(BF16)

    block_end = jnp.arange(N_CMP, dtype=jnp.int32) * CMP_STRIDE + (CMP_BLOCK - 1)
    cos_c, sin_c = _rope_table(block_end, HEAD_DIM)
    return (pe(pos_k), pe(pos_v), phi_k1.reshape(2, half, CMP_HIDDEN).astype(BF16), phi_k2.astype(BF16),
            _rot_cols(phi_k2, HEAD_DIM).astype(BF16), phi_v1.reshape(2, half, CMP_HIDDEN).astype(BF16),
            phi_v2.T.astype(BF16), cos_c, sin_c)


def _overlap_matrix(s):
    c0 = np.arange(N_CMP) * CMP_STRIDE
    s0 = np.arange(N_SLC) * SLC_BLOCK
    ovl = (c0[None, :] < s0[:, None] + SLC_BLOCK) & (c0[None, :] + CMP_BLOCK > s0[:, None])
    ovl[:, (s - CMP_BLOCK) // CMP_STRIDE + 1:] = False
    return jnp.asarray(ovl.astype(np.float32), dtype=BF16)


def kernel(x, p, ln_g, ln_b, ffn1_w_gate, ffn1_w_up, ffn1_w_down, ffn2_w_gate, ffn2_w_up, ffn2_w_down, w_in, fox_b_f, nsa_pos_k, nsa_pos_v, nsa_phi_k1, nsa_phi_k2, nsa_phi_v1, nsa_phi_v2, diff_lambda, diff_subln_g, w_out, ple_w_gate, ple_b_gate, ple_w_proj):
    b, s, _ = x.shape
    assert s // SLC_BLOCK == N_SLC and (s - CMP_BLOCK) // CMP_STRIDE + 1 <= N_CMP
    t = b * s
    ovl = _overlap_matrix(s)
    tri = jnp.asarray(np.tril(np.ones((CUM_TILE, CUM_TILE), np.float32)), dtype=BF16)
    place = _fox_placement()
    x = x.reshape(t, D_MODEL)
    for i in range(DEPTH):
        lambda_init = 0.8 - 0.6 * math.exp(-0.3 * i)
        lng = ln_g[i][:, None, :]
        lnb = ln_b[i][:, None, :]
        x = _ffn_ln(x, *_prep_ffn(ffn1_w_gate[i], ffn1_w_up[i], ffn1_w_down[i]), lng[0], lnb[0])
        proj = _in_proj(x.reshape(b, s, D_MODEL), *_prep_in_proj(w_in[i], fox_b_f[i], s))
        kslc, kwin, dk, fk, kcmp, vcmp, fls, nq_t, dq_t, vslc_t, vwin_t, dv_t, fq_t, fv_t, gates_t = proj
        half = CMP_STRIDE * HEAD_DIM
        kc, vc_t = _compress(kcmp.reshape(b, s // CMP_STRIDE, half), vcmp.reshape(b, s // CMP_STRIDE, half),
                             *_prep_compress(nsa_pos_k[i], nsa_pos_v[i], nsa_phi_k1[i], nsa_phi_k2[i],
                                             nsa_phi_v1[i], nsa_phi_v2[i]))
        o_nsa = _nsa_attention(nq_t, kc, vc_t, kslc, vslc_t, kwin, vwin_t, gates_t, ovl)
        o_diff = _diff_attention(dq_t, dk, dv_t, diff_lambda[i], jnp.tile(diff_subln_g[i], DIFF_HEADS)[None, :],
                                 lambda_init)
        o_fox = _fox_attention(fq_t, fk, fv_t, fls, tri, place)
        x = _out_ln(x, o_nsa, o_diff, o_fox, w_out[i].astype(BF16), lng[1], lnb[1])
        ple = (p, i, ple_w_gate[i].astype(BF16), ple_b_gate[i][None, :],
               ple_w_proj[i].astype(BF16))
        x = _ffn_ln(x, *_prep_ffn(ffn2_w_gate[i], ffn2_w_up[i], ffn2_w_down[i]), lng[2], lnb[2], ple=ple)
    return x.reshape(b, s, D_MODEL)
```
